```python
import jax, jax.numpy as jnp
from jax import lax
import numpy as np

D_MODEL = 1024
BATCH = 8
SEQ = 2048
DEPTH = 2

GRID_W = 64
CTX_LEN = 256
HEAD_DIM = 64
FOURIER_GROUPS = 4
FOURIER_GROUP_CH = 64
FOURIER_W = FOURIER_GROUPS * FOURIER_GROUP_CH
G_HEADS = 8
G_KV = 2
G_GROUP = G_HEADS // G_KV
W_HEADS = 4
W_KV = 2
W_GROUP = W_HEADS // W_KV
WINDOW = 128
Q_BLOCK = 128
N_BRANCH = 3
SPLIT_SIZES = (FOURIER_W,
               G_HEADS * HEAD_DIM, G_KV * HEAD_DIM, G_KV * HEAD_DIM,
               W_HEADS * HEAD_DIM, W_KV * HEAD_DIM, W_KV * HEAD_DIM,
               N_BRANCH * D_MODEL)
IN_W = sum(SPLIT_SIZES)
ROPE_THETA = 10000.0
N_EXPERTS = 16
CAPACITY_FACTOR = 2
EXPERT_FF = D_MODEL
EPS = 1e-6
NEG_INF = -1e30

kernel_name = "hybrid_dit_fourier_gqa_window_ecmoe"


def rms_norm(x, g):
    xf = x.astype(jnp.float32)
    y = xf * lax.rsqrt(jnp.mean(jnp.square(xf), axis=-1, keepdims=True) + EPS)
    return (y * g.astype(jnp.float32)).astype(x.dtype)


def modulate(h, shift, scale):
    return h * (1 + scale[:, None, :]) + shift[:, None, :]


def heads(z, n):
    return z.reshape(z.shape[:-1] + (n, HEAD_DIM))


def split_proj(z):
    points = np.cumsum(SPLIT_SIZES)[:-1].tolist()
    return jnp.split(z, points, axis=-1)


def axial_rope_angles(n_tokens):
    rows = n_tokens // GRID_W
    r, col = jnp.meshgrid(jnp.arange(rows), jnp.arange(GRID_W), indexing="ij")
    half = HEAD_DIM // 2
    inv = ROPE_THETA ** (-jnp.arange(0, half, 2, dtype=jnp.float32) / half)
    ang = jnp.concatenate([r.reshape(-1, 1).astype(jnp.float32) * inv,
                           col.reshape(-1, 1).astype(jnp.float32) * inv], axis=-1)
    return jnp.cos(ang), jnp.sin(ang)


def apply_rope(x, cos, sin):
    xf = x.astype(jnp.float32).reshape(x.shape[:-1] + (HEAD_DIM // 2, 2))
    x0, x1 = xf[..., 0], xf[..., 1]
    cs, sn = cos[None, :, None, :], sin[None, :, None, :]
    out = jnp.stack([x0 * cs - x1 * sn, x0 * sn + x1 * cs], axis=-1)
    return out.reshape(x.shape).astype(x.dtype)


def attend(q, k, v):
    s = jnp.einsum("bqhgd,bkhd->bhgqk", q, k).astype(jnp.float32) * (HEAD_DIM ** -0.5)
    p = jax.nn.softmax(s, axis=-1).astype(v.dtype)
    return jnp.einsum("bhgqk,bkhd->bqhgd", p, v)


def global_attention(q_l, k_l, v_l, k_c, v_c):
    B, T = q_l.shape[:2]
    nb = T // Q_BLOCK
    k_all = jnp.concatenate([k_c, k_l], axis=1)
    v_all = jnp.concatenate([v_c, v_l], axis=1)
    qb = q_l.reshape(B, nb, Q_BLOCK, G_KV, G_GROUP, HEAD_DIM).transpose(1, 0, 2, 3, 4, 5)
    ob = lax.map(lambda qq: attend(qq, k_all, v_all), qb)
    return ob.transpose(1, 0, 2, 3, 4, 5).reshape(B, T, G_HEADS * HEAD_DIM)


def window_attention(q_l, k_l, v_l, k_c, v_c, sink):
    B, T = q_l.shape[:2]
    L = k_c.shape[1]
    nb = T // Q_BLOCK
    q = q_l.reshape(B, nb, Q_BLOCK, W_KV, W_GROUP, HEAD_DIM)

    def band(t):
        tp = jnp.pad(t, ((0, 0), (Q_BLOCK, Q_BLOCK), (0, 0), (0, 0)))
        tp = tp.reshape(B, nb + 2, Q_BLOCK, W_KV, HEAD_DIM)
        return jnp.concatenate([tp[:, :-2], tp[:, 1:-1], tp[:, 2:]], axis=2)

    kb, vb = band(k_l), band(v_l)
    scale = HEAD_DIM ** -0.5
    s_c = jnp.einsum("bnqhgd,bkhd->bnhgqk", q, k_c).astype(jnp.float32) * scale
    s_b = jnp.einsum("bnqhgd,bnkhd->bnhgqk", q, kb).astype(jnp.float32) * scale
    qpos = jnp.arange(Q_BLOCK)[:, None]
    kpos = jnp.arange(3 * Q_BLOCK)[None, :]
    rel = qpos - kpos + Q_BLOCK
    key_abs = (jnp.arange(nb)[:, None, None] - 1) * Q_BLOCK + kpos[None]
    valid = (jnp.abs(rel)[None] <= WINDOW) & (key_abs >= 0) & (key_abs < T)
    s_b = jnp.where(valid[None, :, None, None], s_b, NEG_INF)
    s_sink = jnp.broadcast_to(sink.astype(jnp.float32).reshape(1, 1, W_KV, W_GROUP, 1, 1),
                              s_c.shape[:-1] + (1,))
    p = jax.nn.softmax(jnp.concatenate([s_c, s_b, s_sink], axis=-1), axis=-1).astype(v_l.dtype)
    o = (jnp.einsum("bnhgqk,bkhd->bnqhgd", p[..., :L], v_c)
         + jnp.einsum("bnhgqk,bnkhd->bnqhgd", p[..., L:L + 3 * Q_BLOCK], vb))
    return o.reshape(B, T, W_HEADS * HEAD_DIM)


def sink_attention_ctx(q_c, k_c, v_c, sink):
    B, L = q_c.shape[:2]
    q = q_c.reshape(B, L, W_KV, W_GROUP, HEAD_DIM)
    s = jnp.einsum("bqhgd,bkhd->bhgqk", q, k_c).astype(jnp.float32) * (HEAD_DIM ** -0.5)
    s_sink = jnp.broadcast_to(sink.astype(jnp.float32).reshape(1, W_KV, W_GROUP, 1, 1),
                              s.shape[:-1] + (1,))
    p = jax.nn.softmax(jnp.concatenate([s, s_sink], axis=-1), axis=-1)[..., :L].astype(v_c.dtype)
    o = jnp.einsum("bhgqk,bkhd->bqhgd", p, v_c)
    return o.reshape(B, L, W_HEADS * HEAD_DIM)


def fourier_mix(u):
    B, N, _ = u.shape
    g = u.astype(jnp.float32).reshape(B, N, FOURIER_GROUPS, FOURIER_GROUP_CH)
    f = jnp.fft.fft2(g, axes=(1, 3), norm="ortho").real
    return f.reshape(B, N, FOURIER_W).astype(u.dtype)


def gated_merge(gate_cols, f_mix, o_g, o_w, w_br_fourier, w_br_global, w_br_window, w_out):
    g = jax.nn.sigmoid(gate_cols).reshape(gate_cols.shape[:-1] + (N_BRANCH, D_MODEL))
    m = (g[..., 0, :] * (f_mix @ w_br_fourier)
         + g[..., 1, :] * (o_g @ w_br_global)
         + g[..., 2, :] * (o_w @ w_br_window))
    return m @ w_out


def mixer_sublayer(h, hc, cos, sin, w_in, q_norm_g, k_norm_g, sink,
                   w_br_fourier, w_br_global, w_br_window, w_out, need_ctx):
    B, T, _ = h.shape
    L = hc.shape[1]
    lf, lgq, lgk, lgv, lwq, lwk, lwv, lgate = split_proj(h @ w_in)
    cf, cgq, cgk, cgv, cwq, cwk, cwv, cgate = split_proj(hc @ w_in)
    kc_g = rms_norm(heads(cgk, G_KV), k_norm_g)
    vc_g = heads(cgv, G_KV)
    kc_w = heads(cwk, W_KV)
    vc_w = heads(cwv, W_KV)
    q_g = apply_rope(rms_norm(heads(lgq, G_HEADS), q_norm_g), cos, sin)
    k_g = apply_rope(rms_norm(heads(lgk, G_KV), k_norm_g), cos, sin)
    o_g = global_attention(q_g, k_g, heads(lgv, G_KV), kc_g, vc_g)
    q_w = apply_rope(heads(lwq, W_HEADS), cos, sin)
    k_w = apply_rope(heads(lwk, W_KV), cos, sin)
    o_w = window_attention(q_w, k_w, heads(lwv, W_KV), kc_w, vc_w, sink)
    y = gated_merge(lgate, fourier_mix(lf), o_g, o_w,
                    w_br_fourier, w_br_global, w_br_window, w_out)
    if not need_ctx:
        return y, None
    qc_g = rms_norm(heads(cgq, G_HEADS), q_norm_g).reshape(B, L, G_KV, G_GROUP, HEAD_DIM)
    oc_g = attend(qc_g, kc_g, vc_g).reshape(B, L, G_HEADS * HEAD_DIM)
    oc_w = sink_attention_ctx(heads(cwq, W_HEADS), kc_w, vc_w, sink)
    yc = gated_merge(cgate, fourier_mix(cf), oc_g, oc_w,
                     w_br_fourier, w_br_global, w_br_window, w_out)
    return y, yc


def expert_choice_ffn(h, w_router, w_gate_e, w_up_e, w_down_e):
    B, N, _ = h.shape
    cap = CAPACITY_FACTOR * N // N_EXPERTS
    aff = jax.nn.softmax((h @ w_router).astype(jnp.float32), axis=-1)
    top_aff, top_idx = lax.top_k(aff.transpose(0, 2, 1), cap)
    bidx = jnp.arange(B)[:, None, None]
    xg = h[bidx, top_idx]
    a = jnp.einsum("becd,edf->becf", xg, w_gate_e)
    u = jnp.einsum("becd,edf->becf", xg, w_up_e)
    out = jnp.einsum("becf,efd->becd", jax.nn.silu(a) * u, w_down_e) * top_aff[..., None].astype(h.dtype)
    return jnp.zeros_like(h).at[bidx, top_idx].add(out)


def setup_inputs(seed: int = 0) -> dict:
    key = jax.random.key(seed)
    ks = jax.random.split(key, 24)
    D, F, E = D_MODEL, EXPERT_FF, N_EXPERTS
    nrm = jax.random.normal
    f32 = jnp.float32
    return {
        "x": nrm(ks[0], (BATCH, SEQ, D), f32),
        "c": nrm(ks[1], (BATCH, D), f32),
        "ctx": nrm(ks[2], (BATCH, CTX_LEN, D), f32),
        "c_ctx": nrm(ks[3], (D,), f32),
        "w_ada": nrm(ks[4], (DEPTH, D, 6 * D), f32) * (0.5 * D ** -0.5),
        "b_ada": nrm(ks[5], (DEPTH, 6 * D), f32) * 0.01,
        "norm1_g": 1.0 + 0.05 * nrm(ks[6], (DEPTH, D), f32),
        "w_in": nrm(ks[7], (DEPTH, D, IN_W), f32) * D ** -0.5,
        "q_norm_g": 1.0 + 0.05 * nrm(ks[8], (DEPTH, HEAD_DIM), f32),
        "k_norm_g": 1.0 + 0.05 * nrm(ks[9], (DEPTH, HEAD_DIM), f32),
        "sink": 0.5 * nrm(ks[10], (DEPTH, W_HEADS), f32),
        "w_br_fourier": nrm(ks[11], (DEPTH, FOURIER_W, D), f32) * FOURIER_W ** -0.5,
        "w_br_global": nrm(ks[12], (DEPTH, G_HEADS * HEAD_DIM, D), f32) * (G_HEADS * HEAD_DIM) ** -0.5,
        "w_br_window": nrm(ks[13], (DEPTH, W_HEADS * HEAD_DIM, D), f32) * (W_HEADS * HEAD_DIM) ** -0.5,
        "w_out": nrm(ks[14], (DEPTH, D, D), f32) * D ** -0.5,
        "norm2_g": 1.0 + 0.05 * nrm(ks[15], (DEPTH, D), f32),
        "w_router": nrm(ks[16], (DEPTH, D, E), f32) * D ** -0.5,
        "w_gate_e": nrm(ks[17], (DEPTH, E, D, F), f32) * D ** -0.5,
        "w_up_e": nrm(ks[18], (DEPTH, E, D, F), f32) * D ** -0.5,
        "w_down_e": nrm(ks[19], (DEPTH, E, F, D), f32) * F ** -0.5,
        "final_g": 1.0 + 0.05 * nrm(ks[20], (D,), f32),
    }


def reference(x, c, ctx, c_ctx, w_ada, b_ada, norm1_g, w_in, q_norm_g, k_norm_g, sink,
              w_br_fourier, w_br_global, w_br_window, w_out, norm2_g, w_router,
              w_gate_e, w_up_e, w_down_e, final_g):
    T = x.shape[1]
    cos, sin = axial_rope_angles(T)
    xc = ctx
    silu_c = jax.nn.silu(c)
    silu_cc = jax.nn.silu(c_ctx)[None]
    for l in range(DEPTH):
        need_ctx = l < DEPTH - 1
        mod = silu_c @ w_ada[l] + b_ada[l]
        mod_c = silu_cc @ w_ada[l] + b_ada[l]
        sh1, sc1, g1, sh2, sc2, g2 = jnp.split(mod, 6, axis=-1)
        csh1, csc1, cg1, csh2, csc2, cg2 = jnp.split(mod_c, 6, axis=-1)
        h = modulate(rms_norm(x, norm1_g[l]), sh1, sc1)
        hc = modulate(rms_norm(xc, norm1_g[l]), csh1, csc1)
        y, yc = mixer_sublayer(h, hc, cos, sin, w_in[l], q_norm_g[l], k_norm_g[l], sink[l],
                               w_br_fourier[l], w_br_global[l], w_br_window[l], w_out[l], need_ctx)
        x = x + g1[:, None, :] * y
        h = modulate(rms_norm(x, norm2_g[l]), sh2, sc2)
        x = x + g2[:, None, :] * expert_choice_ffn(h, w_router[l], w_gate_e[l], w_up_e[l], w_down_e[l])
        if need_ctx:
            xc = xc + cg1[:, None, :] * yc
            hc = modulate(rms_norm(xc, norm2_g[l]), csh2, csc2)
            xc = xc + cg2[:, None, :] * expert_choice_ffn(hc, w_router[l], w_gate_e[l], w_up_e[l], w_down_e[l])
    return rms_norm(x, final_g)
```

```python
import functools

import numpy as np
import jax
import jax.numpy as jnp
from jax import lax
from jax.experimental import pallas as pl
from jax.experimental.pallas import tpu as pltpu

F32 = jnp.float32
BF16 = jnp.bfloat16

D_MODEL = 1024
HEAD_DIM = 64
GRID_W = 64
FOURIER_GROUPS = 4
FOURIER_GROUP_CH = 64
FOURIER_W = FOURIER_GROUPS * FOURIER_GROUP_CH
G_HEADS, G_KV = 8, 2
G_GROUP = G_HEADS // G_KV
W_HEADS, W_KV = 4, 2
W_GROUP = W_HEADS // W_KV
WINDOW = 128
Q_BLOCK = 128
N_BRANCH = 3
GQ_W = G_HEADS * HEAD_DIM
GKV_W = G_KV * HEAD_DIM
WQ_W = W_HEADS * HEAD_DIM
WKV_W = W_KV * HEAD_DIM
OFF_F = 0
OFF_GQ = OFF_F + FOURIER_W
OFF_GK = OFF_GQ + GQ_W
OFF_GV = OFF_GK + GKV_W
OFF_WQ = OFF_GV + GKV_W
OFF_WK = OFF_WQ + WQ_W
OFF_WV = OFF_WK + WKV_W
MIX_W = OFF_WV + WKV_W
GATE_W = N_BRANCH * D_MODEL
ROPE_THETA = 10000.0
N_EXPERTS = 16
CAPACITY_FACTOR = 2
EPS = 1e-6
NEG_INF = -1e30
LANES = 128
HEAD_PAD = LANES // HEAD_DIM
MOD_ROWS = 16
ROUTE_ROWS = 128
WINDOW_TILE = 4
VMEM_LIMIT = 56 * 1024 * 1024


def _cparams(*sem):
    return pltpu.CompilerParams(dimension_semantics=sem, vmem_limit_bytes=VMEM_LIMIT)


def _norm_mod(x, g, sh, sc):
    ms = jnp.mean(x * x, axis=-1, keepdims=True)
    return (x * lax.rsqrt(ms + EPS) * g) * (1.0 + sc) + sh


def _dot(a, b):
    return jnp.dot(a, b, preferred_element_type=F32)


def _dot_nt(a, b):
    return lax.dot_general(a, b, (((1,), (1,)), ((), ())), preferred_element_type=F32)


def _ada_kernel(c_ref, w_ref, b_ref, o_ref):
    c = c_ref[...]
    s = (c * jax.nn.sigmoid(c)).astype(BF16)
    o_ref[0] = _dot(s, w_ref[0].astype(BF16)) + b_ref[0]


def _ada(cvec, w_ada, b_ada):
    depth, d, n = w_ada.shape
    tn = 1536
    return pl.pallas_call(
        _ada_kernel,
        grid=(depth, n // tn),
        in_specs=[pl.BlockSpec((MOD_ROWS, d), lambda l, j: (0, 0)),
                  pl.BlockSpec((1, d, tn), lambda l, j: (l, 0, j)),
                  pl.BlockSpec((1, 1, tn), lambda l, j: (l, 0, j))],
        out_specs=pl.BlockSpec((1, MOD_ROWS, tn), lambda l, j: (l, 0, j)),
        out_shape=jax.ShapeDtypeStruct((depth, MOD_ROWS, n), F32),
        compiler_params=_cparams("arbitrary", "arbitrary"),
        name="ada",
    )(cvec, w_ada, b_ada.reshape(depth, 1, n))


def _rope(x, cos, sin_signed, even):
    outs = []
    for j in range(x.shape[1] // LANES):
        xb = x[:, j * LANES:(j + 1) * LANES]
        swap = jnp.where(even, pltpu.roll(xb, LANES - 1, 1), pltpu.roll(xb, 1, 1))
        outs.append(xb * cos + swap * sin_signed)
    return outs[0] if len(outs) == 1 else jnp.concatenate(outs, axis=1)


def _head_mean_square(z, bd):
    zz = z * z
    hi = zz.astype(BF16)
    lo = (zz - hi.astype(F32)).astype(BF16)
    outs = []
    for j in range(z.shape[1] // LANES):
        sl = slice(j * LANES, (j + 1) * LANES)
        outs.append(_dot(hi[:, sl], bd) + _dot(lo[:, sl], bd))
    return outs[0] if len(outs) == 1 else jnp.concatenate(outs, axis=1)


def _pad_heads_f32(x, fill):
    blk = jnp.full((x.shape[0], LANES - HEAD_DIM), fill, F32)
    parts = []
    for h in range(x.shape[1] // HEAD_DIM):
        parts += [x[:, h * HEAD_DIM:(h + 1) * HEAD_DIM], blk]
    return jnp.concatenate(parts, axis=1)


def _pad_heads(x, fill):
    return _pad_heads_f32(x, fill).astype(BF16)


def _inproj_kernel(x_ref, sh_ref, sc_ref, g_ref, w_ref, cos_ref, sin_ref, bd_ref, cdft_ref, qn_ref, kn_ref,
                   fcs_ref, qg_ref, kg_ref, vg_ref, qw_ref, kw_ref, vw_ref, *, rope):
    h = _norm_mod(x_ref[0], g_ref[...], sh_ref[0], sc_ref[0]).astype(BF16)
    z = _dot(h, w_ref[...])
    fcs_ref[0] = _dot(z[:, OFF_F:OFF_F + FOURIER_W].astype(BF16), cdft_ref[...].astype(BF16)).astype(BF16)
    bd = bd_ref[...]
    q = z[:, OFF_GQ:OFF_GQ + GQ_W]
    k = z[:, OFF_GK:OFF_GK + GKV_W]
    q = q * lax.rsqrt(_head_mean_square(q, bd) + EPS) * qn_ref[...]
    k = k * lax.rsqrt(_head_mean_square(k, bd) + EPS) * kn_ref[...]
    qw = z[:, OFF_WQ:OFF_WQ + WQ_W]
    kw = z[:, OFF_WK:OFF_WK + WKV_W]
    if rope:
        cos, sin = cos_ref[...], sin_ref[...]
        even = (lax.broadcasted_iota(jnp.int32, cos.shape, 1) % 2) == 0
        q, k = _rope(q, cos, sin, even), _rope(k, cos, sin, even)
        qw, kw = _rope(qw, cos, sin, even), _rope(kw, cos, sin, even)
    scale = HEAD_DIM ** -0.5
    qg_ref[0] = _pad_heads(q * scale, 0.0)
    kg_ref[0] = _pad_heads(k, 0.0)
    vg_ref[0] = _pad_heads(z[:, OFF_GV:OFF_GV + GKV_W], 1.0)
    qw_ref[0] = _pad_heads(qw * scale, 0.0)
    kw_ref[0] = _pad_heads(kw, 0.0)
    vw_ref[0] = _pad_heads(z[:, OFF_WV:OFF_WV + WKV_W], 1.0)


def _inproj(x, sh, sc, g, w_mix, cos_t, sin_t, bd, cdft, qn, kn, *, rope, tm):
    b, t, d = x.shape
    tok = lambda w: pl.BlockSpec((1, tm, w), lambda i, j: (i, j, 0))
    row = pl.BlockSpec((1, 1, d), lambda i, j: (i, 0, 0))
    const = lambda shape: pl.BlockSpec(shape, lambda i, j: (0,) * len(shape))
    widths = (2 * FOURIER_W,) + tuple(HEAD_PAD * w for w in (GQ_W, GKV_W, GKV_W, WQ_W, WKV_W, WKV_W))
    out_specs = [tok(w) for w in widths]
    out_shape = [jax.ShapeDtypeStruct((b, t, w), BF16) for w in widths]
    return pl.pallas_call(
        functools.partial(_inproj_kernel, rope=rope),
        grid=(b, t // tm),
        in_specs=[tok(d), row, row, const((1, d)), const((d, MIX_W)),
                  pl.BlockSpec((tm, LANES), lambda i, j: (j, 0)),
                  pl.BlockSpec((tm, LANES), lambda i, j: (j, 0)),
                  const((LANES, LANES)), const((FOURIER_W, 2 * FOURIER_W)),
                  const((1, GQ_W)), const((1, GKV_W))],
        out_specs=out_specs,
        out_shape=out_shape,
        compiler_params=_cparams("arbitrary", "arbitrary"),
        name="inproj_rope" if rope else "inproj_ctx",
    )(x, sh, sc, g, w_mix, cos_t, sin_t, bd, cdft, qn, kn)


def _fourier_kernel(cn_ref, sn_ref, fcs_ref, o_ref, cn_s, sn_s, *, scale):
    @pl.when(pl.program_id(1) == 0)
    def _():
        cn_s[...] = cn_ref[...].astype(BF16)
        sn_s[...] = sn_ref[...].astype(BF16)

    fcs = fcs_ref[0]
    re = _dot(cn_s[...], fcs[:, :FOURIER_W]) - _dot(sn_s[...], fcs[:, FOURIER_W:])
    o_ref[0] = (re * scale).astype(BF16)


def _fourier(fcs, cn, sn, *, tr):
    b, n, _ = fcs.shape
    scale = float((n * FOURIER_GROUP_CH) ** -0.5)
    return pl.pallas_call(
        functools.partial(_fourier_kernel, scale=scale),
        grid=(n // tr, b),
        in_specs=[pl.BlockSpec((tr, n), lambda r, i: (r, 0)),
                  pl.BlockSpec((tr, n), lambda r, i: (r, 0)),
                  pl.BlockSpec((1, n, 2 * FOURIER_W), lambda r, i: (i, 0, 0))],
        out_specs=pl.BlockSpec((1, tr, FOURIER_W), lambda r, i: (i, r, 0)),
        out_shape=jax.ShapeDtypeStruct((b, n, FOURIER_W), BF16),
        scratch_shapes=[pltpu.VMEM((tr, n), BF16), pltpu.VMEM((tr, n), BF16)],
        compiler_params=_cparams("arbitrary", "arbitrary"),
        name="fourier",
    )(cn, sn, fcs)


def _head(x, h):
    return x[:, h * LANES:(h + 1) * LANES]


def _stack_heads(q, first, count):
    return jnp.concatenate([_head(q, first + g) for g in range(count)], axis=0)


def _softmax_pv(parts, extra=None):
    m = functools.reduce(jnp.maximum, [jnp.max(s, axis=-1, keepdims=True) for s, _ in parts])
    if extra is not None:
        m = jnp.maximum(m, extra)
    acc = 0.0
    for s, v in parts:
        acc = acc + _dot(jnp.exp((s - m).astype(BF16)), v)
    den = acc[:, HEAD_DIM:HEAD_DIM + 1]
    if extra is not None:
        den = den + jnp.exp(extra - m)
    return acc[:, :HEAD_DIM] / den


def _unstack_heads(o_list, tq, count):
    cols = []
    for o in o_list:
        cols += [o[g * tq:(g + 1) * tq] for g in range(count)]
    return jnp.concatenate(cols, axis=1)


def _global_attn_kernel(q_ref, kl_ref, vl_ref, kc_ref, vc_ref, o_ref):
    q = q_ref[0]
    tq = q.shape[0]
    parts = []
    for kv in range(G_KV):
        qs = _stack_heads(q, kv * G_GROUP, G_GROUP)
        parts.append([(_dot_nt(qs, _head(kc_ref[0], kv)), _head(vc_ref[0], kv)),
                      (_dot_nt(qs, _head(kl_ref[0], kv)), _head(vl_ref[0], kv))])
    outs = [_softmax_pv(p) for p in parts]
    o_ref[0] = _unstack_heads(outs, tq, G_GROUP).astype(BF16)


def _global_attn(q, kl, vl, kc, vc, *, tq):
    b, t, _ = q.shape
    l = kc.shape[1]
    full = lambda n: pl.BlockSpec((1, n, HEAD_PAD * GKV_W), lambda i, j: (i, 0, 0))
    return pl.pallas_call(
        _global_attn_kernel,
        grid=(b, t // tq),
        in_specs=[pl.BlockSpec((1, tq, HEAD_PAD * GQ_W), lambda i, j: (i, j, 0)), full(t), full(t), full(l), full(l)],
        out_specs=pl.BlockSpec((1, tq, GQ_W), lambda i, j: (i, j, 0)),
        out_shape=jax.ShapeDtypeStruct((b, t, GQ_W), BF16),
        compiler_params=_cparams("arbitrary", "arbitrary"),
        name="global_attn",
    )(q, kl, vl, kc, vc)


def _sink_column(sink_ref, kv, rows_per_head):
    r = lax.broadcasted_iota(jnp.int32, (W_GROUP * rows_per_head, 1), 0)
    col = jnp.full(r.shape, sink_ref[kv * W_GROUP], F32)
    for g in range(1, W_GROUP):
        col = jnp.where(r >= g * rows_per_head, sink_ref[kv * W_GROUP + g], col)
    return col


def _window_attn_kernel(sink_ref, q_ref, kp_ref, k0_ref, kn_ref, vp_ref, v0_ref, vn_ref, kc_ref, vc_ref, o_ref,
                        *, n_tiles):
    j = pl.program_id(1)
    q = q_ref[0]
    kb = jnp.concatenate([kp_ref[0], k0_ref[0], kn_ref[0]], axis=0)
    vb = jnp.concatenate([vp_ref[0], v0_ref[0], vn_ref[0]], axis=0)
    rows = W_GROUP * Q_BLOCK
    qpos = lax.broadcasted_iota(jnp.int32, (rows, 3 * Q_BLOCK), 0) % Q_BLOCK
    kpos = lax.broadcasted_iota(jnp.int32, (rows, 3 * Q_BLOCK), 1)
    dist = jnp.abs(qpos - kpos + Q_BLOCK)
    in_window = dist <= WINDOW
    outside = jnp.full(dist.shape, WINDOW + 1, jnp.int32)
    first_ok = jnp.where((kpos < Q_BLOCK) & (j == 0), outside, dist) <= WINDOW
    last_ok = jnp.where((kpos >= 2 * Q_BLOCK) & (j == n_tiles - 1), outside, dist) <= WINDOW
    parts = []
    for i in range(WINDOW_TILE):
        valid = first_ok if i == 0 else (last_ok if i == WINDOW_TILE - 1 else in_window)
        q_i = q[i * Q_BLOCK:(i + 1) * Q_BLOCK]
        band = slice(i * Q_BLOCK, (i + 3) * Q_BLOCK)
        for kv in range(W_KV):
            qs = _stack_heads(q_i, kv * W_GROUP, W_GROUP)
            s_b = jnp.where(valid, _dot_nt(qs, _head(kb, kv)[band]), NEG_INF)
            parts.append([(_dot_nt(qs, _head(kc_ref[0], kv)), _head(vc_ref[0], kv)), (s_b, _head(vb, kv)[band])])
    sinks = [_sink_column(sink_ref, kv, Q_BLOCK) for kv in range(W_KV)]
    outs = [_softmax_pv(p, sinks[n % W_KV]) for n, p in enumerate(parts)]
    blocks = [_unstack_heads(outs[i * W_KV:(i + 1) * W_KV], Q_BLOCK, W_GROUP) for i in range(WINDOW_TILE)]
    o_ref[0] = jnp.concatenate(blocks, axis=0).astype(BF16)


def _window_attn(sink, q, k, v, kc, vc):
    b, t, _ = q.shape
    l = kc.shape[1]
    nb = t // Q_BLOCK
    tile = WINDOW_TILE * Q_BLOCK
    kvw = HEAD_PAD * WKV_W
    edge = lambda f: pl.BlockSpec((1, Q_BLOCK, kvw), lambda i, j: (i, f(j), 0))
    prev = lambda j: jnp.maximum(j * WINDOW_TILE - 1, 0)
    nxt = lambda j: jnp.minimum((j + 1) * WINDOW_TILE, nb - 1)
    mid = pl.BlockSpec((1, tile, kvw), lambda i, j: (i, j, 0))
    full = pl.BlockSpec((1, l, kvw), lambda i, j: (i, 0, 0))
    return pl.pallas_call(
        functools.partial(_window_attn_kernel, n_tiles=t // tile),
        grid=(b, t // tile),
        in_specs=[pl.BlockSpec(memory_space=pltpu.SMEM),
                  pl.BlockSpec((1, tile, HEAD_PAD * WQ_W), lambda i, j: (i, j, 0)),
                  edge(prev), mid, edge(nxt), edge(prev), mid, edge(nxt), full, full],
        out_specs=pl.BlockSpec((1, tile, WQ_W), lambda i, j: (i, j, 0)),
        out_shape=jax.ShapeDtypeStruct((b, t, WQ_W), BF16),
        compiler_params=_cparams("arbitrary", "arbitrary"),
        name="window_attn",
    )(sink, q, k, k, k, v, v, v, kc, vc)


def _ctx_attn_kernel(sink_ref, qg_ref, kg_ref, vg_ref, qw_ref, kw_ref, vw_ref, og_ref, ow_ref):
    l = qg_ref.shape[1]
    qg, qw = qg_ref[0], qw_ref[0]
    g_parts = [[(_dot_nt(_stack_heads(qg, kv * G_GROUP, G_GROUP), _head(kg_ref[0], kv)), _head(vg_ref[0], kv))]
               for kv in range(G_KV)]
    w_parts = [[(_dot_nt(_stack_heads(qw, kv * W_GROUP, W_GROUP), _head(kw_ref[0], kv)), _head(vw_ref[0], kv))]
               for kv in range(W_KV)]
    og_ref[0] = _unstack_heads([_softmax_pv(p) for p in g_parts], l, G_GROUP).astype(BF16)
    outs = [_softmax_pv(p, _sink_column(sink_ref, kv, l)) for kv, p in enumerate(w_parts)]
    ow_ref[0] = _unstack_heads(outs, l, W_GROUP).astype(BF16)


def _ctx_attn(sink, qg, kg, vg, qw, kw, vw):
    b, l, _ = qg.shape
    spec = lambda w: pl.BlockSpec((1, l, w), lambda i: (i, 0, 0))
    padded = lambda w: spec(HEAD_PAD * w)
    return pl.pallas_call(
        _ctx_attn_kernel,
        grid=(b,),
        in_specs=[pl.BlockSpec(memory_space=pltpu.SMEM),
                  padded(GQ_W), padded(GKV_W), padded(GKV_W), padded(WQ_W), padded(WKV_W), padded(WKV_W)],
        out_specs=[spec(GQ_W), spec(WQ_W)],
        out_shape=[jax.ShapeDtypeStruct((b, l, GQ_W), BF16), jax.ShapeDtypeStruct((b, l, WQ_W), BF16)],
        compiler_params=_cparams("arbitrary"),
        name="ctx_attn",
    )(sink, qg, kg, vg, qw, kw, vw)


def _merge_kernel(x_ref, f_ref, og_ref, ow_ref, sh1_ref, sc1_ref, g1_ref, sh2_ref, sc2_ref,
                  n1_ref, n2_ref, wgate_ref, wbf_ref, wbg_ref, wbw_ref, wout_ref, wr_ref,
                  x1_ref, h2_ref, aff_ref):
    x = x_ref[0]
    h = _norm_mod(x, n1_ref[...], sh1_ref[0], sc1_ref[0]).astype(BF16)
    gate = jax.nn.sigmoid(_dot(h, wgate_ref[...]))
    d = D_MODEL
    m = (gate[:, 0:d] * _dot(f_ref[0], wbf_ref[...])
         + gate[:, d:2 * d] * _dot(og_ref[0], wbg_ref[...])
         + gate[:, 2 * d:3 * d] * _dot(ow_ref[0], wbw_ref[...]))
    x1 = x + g1_ref[0] * _dot(m.astype(BF16), wout_ref[...])
    x1_ref[0] = x1
    h2 = _norm_mod(x1, n2_ref[...], sh2_ref[0], sc2_ref[0]).astype(BF16)
    h2_ref[0] = h2
    logits = _dot(h2, wr_ref[...]).T[:N_EXPERTS]
    e = jnp.exp(logits - jnp.max(logits, axis=0, keepdims=True))
    aff_ref[0] = e / jnp.sum(e, axis=0, keepdims=True)


def _merge(x, f, og, ow, sh1, sc1, g1, sh2, sc2, n1, n2, wgate, wbf, wbg, wbw, wout, wr, *, tm):
    b, t, d = x.shape
    tok = lambda w: pl.BlockSpec((1, tm, w), lambda i, j: (i, j, 0))
    row = pl.BlockSpec((1, 1, d), lambda i, j: (i, 0, 0))
    const = lambda shape: pl.BlockSpec(shape, lambda i, j: (0,) * len(shape))
    return pl.pallas_call(
        _merge_kernel,
        grid=(b, t // tm),
        in_specs=[tok(d), tok(FOURIER_W), tok(GQ_W), tok(WQ_W), row, row, row, row, row,
                  const((1, d)), const((1, d)), const((d, GATE_W)), const((FOURIER_W, d)),
                  const((GQ_W, d)), const((WQ_W, d)), const((d, d)), const((d, LANES))],
        out_specs=[tok(d), tok(d), pl.BlockSpec((1, N_EXPERTS, tm), lambda i, j: (i, 0, j))],
        out_shape=[jax.ShapeDtypeStruct((b, t, d), F32), jax.ShapeDtypeStruct((b, t, d), BF16),
                   jax.ShapeDtypeStruct((b, N_EXPERTS, t), F32)],
        compiler_params=_cparams("arbitrary", "arbitrary"),
        name="merge",
    )(x, f, og, ow, sh1, sc1, g1, sh2, sc2, n1, n2, wgate, wbf, wbg, wbw, wout, wr)


def _cumsum_lanes(m, tri):
    e, n = m.shape
    nch = n // LANES
    stacked = jnp.concatenate([m[:, j * LANES:(j + 1) * LANES] for j in range(nch)], axis=0).astype(BF16)
    w = _dot(stacked, tri)
    outs, off = [], jnp.zeros((e, 1), F32)
    for j in range(nch):
        wj = w[j * e:(j + 1) * e]
        outs.append(wj + off)
        off = off + wj[:, LANES - 1:LANES]
    return jnp.concatenate(outs, axis=1)


def _route_kernel(aff_ref, tri_ref, pos_ref, rt_ref, *, cap, n_exp):
    aff = aff_ref[...]
    e, n = aff.shape
    thr_bits = jnp.zeros((e, 1), jnp.int32)
    for bit in range(30, -1, -1):
        cand = thr_bits | (1 << bit)
        cnt = jnp.sum(jnp.where(aff >= pltpu.bitcast(cand, F32), 1.0, 0.0), axis=1, keepdims=True)
        thr_bits = jnp.where(cnt >= cap, cand, thr_bits)
    ge = jnp.where(aff >= pltpu.bitcast(thr_bits, F32), 1.0, 0.0)
    gt = jnp.where(aff >= pltpu.bitcast(thr_bits + 1, F32), 1.0, 0.0)
    eq = ge - gt
    room = cap - jnp.sum(gt, axis=1, keepdims=True)
    tri = tri_ref[...]
    sel = gt + jnp.where(_cumsum_lanes(eq, tri) <= room, eq, 0.0)
    pos = jnp.where(sel > 0.0, _cumsum_lanes(sel, tri) - 1.0, -1.0)
    pos_ref[...] = pos.astype(jnp.int32)
    weight = sel * aff
    pad = jnp.zeros((ROUTE_ROWS - 2 * n_exp, n), F32)
    for i in range(e // n_exp):
        rows = slice(i * n_exp, (i + 1) * n_exp)
        rt_ref[i] = jnp.concatenate([pos[rows], weight[rows], pad], axis=0).T


def _route(aff, tri, *, cap):
    b, e, n = aff.shape
    pos, rt = pl.pallas_call(
        functools.partial(_route_kernel, cap=cap, n_exp=e),
        grid=(1,),
        in_specs=[pl.BlockSpec((b * e, n), lambda i: (0, 0)), pl.BlockSpec((LANES, LANES), lambda i: (0, 0))],
        out_specs=[pl.BlockSpec((b * e, n), lambda i: (0, 0)), pl.BlockSpec((b, n, ROUTE_ROWS), lambda i: (0, 0, 0))],
        out_shape=[jax.ShapeDtypeStruct((b * e, n), jnp.int32), jax.ShapeDtypeStruct((b, n, ROUTE_ROWS), F32)],
        compiler_params=_cparams("arbitrary"),
        name="route",
    )(aff.reshape(b * e, n), tri)
    return pos.reshape(b, e, n), rt


def _gather_kernel(pos_ref, h_ref, o_ref, *, cap, group):
    j = pl.program_id(1)
    n = h_ref.shape[1]
    slot = lax.broadcasted_iota(jnp.int32, (cap, n), 0)
    sel = [jnp.where(pos_ref[0, pl.ds(j * group + g, 1), :] == slot, 1.0, 0.0).astype(BF16) for g in range(group)]
    o_ref[0] = _dot(jnp.concatenate(sel, axis=0), h_ref[0]).astype(BF16)


def _gather(pos, h, *, cap, group):
    b, e, n = pos.shape
    d = h.shape[2]
    return pl.pallas_call(
        functools.partial(_gather_kernel, cap=cap, group=group),
        grid=(b, e // group),
        in_specs=[pl.BlockSpec((1, e, n), lambda i, j: (i, 0, 0)), pl.BlockSpec((1, n, d), lambda i, j: (i, 0, 0))],
        out_specs=pl.BlockSpec((1, group * cap, d), lambda i, j: (i, j, 0)),
        out_shape=jax.ShapeDtypeStruct((b, e * cap, d), BF16),
        compiler_params=_cparams("arbitrary", "arbitrary"),
        name="moe_gather",
    )(pos, h)


def _expert_kernel(*refs, n_sets):
    x_refs, (wg_ref, wu_ref, wd_ref) = refs[:n_sets], refs[n_sets:n_sets + 3]
    o_refs = refs[n_sets + 3:2 * n_sets + 3]
    wg_s, wu_s, wd_s = refs[2 * n_sets + 3:]

    @pl.when(pl.program_id(1) == 0)
    def _():
        wg_s[...] = wg_ref[0, 0].astype(BF16)
        wu_s[...] = wu_ref[0, 0].astype(BF16)
        wd_s[...] = wd_ref[0, 0].astype(BF16)

    d = x_refs[0].shape[2]
    rows = [r.shape[0] * r.shape[1] for r in x_refs]
    xs = [r[...].reshape(n, d) for r, n in zip(x_refs, rows)]
    x = xs[0] if n_sets == 1 else jnp.concatenate(xs, axis=0)
    a = _dot(x, wg_s[...])
    u = _dot(x, wu_s[...])
    y = _dot((a * jax.nn.sigmoid(a) * u).astype(BF16), wd_s[...]).astype(BF16)
    start = 0
    for o_ref, n in zip(o_refs, rows):
        o_ref[...] = y[start:start + n].reshape(o_ref.shape)
        start += n


def _experts(xgs, caps, layer, wg, wu, wd, *, nb):
    b, _, d = xgs[0].shape
    _, e, _, f = wg.shape
    wspec = lambda r, c: pl.BlockSpec((1, 1, r, c), lambda i, j: (layer, i, 0, 0))
    xspecs = [pl.BlockSpec((nb, cap, d), lambda i, j: (j, i, 0)) for cap in caps]
    return pl.pallas_call(
        functools.partial(_expert_kernel, n_sets=len(xgs)),
        grid=(e, b // nb),
        in_specs=xspecs + [wspec(d, f), wspec(d, f), wspec(f, d)],
        out_specs=xspecs,
        out_shape=[jax.ShapeDtypeStruct(xg.shape, BF16) for xg in xgs],
        scratch_shapes=[pltpu.VMEM((d, f), BF16), pltpu.VMEM((d, f), BF16), pltpu.VMEM((f, d), BF16)],
        compiler_params=_cparams("arbitrary", "arbitrary"),
        name="moe_experts",
    )(*xgs, wg, wu, wd)


def _scatter_kernel(x_ref, rt_ref, y_ref, g2_ref, fg_ref, o_ref, *, cap, final):
    rt = rt_ref[0]
    tn = rt.shape[0]
    slot = lax.broadcasted_iota(jnp.int32, (tn, cap), 1).astype(F32)
    cols = [jnp.where(rt[:, e:e + 1] == slot, rt[:, N_EXPERTS + e:N_EXPERTS + e + 1], 0.0).astype(BF16)
            for e in range(N_EXPERTS)]
    y = _dot(jnp.concatenate(cols, axis=1), y_ref[0])
    x = x_ref[0] + g2_ref[0] * y
    if final:
        ms = jnp.mean(x * x, axis=-1, keepdims=True)
        x = x * lax.rsqrt(ms + EPS) * fg_ref[...]
    o_ref[0] = x


def _scatter(x, rt, y, g2, fg, *, cap, final, tn):
    b, t, d = x.shape
    return pl.pallas_call(
        functools.partial(_scatter_kernel, cap=cap, final=final),
        grid=(b, t // tn),
        in_specs=[pl.BlockSpec((1, tn, d), lambda i, j: (i, j, 0)),
                  pl.BlockSpec((1, tn, ROUTE_ROWS), lambda i, j: (i, j, 0)),
                  pl.BlockSpec((1, N_EXPERTS * cap, d), lambda i, j: (i, 0, 0)),
                  pl.BlockSpec((1, 1, d), lambda i, j: (i, 0, 0)),
                  pl.BlockSpec((1, d), lambda i, j: (0, 0))],
        out_specs=pl.BlockSpec((1, tn, d), lambda i, j: (i, j, 0)),
        out_shape=jax.ShapeDtypeStruct((b, t, d), F32),
        compiler_params=_cparams("arbitrary", "arbitrary"),
        name="moe_scatter",
    )(x, rt, y, g2, fg)


def _capacity(t):
    return CAPACITY_FACTOR * t // N_EXPERTS


def _dft_tables(n):
    k = np.arange(n, dtype=np.int64)
    ang = 2.0 * np.pi * ((k[:, None] * k[None, :]) % n).astype(np.float64) / n
    return np.cos(ang), np.sin(ang)


def _channel_dft():
    c, s = _dft_tables(FOURIER_GROUP_CH)
    eye = np.eye(FOURIER_GROUPS)
    return np.concatenate([np.kron(eye, c), np.kron(eye, s)], axis=1)


def _rope_tables(t):
    rows = t // GRID_W
    r, col = jnp.meshgrid(jnp.arange(rows), jnp.arange(GRID_W), indexing="ij")
    half = HEAD_DIM // 2
    inv = ROPE_THETA ** (-jnp.arange(0, half, 2, dtype=F32) / half)
    ang = jnp.concatenate([r.reshape(-1, 1).astype(F32) * inv, col.reshape(-1, 1).astype(F32) * inv], axis=-1)
    cos = jnp.repeat(jnp.cos(ang), 2, axis=1)
    sin = jnp.repeat(jnp.sin(ang), 2, axis=1) * jnp.tile(jnp.asarray([-1.0, 1.0], F32), half)
    return jnp.tile(cos, (1, HEAD_PAD)), jnp.tile(sin, (1, HEAD_PAD))


def kernel(x, c, ctx, c_ctx, w_ada, b_ada, norm1_g, w_in, q_norm_g, k_norm_g, sink, w_br_fourier, w_br_global,
           w_br_window, w_out, norm2_g, w_router, w_gate_e, w_up_e, w_down_e, final_g):
    b, t, d = x.shape
    l_ctx = ctx.shape[1]
    depth = w_ada.shape[0]
    cap_t, cap_c = _capacity(t), _capacity(l_ctx)

    cos_t, sin_t = _rope_tables(t)
    cos_c, sin_c = jnp.ones((l_ctx, LANES), F32), jnp.zeros((l_ctx, LANES), F32)
    head_avg = jnp.asarray(np.kron(np.eye(HEAD_PAD), np.full((HEAD_DIM, HEAD_DIM), 1.0 / HEAD_DIM)), BF16)
    cdft = jnp.asarray(_channel_dft(), F32)
    cn_t, sn_t = (jnp.asarray(a, F32) for a in _dft_tables(t))
    cn_c, sn_c = (jnp.asarray(a, F32) for a in _dft_tables(l_ctx))
    tri = jnp.asarray(np.triu(np.ones((LANES, LANES))), BF16)

    cvec = jnp.concatenate([c, c_ctx[None], jnp.zeros((MOD_ROWS - b - 1, d), F32)], axis=0)
    mods = _ada(cvec, w_ada, b_ada)

    xc = ctx
    for l in range(depth):
        need_ctx = l < depth - 1
        final = l == depth - 1
        lat = [mods[l, :b, i * d:(i + 1) * d].reshape(b, 1, d) for i in range(6)]
        cmod = [jnp.broadcast_to(mods[l, b, i * d:(i + 1) * d].reshape(1, 1, d), (b, 1, d)) for i in range(6)]
        w_mix = w_in[l, :, :MIX_W].astype(BF16)
        w_gate = w_in[l, :, MIX_W:].astype(BF16)
        n1, n2 = norm1_g[l].reshape(1, d), norm2_g[l].reshape(1, d)
        qn = jnp.tile(q_norm_g[l], G_HEADS).reshape(1, GQ_W)
        kn = jnp.tile(k_norm_g[l], G_KV).reshape(1, GKV_W)
        wbf, wbg, wbw = (w[l].astype(BF16) for w in (w_br_fourier, w_br_global, w_br_window))
        wout = w_out[l].astype(BF16)
        wr = jnp.pad(w_router[l], ((0, 0), (0, LANES - N_EXPERTS))).astype(BF16)
        fg = final_g.reshape(1, d)
        merge_w = (n1, n2, w_gate, wbf, wbg, wbw, wout, wr)

        cfcs, cqg, ckg, cvg, cqw, ckw, cvw = _inproj(xc, cmod[0], cmod[1], n1, w_mix, cos_c, sin_c, head_avg, cdft,
                                                     qn, kn, rope=False, tm=l_ctx)
        xgs, caps = [], []
        if need_ctx:
            cf_mix = _fourier(cfcs, cn_c, sn_c, tr=l_ctx)
            oc_g, oc_w = _ctx_attn(sink[l], cqg, ckg, cvg, cqw, ckw, cvw)
            xc1, hc2, caff = _merge(xc, cf_mix, oc_g, oc_w, *cmod[:5], *merge_w, tm=l_ctx)
            cpos, crt = _route(caff, tri, cap=cap_c)
            xgs.append(_gather(cpos, hc2, cap=cap_c, group=4))
            caps.append(cap_c)

        fcs, qg, kg, vg, qw, kw, vw = _inproj(x, lat[0], lat[1], n1, w_mix, cos_t, sin_t, head_avg, cdft, qn, kn,
                                              rope=True, tm=512)
        f_mix = _fourier(fcs, cn_t, sn_t, tr=512)
        o_g = _global_attn(qg, kg, vg, ckg, cvg, tq=256)
        o_w = _window_attn(sink[l], qw, kw, vw, ckw, cvw)
        x1, h2, aff = _merge(x, f_mix, o_g, o_w, *lat[:5], *merge_w, tm=512)
        pos, rt = _route(aff, tri, cap=cap_t)
        xgs.insert(0, _gather(pos, h2, cap=cap_t, group=4))
        caps.insert(0, cap_t)

        ys = _experts(xgs, caps, l, w_gate_e, w_up_e, w_down_e, nb=4)
        x = _scatter(x1, rt, ys[0], lat[5], fg, cap=cap_t, final=final, tn=512)
        if need_ctx:
            xc = _scatter(xc1, crt, ys[1], cmod[5], fg, cap=cap_c, final=False, tn=l_ctx)
    return x
```

```python
import functools

import numpy as np
import jax
import jax.numpy as jnp
from jax import lax
from jax.experimental import pallas as pl
from jax.experimental.pallas import tpu as pltpu

F32 = jnp.float32
BF16 = jnp.bfloat16

D_MODEL = 1024
HEAD_DIM = 64
GRID_W = 64
FOURIER_GROUPS = 4
FOURIER_GROUP_CH = 64
FOURIER_W = FOURIER_GROUPS * FOURIER_GROUP_CH
G_HEADS, G_KV = 8, 2
G_GROUP = G_HEADS // G_KV
W_HEADS, W_KV = 4, 2
W_GROUP = W_HEADS // W_KV
WINDOW = 128
Q_BLOCK = 128
N_BRANCH = 3
GQ_W = G_HEADS * HEAD_DIM
GKV_W = G_KV * HEAD_DIM
WQ_W = W_HEADS * HEAD_DIM
WKV_W = W_KV * HEAD_DIM
OFF_F = 0
OFF_GQ = OFF_F + FOURIER_W
OFF_GK = OFF_GQ + GQ_W
OFF_GV = OFF_GK + GKV_W
OFF_WQ = OFF_GV + GKV_W
OFF_WK = OFF_WQ + WQ_W
OFF_WV = OFF_WK + WKV_W
MIX_W = OFF_WV + WKV_W
GATE_W = N_BRANCH * D_MODEL
ROPE_THETA = 10000.0
N_EXPERTS = 16
CAPACITY_FACTOR = 2
EPS = 1e-6
NEG_INF = -1e30
LANES = 128
HEAD_PAD = LANES // HEAD_DIM
MXU_DIM = 256
MOD_ROWS = 16
SCATTER_TILE = 256
SCATTER_WINDOW = 64
ROUTE_ROWS = 128
WINDOW_TILE = 4
VMEM_LIMIT = 56 * 1024 * 1024


def _cparams(*sem):
    return pltpu.CompilerParams(dimension_semantics=sem, vmem_limit_bytes=VMEM_LIMIT)


def _norm_mod(x, g, sh, sc):
    ms = jnp.mean(x * x, axis=-1, keepdims=True)
    return (x * lax.rsqrt(ms + EPS) * g) * (1.0 + sc) + sh


def _dot(a, b):
    return jnp.dot(a, b, preferred_element_type=F32)


def _dot_nt(a, b):
    return lax.dot_general(a, b, (((1,), (1,)), ((), ())), preferred_element_type=F32)


def _ada_kernel(c_ref, w_ref, b_ref, o_ref):
    c = c_ref[...]
    s = (c * jax.nn.sigmoid(c)).astype(BF16)
    o_ref[0] = _dot(s, w_ref[0].astype(BF16)) + b_ref[0]


def _ada(cvec, w_ada, b_ada):
    depth, d, n = w_ada.shape
    tn = 1536
    return pl.pallas_call(
        _ada_kernel,
        grid=(depth, n // tn),
        in_specs=[pl.BlockSpec((MOD_ROWS, d), lambda l, j: (0, 0)),
                  pl.BlockSpec((1, d, tn), lambda l, j: (l, 0, j)),
                  pl.BlockSpec((1, 1, tn), lambda l, j: (l, 0, j))],
        out_specs=pl.BlockSpec((1, MOD_ROWS, tn), lambda l, j: (l, 0, j)),
        out_shape=jax.ShapeDtypeStruct((depth, MOD_ROWS, n), F32),
        compiler_params=_cparams("arbitrary", "arbitrary"),
        name="ada",
    )(cvec, w_ada, b_ada.reshape(depth, 1, n))


def _rope(x, cos, sin_signed, even):
    outs = []
    for j in range(x.shape[1] // LANES):
        xb = x[:, j * LANES:(j + 1) * LANES]
        swap = jnp.where(even, pltpu.roll(xb, LANES - 1, 1), pltpu.roll(xb, 1, 1))
        outs.append(xb * cos + swap * sin_signed)
    return outs[0] if len(outs) == 1 else jnp.concatenate(outs, axis=1)


def _head_mean_square(z, bd):
    zz = (z * z).astype(BF16)
    blk = bd.shape[0]
    if z.shape[1] < blk:
        return _dot(zz, bd[:z.shape[1], :z.shape[1]])
    outs = [_dot(zz[:, j * blk:(j + 1) * blk], bd) for j in range(z.shape[1] // blk)]
    return outs[0] if len(outs) == 1 else jnp.concatenate(outs, axis=1)


def _pad_heads_f32(x, fill):
    blk = jnp.full((x.shape[0], LANES - HEAD_DIM), fill, F32)
    parts = []
    for h in range(x.shape[1] // HEAD_DIM):
        parts += [x[:, h * HEAD_DIM:(h + 1) * HEAD_DIM], blk]
    return jnp.concatenate(parts, axis=1)


def _pad_heads(x, fill):
    return _pad_heads_f32(x, fill).astype(BF16)


def _inproj_kernel(x_ref, sh_ref, sc_ref, g_ref, w_ref, cos_ref, sin_ref, bd_ref, cdft_ref, qn_ref, kn_ref,
                   fcs_ref, qg_ref, kg_ref, vg_ref, qw_ref, kw_ref, vw_ref, *, rope, chunks):
    tm = x_ref.shape[1]
    rows = [slice(c * (tm // chunks), (c + 1) * (tm // chunks)) for c in range(chunks)]
    zs = [_dot(_norm_mod(x_ref[0, r], g_ref[...], sh_ref[0], sc_ref[0]).astype(BF16), w_ref[...]) for r in rows]
    bd = bd_ref[...]
    scale = HEAD_DIM ** -0.5
    for r, z in zip(rows, zs):
        fcs_ref[0, r] = _dot(z[:, OFF_F:OFF_F + FOURIER_W].astype(BF16), cdft_ref[...].astype(BF16)).astype(BF16)
        q = z[:, OFF_GQ:OFF_GQ + GQ_W]
        k = z[:, OFF_GK:OFF_GK + GKV_W]
        q = q * lax.rsqrt(_head_mean_square(q, bd) + EPS) * qn_ref[...]
        k = k * lax.rsqrt(_head_mean_square(k, bd) + EPS) * kn_ref[...]
        qw = z[:, OFF_WQ:OFF_WQ + WQ_W]
        kw = z[:, OFF_WK:OFF_WK + WKV_W]
        if rope:
            cos, sin = cos_ref[r], sin_ref[r]
            even = (lax.broadcasted_iota(jnp.int32, cos.shape, 1) % 2) == 0
            q, k = _rope(q, cos, sin, even), _rope(k, cos, sin, even)
            qw, kw = _rope(qw, cos, sin, even), _rope(kw, cos, sin, even)
        qg_ref[0, r] = _pad_heads(q * scale, 0.0)
        kg_ref[0, r] = _pad_heads(k, 0.0)
        vg_ref[0, r] = _pad_heads(z[:, OFF_GV:OFF_GV + GKV_W], 1.0)
        qw_ref[0, r] = _pad_heads(qw * scale, 0.0)
        kw_ref[0, r] = _pad_heads(kw, 0.0)
        vw_ref[0, r] = _pad_heads(z[:, OFF_WV:OFF_WV + WKV_W], 1.0)


def _inproj(x, sh, sc, g, w_mix, cos_t, sin_t, bd, cdft, qn, kn, *, rope, tm, chunks):
    b, t, d = x.shape
    tok = lambda w: pl.BlockSpec((1, tm, w), lambda i, j: (i, j, 0))
    row = pl.BlockSpec((1, 1, d), lambda i, j: (i, 0, 0))
    const = lambda shape: pl.BlockSpec(shape, lambda i, j: (0,) * len(shape))
    widths = (2 * FOURIER_W,) + tuple(HEAD_PAD * w for w in (GQ_W, GKV_W, GKV_W, WQ_W, WKV_W, WKV_W))
    out_specs = [tok(w) for w in widths]
    out_shape = [jax.ShapeDtypeStruct((b, t, w), BF16) for w in widths]
    return pl.pallas_call(
        functools.partial(_inproj_kernel, rope=rope, chunks=chunks),
        grid=(b, t // tm),
        in_specs=[tok(d), row, row, const((1, d)), const((d, MIX_W)),
                  pl.BlockSpec((tm, LANES), lambda i, j: (j, 0)),
                  pl.BlockSpec((tm, LANES), lambda i, j: (j, 0)),
                  const((MXU_DIM, MXU_DIM)), const((FOURIER_W, 2 * FOURIER_W)),
                  const((1, GQ_W)), const((1, GKV_W))],
        out_specs=out_specs,
        out_shape=out_shape,
        compiler_params=_cparams("arbitrary", "arbitrary"),
        name="inproj_rope" if rope else "inproj_ctx",
    )(x, sh, sc, g, w_mix, cos_t, sin_t, bd, cdft, qn, kn)


def _fourier_kernel(cn_ref, sn_ref, fcs_ref, o_ref, cn_s, sn_s, *, scale):
    @pl.when(pl.program_id(1) == 0)
    def _():
        cn_s[...] = cn_ref[...].astype(BF16)
        sn_s[...] = sn_ref[...].astype(BF16)

    fcs = fcs_ref[0]
    re = _dot(cn_s[...], fcs[:, :FOURIER_W]) - _dot(sn_s[...], fcs[:, FOURIER_W:])
    o_ref[0] = (re * scale).astype(BF16)


def _fourier(fcs, cn, sn, *, tr):
    b, n, _ = fcs.shape
    scale = float((n * FOURIER_GROUP_CH) ** -0.5)
    return pl.pallas_call(
        functools.partial(_fourier_kernel, scale=scale),
        grid=(n // tr, b),
        in_specs=[pl.BlockSpec((tr, n), lambda r, i: (r, 0)),
                  pl.BlockSpec((tr, n), lambda r, i: (r, 0)),
                  pl.BlockSpec((1, n, 2 * FOURIER_W), lambda r, i: (i, 0, 0))],
        out_specs=pl.BlockSpec((1, tr, FOURIER_W), lambda r, i: (i, r, 0)),
        out_shape=jax.ShapeDtypeStruct((b, n, FOURIER_W), BF16),
        scratch_shapes=[pltpu.VMEM((tr, n), BF16), pltpu.VMEM((tr, n), BF16)],
        compiler_params=_cparams("arbitrary", "arbitrary"),
        name="fourier",
    )(cn, sn, fcs)


def _head(x, h):
    return x[:, h * LANES:(h + 1) * LANES]


def _stack_heads(q, first, count):
    return jnp.concatenate([_head(q, first + g) for g in range(count)], axis=0)


def _softmax_pv(parts, extra=None):
    m = functools.reduce(jnp.maximum, [jnp.max(s, axis=-1, keepdims=True) for s, _ in parts])
    if extra is not None:
        m = jnp.maximum(m, extra)
    acc = 0.0
    for s, v in parts:
        acc = acc + _dot(jnp.exp((s - m).astype(BF16)), v)
    den = acc[:, HEAD_DIM:HEAD_DIM + 1]
    if extra is not None:
        den = den + jnp.exp(extra - m)
    return acc[:, :HEAD_DIM] / den


def _unstack_heads(o_list, tq, count):
    cols = []
    for o in o_list:
        cols += [o[g * tq:(g + 1) * tq] for g in range(count)]
    return jnp.concatenate(cols, axis=1)


def _global_attn_kernel(q_ref, kl_ref, vl_ref, kc_ref, vc_ref, o_ref):
    q = q_ref[0]
    tq = q.shape[0]
    parts = []
    for kv in range(G_KV):
        qs = _stack_heads(q, kv * G_GROUP, G_GROUP)
        parts.append([(_dot_nt(qs, _head(kc_ref[0], kv)), _head(vc_ref[0], kv)),
                      (_dot_nt(qs, _head(kl_ref[0], kv)), _head(vl_ref[0], kv))])
    outs = [_softmax_pv(p) for p in parts]
    o_ref[0] = _unstack_heads(outs, tq, G_GROUP).astype(BF16)


def _global_attn(q, kl, vl, kc, vc, *, tq):
    b, t, _ = q.shape
    l = kc.shape[1]
    full = lambda n: pl.BlockSpec((1, n, HEAD_PAD * GKV_W), lambda i, j: (i, 0, 0))
    return pl.pallas_call(
        _global_attn_kernel,
        grid=(b, t // tq),
        in_specs=[pl.BlockSpec((1, tq, HEAD_PAD * GQ_W), lambda i, j: (i, j, 0)), full(t), full(t), full(l), full(l)],
        out_specs=pl.BlockSpec((1, tq, GQ_W), lambda i, j: (i, j, 0)),
        out_shape=jax.ShapeDtypeStruct((b, t, GQ_W), BF16),
        compiler_params=_cparams("arbitrary", "arbitrary"),
        name="global_attn",
    )(q, kl, vl, kc, vc)


def _sink_column(sink_ref, kv, rows_per_head):
    r = lax.broadcasted_iota(jnp.int32, (W_GROUP * rows_per_head, 1), 0)
    col = jnp.full(r.shape, sink_ref[kv * W_GROUP], F32)
    for g in range(1, W_GROUP):
        col = jnp.where(r >= g * rows_per_head, sink_ref[kv * W_GROUP + g], col)
    return col


def _window_attn_kernel(sink_ref, q_ref, kp_ref, k0_ref, kn_ref, vp_ref, v0_ref, vn_ref, kc_ref, vc_ref, o_ref,
                        *, n_tiles):
    j = pl.program_id(1)
    q = q_ref[0]
    kb = jnp.concatenate([kp_ref[0], k0_ref[0], kn_ref[0]], axis=0)
    vb = jnp.concatenate([vp_ref[0], v0_ref[0], vn_ref[0]], axis=0)
    rows = W_GROUP * Q_BLOCK
    qpos = lax.broadcasted_iota(jnp.int32, (rows, 3 * Q_BLOCK), 0) % Q_BLOCK
    kpos = lax.broadcasted_iota(jnp.int32, (rows, 3 * Q_BLOCK), 1)
    dist = jnp.abs(qpos - kpos + Q_BLOCK)
    in_window = dist <= WINDOW
    outside = jnp.full(dist.shape, WINDOW + 1, jnp.int32)
    first_ok = jnp.where((kpos < Q_BLOCK) & (j == 0), outside, dist) <= WINDOW
    last_ok = jnp.where((kpos >= 2 * Q_BLOCK) & (j == n_tiles - 1), outside, dist) <= WINDOW
    parts = []
    for i in range(WINDOW_TILE):
        valid = first_ok if i == 0 else (last_ok if i == WINDOW_TILE - 1 else in_window)
        q_i = q[i * Q_BLOCK:(i + 1) * Q_BLOCK]
        band = slice(i * Q_BLOCK, (i + 3) * Q_BLOCK)
        for kv in range(W_KV):
            qs = _stack_heads(q_i, kv * W_GROUP, W_GROUP)
            s_b = jnp.where(valid, _dot_nt(qs, _head(kb, kv)[band]), NEG_INF)
            parts.append([(_dot_nt(qs, _head(kc_ref[0], kv)), _head(vc_ref[0], kv)), (s_b, _head(vb, kv)[band])])
    sinks = [_sink_column(sink_ref, kv, Q_BLOCK) for kv in range(W_KV)]
    outs = [_softmax_pv(p, sinks[n % W_KV]) for n, p in enumerate(parts)]
    blocks = [_unstack_heads(outs[i * W_KV:(i + 1) * W_KV], Q_BLOCK, W_GROUP) for i in range(WINDOW_TILE)]
    o_ref[0] = jnp.concatenate(blocks, axis=0).astype(BF16)


def _window_attn(sink, q, k, v, kc, vc):
    b, t, _ = q.shape
    l = kc.shape[1]
    nb = t // Q_BLOCK
    tile = WINDOW_TILE * Q_BLOCK
    kvw = HEAD_PAD * WKV_W
    edge = lambda f: pl.BlockSpec((1, Q_BLOCK, kvw), lambda i, j: (i, f(j), 0))
    prev = lambda j: jnp.maximum(j * WINDOW_TILE - 1, 0)
    nxt = lambda j: jnp.minimum((j + 1) * WINDOW_TILE, nb - 1)
    mid = pl.BlockSpec((1, tile, kvw), lambda i, j: (i, j, 0))
    full = pl.BlockSpec((1, l, kvw), lambda i, j: (i, 0, 0))
    return pl.pallas_call(
        functools.partial(_window_attn_kernel, n_tiles=t // tile),
        grid=(b, t // tile),
        in_specs=[pl.BlockSpec(memory_space=pltpu.SMEM),
                  pl.BlockSpec((1, tile, HEAD_PAD * WQ_W), lambda i, j: (i, j, 0)),
                  edge(prev), mid, edge(nxt), edge(prev), mid, edge(nxt), full, full],
        out_specs=pl.BlockSpec((1, tile, WQ_W), lambda i, j: (i, j, 0)),
        out_shape=jax.ShapeDtypeStruct((b, t, WQ_W), BF16),
        compiler_params=_cparams("arbitrary", "arbitrary"),
        name="window_attn",
    )(sink, q, k, k, k, v, v, v, kc, vc)


def _ctx_attn_kernel(sink_ref, qg_ref, kg_ref, vg_ref, qw_ref, kw_ref, vw_ref, og_ref, ow_ref):
    l = qg_ref.shape[1]
    qg, qw = qg_ref[0], qw_ref[0]
    g_parts = [[(_dot_nt(_stack_heads(qg, kv * G_GROUP, G_GROUP), _head(kg_ref[0], kv)), _head(vg_ref[0], kv))]
               for kv in range(G_KV)]
    w_parts = [[(_dot_nt(_stack_heads(qw, kv * W_GROUP, W_GROUP), _head(kw_ref[0], kv)), _head(vw_ref[0], kv))]
               for kv in range(W_KV)]
    og_ref[0] = _unstack_heads([_softmax_pv(p) for p in g_parts], l, G_GROUP).astype(BF16)
    outs = [_softmax_pv(p, _sink_column(sink_ref, kv, l)) for kv, p in enumerate(w_parts)]
    ow_ref[0] = _unstack_heads(outs, l, W_GROUP).astype(BF16)


def _ctx_attn(sink, qg, kg, vg, qw, kw, vw):
    b, l, _ = qg.shape
    spec = lambda w: pl.BlockSpec((1, l, w), lambda i: (i, 0, 0))
    padded = lambda w: spec(HEAD_PAD * w)
    return pl.pallas_call(
        _ctx_attn_kernel,
        grid=(b,),
        in_specs=[pl.BlockSpec(memory_space=pltpu.SMEM),
                  padded(GQ_W), padded(GKV_W), padded(GKV_W), padded(WQ_W), padded(WKV_W), padded(WKV_W)],
        out_specs=[spec(GQ_W), spec(WQ_W)],
        out_shape=[jax.ShapeDtypeStruct((b, l, GQ_W), BF16), jax.ShapeDtypeStruct((b, l, WQ_W), BF16)],
        compiler_params=_cparams("arbitrary"),
        name="ctx_attn",
    )(sink, qg, kg, vg, qw, kw, vw)


def _merge_kernel(x_ref, f_ref, og_ref, ow_ref, sh1_ref, sc1_ref, g1_ref, sh2_ref, sc2_ref,
                  n1_ref, n2_ref, wgate_ref, wbf_ref, wbg_ref, wbw_ref, wout_ref, wr_ref,
                  x1_ref, h2_ref, aff_ref, *, chunks):
    tm = x_ref.shape[1]
    rows = [slice(c * (tm // chunks), (c + 1) * (tm // chunks)) for c in range(chunks)]
    d = D_MODEL
    heads = []
    for r in rows:
        h = _norm_mod(x_ref[0, r], n1_ref[...], sh1_ref[0], sc1_ref[0]).astype(BF16)
        heads.append((_dot(h, wgate_ref[...]), _dot(f_ref[0, r], wbf_ref[...]),
                      _dot(og_ref[0, r], wbg_ref[...]), _dot(ow_ref[0, r], wbw_ref[...])))
    for r, (gate_logits, bf, bg, bw) in zip(rows, heads):
        gate = jax.nn.sigmoid(gate_logits)
        m = gate[:, 0:d] * bf + gate[:, d:2 * d] * bg + gate[:, 2 * d:3 * d] * bw
        x1 = x_ref[0, r] + g1_ref[0] * _dot(m.astype(BF16), wout_ref[...])
        x1_ref[0, r] = x1
        h2 = _norm_mod(x1, n2_ref[...], sh2_ref[0], sc2_ref[0]).astype(BF16)
        h2_ref[0, r] = h2
        logits = _dot(h2, wr_ref[...]).T[:N_EXPERTS]
        e = jnp.exp(logits - jnp.max(logits, axis=0, keepdims=True))
        aff_ref[0, :, r] = e / jnp.sum(e, axis=0, keepdims=True)


def _merge(x, f, og, ow, sh1, sc1, g1, sh2, sc2, n1, n2, wgate, wbf, wbg, wbw, wout, wr, *, tm, chunks):
    b, t, d = x.shape
    tok = lambda w: pl.BlockSpec((1, tm, w), lambda i, j: (i, j, 0))
    row = pl.BlockSpec((1, 1, d), lambda i, j: (i, 0, 0))
    const = lambda shape: pl.BlockSpec(shape, lambda i, j: (0,) * len(shape))
    return pl.pallas_call(
        functools.partial(_merge_kernel, chunks=chunks),
        grid=(b, t // tm),
        in_specs=[tok(d), tok(FOURIER_W), tok(GQ_W), tok(WQ_W), row, row, row, row, row,
                  const((1, d)), const((1, d)), const((d, GATE_W)), const((FOURIER_W, d)),
                  const((GQ_W, d)), const((WQ_W, d)), const((d, d)), const((d, LANES))],
        out_specs=[tok(d), tok(d), pl.BlockSpec((1, N_EXPERTS, tm), lambda i, j: (i, 0, j))],
        out_shape=[jax.ShapeDtypeStruct((b, t, d), F32), jax.ShapeDtypeStruct((b, t, d), BF16),
                   jax.ShapeDtypeStruct((b, N_EXPERTS, t), F32)],
        compiler_params=_cparams("arbitrary", "arbitrary"),
        name="merge",
    )(x, f, og, ow, sh1, sc1, g1, sh2, sc2, n1, n2, wgate, wbf, wbg, wbw, wout, wr)


def _cumsum_lanes(m, tri):
    e, n = m.shape
    nch = n // LANES
    stacked = jnp.concatenate([m[:, j * LANES:(j + 1) * LANES] for j in range(nch)], axis=0).astype(BF16)
    w = _dot(stacked, tri)
    outs, off = [], jnp.zeros((e, 1), F32)
    for j in range(nch):
        wj = w[j * e:(j + 1) * e]
        outs.append(wj + off)
        off = off + wj[:, LANES - 1:LANES]
    return jnp.concatenate(outs, axis=1)


def _route_kernel(aff_ref, tri_ref, pos_ref, rt_ref, *win_refs, cap, n_exp):
    aff = aff_ref[...]
    e, n = aff.shape
    thr_bits = jnp.zeros((e, 1), jnp.int32)
    for bit in range(30, -1, -1):
        cand = thr_bits | (1 << bit)
        cnt = jnp.sum(jnp.where(aff >= pltpu.bitcast(cand, F32), 1.0, 0.0), axis=1, keepdims=True)
        thr_bits = jnp.where(cnt >= cap, cand, thr_bits)
    ge = jnp.where(aff >= pltpu.bitcast(thr_bits, F32), 1.0, 0.0)
    gt = jnp.where(aff >= pltpu.bitcast(thr_bits + 1, F32), 1.0, 0.0)
    eq = ge - gt
    room = cap - jnp.sum(gt, axis=1, keepdims=True)
    tri = tri_ref[...]
    sel = gt + jnp.where(_cumsum_lanes(eq, tri) <= room, eq, 0.0)
    cums = _cumsum_lanes(sel, tri)
    pos = jnp.where(sel > 0.0, cums - 1.0, -1.0)
    pos_ref[...] = pos.astype(jnp.int32)
    weight = sel * aff
    pad = jnp.zeros((ROUTE_ROWS - 2 * n_exp, n), F32)
    for i in range(e // n_exp):
        rows = slice(i * n_exp, (i + 1) * n_exp)
        rt_ref[i] = jnp.concatenate([pos[rows], weight[rows], pad], axis=0).T
    if win_refs:
        win_ref, ok_ref = win_refs
        n_tiles = n // SCATTER_TILE
        ends = jnp.concatenate([cums[:, (k + 1) * SCATTER_TILE - 1:(k + 1) * SCATTER_TILE] for k in range(n_tiles)],
                               axis=1)
        starts = jnp.concatenate([jnp.zeros((e, 1), F32), ends[:, :n_tiles - 1]], axis=1)
        align = 16.0
        first = jnp.minimum(jnp.floor(starts / align) * align, float(cap - SCATTER_WINDOW))
        fits = jnp.where(ends - first <= SCATTER_WINDOW, 1.0, 0.0)
        win_ref[...] = first.astype(jnp.int32)
        ok = [jnp.min(fits[i * n_exp:(i + 1) * n_exp], axis=0, keepdims=True) for i in range(e // n_exp)]
        ok_ref[...] = jnp.concatenate(ok, axis=0).astype(jnp.int32)


def _route(aff, tri, *, cap, windows):
    b, e, n = aff.shape
    whole = lambda shape: pl.BlockSpec(shape, lambda i: (0,) * len(shape))
    out_shape = [jax.ShapeDtypeStruct((b * e, n), jnp.int32), jax.ShapeDtypeStruct((b, n, ROUTE_ROWS), F32)]
    if windows:
        n_tiles = n // SCATTER_TILE
        out_shape += [jax.ShapeDtypeStruct((b * e, n_tiles), jnp.int32), jax.ShapeDtypeStruct((b, n_tiles), jnp.int32)]
    outs = pl.pallas_call(
        functools.partial(_route_kernel, cap=cap, n_exp=e),
        grid=(1,),
        in_specs=[whole((b * e, n)), whole((LANES, LANES))],
        out_specs=[whole(s.shape) for s in out_shape],
        out_shape=out_shape,
        compiler_params=_cparams("arbitrary"),
        name="route",
    )(aff.reshape(b * e, n), tri)
    pos, rt = outs[0].reshape(b, e, n), outs[1]
    if not windows:
        return pos, rt
    win = outs[2].reshape(b, e, n_tiles).transpose(0, 2, 1)
    return pos, rt, win, outs[3]


def _gather_kernel(pos_ref, h_ref, o_ref, *, cap, group):
    j = pl.program_id(1)
    n = h_ref.shape[1]
    slot = lax.broadcasted_iota(jnp.int32, (cap, n), 0)
    sel = [jnp.where(pos_ref[0, pl.ds(j * group + g, 1), :] == slot, 1.0, 0.0).astype(BF16) for g in range(group)]
    o_ref[0] = _dot(jnp.concatenate(sel, axis=0), h_ref[0]).astype(BF16)


def _gather(pos, h, *, cap, group):
    b, e, n = pos.shape
    d = h.shape[2]
    return pl.pallas_call(
        functools.partial(_gather_kernel, cap=cap, group=group),
        grid=(b, e // group),
        in_specs=[pl.BlockSpec((1, e, n), lambda i, j: (i, 0, 0)), pl.BlockSpec((1, n, d), lambda i, j: (i, 0, 0))],
        out_specs=pl.BlockSpec((1, group * cap, d), lambda i, j: (i, j, 0)),
        out_shape=jax.ShapeDtypeStruct((b, e * cap, d), BF16),
        compiler_params=_cparams("arbitrary", "arbitrary"),
        name="moe_gather",
    )(pos, h)


def _expert_kernel(*refs, n_sets):
    x_refs, (wg_ref, wu_ref, wd_ref) = refs[:n_sets], refs[n_sets:n_sets + 3]
    o_refs = refs[n_sets + 3:2 * n_sets + 3]
    wg_s, wu_s, wd_s = refs[2 * n_sets + 3:]

    @pl.when(pl.program_id(1) == 0)
    def _():
        wg_s[...] = wg_ref[0, 0].astype(BF16)
        wu_s[...] = wu_ref[0, 0].astype(BF16)
        wd_s[...] = wd_ref[0, 0].astype(BF16)

    d = x_refs[0].shape[2]
    rows = [r.shape[0] * r.shape[1] for r in x_refs]
    xs = [r[...].reshape(n, d) for r, n in zip(x_refs, rows)]
    x = xs[0] if n_sets == 1 else jnp.concatenate(xs, axis=0)
    a = _dot(x, wg_s[...])
    u = _dot(x, wu_s[...])
    y = _dot((a * jax.nn.sigmoid(a) * u).astype(BF16), wd_s[...]).astype(BF16)
    start = 0
    for o_ref, n in zip(o_refs, rows):
        o_ref[...] = y[start:start + n].reshape(o_ref.shape)
        start += n


def _experts(xgs, caps, layer, wg, wu, wd, *, nb):
    b, _, d = xgs[0].shape
    _, e, _, f = wg.shape
    wspec = lambda r, c: pl.BlockSpec((1, 1, r, c), lambda i, j: (layer, i, 0, 0))
    xspecs = [pl.BlockSpec((nb, cap, d), lambda i, j: (j, i, 0)) for cap in caps]
    return pl.pallas_call(
        functools.partial(_expert_kernel, n_sets=len(xgs)),
        grid=(e, b // nb),
        in_specs=xspecs + [wspec(d, f), wspec(d, f), wspec(f, d)],
        out_specs=xspecs,
        out_shape=[jax.ShapeDtypeStruct(xg.shape, BF16) for xg in xgs],
        scratch_shapes=[pltpu.VMEM((d, f), BF16), pltpu.VMEM((d, f), BF16), pltpu.VMEM((f, d), BF16)],
        compiler_params=_cparams("arbitrary", "arbitrary"),
        name="moe_experts",
    )(*xgs, wg, wu, wd)


def _combine_matrix(rt, first, width):
    slot = lax.broadcasted_iota(jnp.int32, (rt.shape[0], width), 1).astype(F32)
    cols = [jnp.where(rt[:, e:e + 1] - first[e] == slot, rt[:, N_EXPERTS + e:N_EXPERTS + e + 1], 0.0).astype(BF16)
            for e in range(N_EXPERTS)]
    return jnp.concatenate(cols, axis=1)


def _scatter_kernel(*refs, cap, final, windowed):
    if windowed:
        win_ref, ok_ref, x_ref, rt_ref, y_ref, g2_ref, fg_ref, o_ref = refs
    else:
        x_ref, rt_ref, y_ref, g2_ref, fg_ref, o_ref = refs
    i, j = pl.program_id(0), pl.program_id(1)
    tn = x_ref.shape[1]
    sub = SCATTER_TILE if windowed else tn
    for u in range(tn // sub):
        rows = slice(u * sub, (u + 1) * sub)
        rt = rt_ref[0, rows]

        def full(rt=rt):
            return _dot(_combine_matrix(rt, [0.0] * N_EXPERTS, cap), y_ref[0])

        if windowed:
            tile = j * (tn // sub) + u

            def windows(rt=rt, tile=tile):
                first = [win_ref[i, tile, e] for e in range(N_EXPERTS)]
                picked = [y_ref[0, pl.ds(pl.multiple_of(e * cap + first[e], 16), SCATTER_WINDOW), :]
                          for e in range(N_EXPERTS)]
                a = _combine_matrix(rt, [f.astype(F32) for f in first], SCATTER_WINDOW)
                return _dot(a, jnp.concatenate(picked, axis=0))

            y = lax.cond(ok_ref[i, tile] > 0, windows, full)
        else:
            y = full()
        x = x_ref[0, rows] + g2_ref[0] * y
        if final:
            ms = jnp.mean(x * x, axis=-1, keepdims=True)
            x = x * lax.rsqrt(ms + EPS) * fg_ref[...]
        o_ref[0, rows] = x


def _scatter(x, rt, y, g2, fg, *, cap, final, tn, windows=None):
    b, t, d = x.shape
    smem = pl.BlockSpec(memory_space=pltpu.SMEM)
    windowed = windows is not None
    return pl.pallas_call(
        functools.partial(_scatter_kernel, cap=cap, final=final, windowed=windowed),
        grid=(b, t // tn),
        in_specs=([smem, smem] if windowed else []) + [
            pl.BlockSpec((1, tn, d), lambda i, j: (i, j, 0)),
            pl.BlockSpec((1, tn, ROUTE_ROWS), lambda i, j: (i, j, 0)),
            pl.BlockSpec((1, N_EXPERTS * cap, d), lambda i, j: (i, 0, 0)),
            pl.BlockSpec((1, 1, d), lambda i, j: (i, 0, 0)),
            pl.BlockSpec((1, d), lambda i, j: (0, 0))],
        out_specs=pl.BlockSpec((1, tn, d), lambda i, j: (i, j, 0)),
        out_shape=jax.ShapeDtypeStruct((b, t, d), F32),
        compiler_params=_cparams("arbitrary", "arbitrary"),
        name="moe_scatter",
    )(*(windows if windowed else ()), x, rt, y, g2, fg)


def _capacity(t):
    return CAPACITY_FACTOR * t // N_EXPERTS


def _dft_tables(n):
    k = np.arange(n, dtype=np.int64)
    ang = 2.0 * np.pi * ((k[:, None] * k[None, :]) % n).astype(np.float64) / n
    return np.cos(ang), np.sin(ang)


def _channel_dft():
    c, s = _dft_tables(FOURIER_GROUP_CH)
    eye = np.eye(FOURIER_GROUPS)
    return np.concatenate([np.kron(eye, c), np.kron(eye, s)], axis=1)


def _rope_tables(t):
    rows = t // GRID_W
    r, col = jnp.meshgrid(jnp.arange(rows), jnp.arange(GRID_W), indexing="ij")
    half = HEAD_DIM // 2
    inv = ROPE_THETA ** (-jnp.arange(0, half, 2, dtype=F32) / half)
    ang = jnp.concatenate([r.reshape(-1, 1).astype(F32) * inv, col.reshape(-1, 1).astype(F32) * inv], axis=-1)
    cos = jnp.repeat(jnp.cos(ang), 2, axis=1)
    sin = jnp.repeat(jnp.sin(ang), 2, axis=1) * jnp.tile(jnp.asarray([-1.0, 1.0], F32), half)
    return jnp.tile(cos, (1, HEAD_PAD)), jnp.tile(sin, (1, HEAD_PAD))


def kernel(x, c, ctx, c_ctx, w_ada, b_ada, norm1_g, w_in, q_norm_g, k_norm_g, sink, w_br_fourier, w_br_global,
           w_br_window, w_out, norm2_g, w_router, w_gate_e, w_up_e, w_down_e, final_g):
    b, t, d = x.shape
    l_ctx = ctx.shape[1]
    depth = w_ada.shape[0]
    cap_t, cap_c = _capacity(t), _capacity(l_ctx)

    cos_t, sin_t = _rope_tables(t)
    cos_c, sin_c = jnp.ones((l_ctx, LANES), F32), jnp.zeros((l_ctx, LANES), F32)
    head_avg = jnp.asarray(np.kron(np.eye(MXU_DIM // HEAD_DIM), np.full((HEAD_DIM, HEAD_DIM), 1.0 / HEAD_DIM)), BF16)
    cdft = jnp.asarray(_channel_dft(), F32)
    cn_t, sn_t = (jnp.asarray(a, F32) for a in _dft_tables(t))
    cn_c, sn_c = (jnp.asarray(a, F32) for a in _dft_tables(l_ctx))
    tri = jnp.asarray(np.triu(np.ones((LANES, LANES))), BF16)

    cvec = jnp.concatenate([c, c_ctx[None], jnp.zeros((MOD_ROWS - b - 1, d), F32)], axis=0)
    mods = _ada(cvec, w_ada, b_ada)

    xc = ctx
    for l in range(depth):
        need_ctx = l < depth - 1
        final = l == depth - 1
        lat = [mods[l, :b, i * d:(i + 1) * d].reshape(b, 1, d) for i in range(6)]
        cmod = [jnp.broadcast_to(mods[l, b, i * d:(i + 1) * d].reshape(1, 1, d), (b, 1, d)) for i in range(6)]
        w_mix = w_in[l, :, :MIX_W].astype(BF16)
        w_gate = w_in[l, :, MIX_W:].astype(BF16)
        n1, n2 = norm1_g[l].reshape(1, d), norm2_g[l].reshape(1, d)
        qn = jnp.tile(q_norm_g[l], G_HEADS).reshape(1, GQ_W)
        kn = jnp.tile(k_norm_g[l], G_KV).reshape(1, GKV_W)
        wbf, wbg, wbw = (w[l].astype(BF16) for w in (w_br_fourier, w_br_global, w_br_window))
        wout = w_out[l].astype(BF16)
        wr = jnp.pad(w_router[l], ((0, 0), (0, LANES - N_EXPERTS))).astype(BF16)
        fg = final_g.reshape(1, d)
        merge_w = (n1, n2, w_gate, wbf, wbg, wbw, wout, wr)

        cfcs, cqg, ckg, cvg, cqw, ckw, cvw = _inproj(xc, cmod[0], cmod[1], n1, w_mix, cos_c, sin_c, head_avg, cdft,
                                                     qn, kn, rope=False, tm=l_ctx, chunks=2)
        xgs, caps = [], []
        if need_ctx:
            cf_mix = _fourier(cfcs, cn_c, sn_c, tr=l_ctx)
            oc_g, oc_w = _ctx_attn(sink[l], cqg, ckg, cvg, cqw, ckw, cvw)
            xc1, hc2, caff = _merge(xc, cf_mix, oc_g, oc_w, *cmod[:5], *merge_w, tm=l_ctx, chunks=1)
            cpos, crt = _route(caff, tri, cap=cap_c, windows=False)
            xgs.append(_gather(cpos, hc2, cap=cap_c, group=4))
            caps.append(cap_c)

        fcs, qg, kg, vg, qw, kw, vw = _inproj(x, lat[0], lat[1], n1, w_mix, cos_t, sin_t, head_avg, cdft, qn, kn,
                                              rope=True, tm=512, chunks=4)
        f_mix = _fourier(fcs, cn_t, sn_t, tr=512)
        o_g = _global_attn(qg, kg, vg, ckg, cvg, tq=256)
        o_w = _window_attn(sink[l], qw, kw, vw, ckw, cvw)
        x1, h2, aff = _merge(x, f_mix, o_g, o_w, *lat[:5], *merge_w, tm=512, chunks=2)
        pos, rt, win, win_ok = _route(aff, tri, cap=cap_t, windows=True)
        xgs.insert(0, _gather(pos, h2, cap=cap_t, group=4))
        caps.insert(0, cap_t)

        ys = _experts(xgs, caps, l, w_gate_e, w_up_e, w_down_e, nb=4)
        x = _scatter(x1, rt, ys[0], lat[5], fg, cap=cap_t, final=final, tn=512, windows=(win, win_ok))
        if need_ctx:
            xc = _scatter(xc1, crt, ys[1], cmod[5], fg, cap=cap_c, final=False, tn=l_ctx)
    return x
```

```python
import functools

import numpy as np
import jax
import jax.numpy as jnp
from jax import lax
from jax.experimental import pallas as pl
from jax.experimental.pallas import tpu as pltpu

F32 = jnp.float32
BF16 = jnp.bfloat16

D_MODEL = 1024
HEAD_DIM = 64
GRID_W = 64
FOURIER_GROUPS = 4
FOURIER_GROUP_CH = 64
FOURIER_W = FOURIER_GROUPS * FOURIER_GROUP_CH
G_HEADS, G_KV = 8, 2
G_GROUP = G_HEADS // G_KV
W_HEADS, W_KV = 4, 2
W_GROUP = W_HEADS // W_KV
WINDOW = 128
Q_BLOCK = 128
N_BRANCH = 3
GQ_W = G_HEADS * HEAD_DIM
GKV_W = G_KV * HEAD_DIM
WQ_W = W_HEADS * HEAD_DIM
WKV_W = W_KV * HEAD_DIM
OFF_F = 0
OFF_GQ = OFF_F + FOURIER_W
OFF_GK = OFF_GQ + GQ_W
OFF_GV = OFF_GK + GKV_W
OFF_WQ = OFF_GV + GKV_W
OFF_WK = OFF_WQ + WQ_W
OFF_WV = OFF_WK + WKV_W
MIX_W = OFF_WV + WKV_W
GATE_W = N_BRANCH * D_MODEL
ROPE_THETA = 10000.0
N_EXPERTS = 16
CAPACITY_FACTOR = 2
EPS = 1e-6
NEG_INF = -1e30
LANES = 128
HEAD_PAD = LANES // HEAD_DIM
MXU_DIM = 256
MOD_ROWS = 16
SCATTER_TILE = 256
SCATTER_WINDOW = 64
ROUTE_ROWS = 128
WINDOW_TILE = 4
VMEM_LIMIT = 56 * 1024 * 1024


def _cparams(*sem):
    return pltpu.CompilerParams(dimension_semantics=sem, vmem_limit_bytes=VMEM_LIMIT)


def _norm_mod(x, g, sh, sc):
    ms = jnp.mean(x * x, axis=-1, keepdims=True)
    return (x * lax.rsqrt(ms + EPS) * g) * (1.0 + sc) + sh


def _dot(a, b):
    return jnp.dot(a, b, preferred_element_type=F32)


def _dot_nt(a, b):
    return lax.dot_general(a, b, (((1,), (1,)), ((), ())), preferred_element_type=F32)


def _ada_kernel(c_ref, w_ref, b_ref, o_ref):
    c = c_ref[...]
    s = (c * jax.nn.sigmoid(c)).astype(BF16)
    o_ref[0] = _dot(s, w_ref[0].astype(BF16)) + b_ref[0]


def _ada(cvec, w_ada, b_ada):
    depth, d, n = w_ada.shape
    tn = 1536
    return pl.pallas_call(
        _ada_kernel,
        grid=(depth, n // tn),
        in_specs=[pl.BlockSpec((MOD_ROWS, d), lambda l, j: (0, 0)),
                  pl.BlockSpec((1, d, tn), lambda l, j: (l, 0, j)),
                  pl.BlockSpec((1, 1, tn), lambda l, j: (l, 0, j))],
        out_specs=pl.BlockSpec((1, MOD_ROWS, tn), lambda l, j: (l, 0, j)),
        out_shape=jax.ShapeDtypeStruct((depth, MOD_ROWS, n), F32),
        compiler_params=_cparams("arbitrary", "arbitrary"),
        name="ada",
    )(cvec, w_ada, b_ada.reshape(depth, 1, n))


def _rope(x, cos, sin_signed, even):
    outs = []
    for j in range(x.shape[1] // LANES):
        xb = x[:, j * LANES:(j + 1) * LANES]
        swap = jnp.where(even, pltpu.roll(xb, LANES - 1, 1), pltpu.roll(xb, 1, 1))
        outs.append(xb * cos + swap * sin_signed)
    return outs[0] if len(outs) == 1 else jnp.concatenate(outs, axis=1)


def _head_mean_square(z, bd):
    zz = (z * z).astype(BF16)
    blk = bd.shape[0]
    if z.shape[1] < blk:
        return _dot(zz, bd[:z.shape[1], :z.shape[1]])
    outs = [_dot(zz[:, j * blk:(j + 1) * blk], bd) for j in range(z.shape[1] // blk)]
    return outs[0] if len(outs) == 1 else jnp.concatenate(outs, axis=1)


def _pad_heads_f32(x, fill):
    blk = jnp.full((x.shape[0], LANES - HEAD_DIM), fill, F32)
    parts = []
    for h in range(x.shape[1] // HEAD_DIM):
        parts += [x[:, h * HEAD_DIM:(h + 1) * HEAD_DIM], blk]
    return jnp.concatenate(parts, axis=1)


def _pad_heads(x, fill):
    return _pad_heads_f32(x, fill).astype(BF16)


def _inproj_kernel(x_ref, sh_ref, sc_ref, g_ref, w_ref, cos_ref, sin_ref, bd_ref, cdft_ref, qn_ref, kn_ref,
                   fcs_ref, qg_ref, kg_ref, vg_ref, qw_ref, kw_ref, vw_ref, *, rope, chunks):
    tm = x_ref.shape[1]
    rows = [slice(c * (tm // chunks), (c + 1) * (tm // chunks)) for c in range(chunks)]
    zs = [_dot(_norm_mod(x_ref[0, r], g_ref[...], sh_ref[0], sc_ref[0]).astype(BF16), w_ref[...]) for r in rows]
    bd = bd_ref[...]
    scale = HEAD_DIM ** -0.5
    for r, z in zip(rows, zs):
        fcs_ref[0, r] = _dot(z[:, OFF_F:OFF_F + FOURIER_W].astype(BF16), cdft_ref[...].astype(BF16)).astype(BF16)
        q = z[:, OFF_GQ:OFF_GQ + GQ_W]
        k = z[:, OFF_GK:OFF_GK + GKV_W]
        q = q * lax.rsqrt(_head_mean_square(q, bd) + EPS) * qn_ref[...]
        k = k * lax.rsqrt(_head_mean_square(k, bd) + EPS) * kn_ref[...]
        qw = z[:, OFF_WQ:OFF_WQ + WQ_W]
        kw = z[:, OFF_WK:OFF_WK + WKV_W]
        if rope:
            cos, sin = cos_ref[r], sin_ref[r]
            even = (lax.broadcasted_iota(jnp.int32, cos.shape, 1) % 2) == 0
            q, k = _rope(q, cos, sin, even), _rope(k, cos, sin, even)
            qw, kw = _rope(qw, cos, sin, even), _rope(kw, cos, sin, even)
        qg_ref[0, r] = _pad_heads(q * scale, 0.0)
        kg_ref[0, r] = _pad_heads(k, 0.0)
        vg_ref[0, r] = _pad_heads(z[:, OFF_GV:OFF_GV + GKV_W], 1.0)
        qw_ref[0, r] = _pad_heads(qw * scale, 0.0)
        kw_ref[0, r] = _pad_heads(kw, 0.0)
        vw_ref[0, r] = _pad_heads(z[:, OFF_WV:OFF_WV + WKV_W], 1.0)


def _inproj(x, sh, sc, g, w_mix, cos_t, sin_t, bd, cdft, qn, kn, *, rope, tm, chunks):
    b, t, d = x.shape
    tok = lambda w: pl.BlockSpec((1, tm, w), lambda i, j: (i, j, 0))
    row = pl.BlockSpec((1, 1, d), lambda i, j: (i, 0, 0))
    const = lambda shape: pl.BlockSpec(shape, lambda i, j: (0,) * len(shape))
    widths = (2 * FOURIER_W,) + tuple(HEAD_PAD * w for w in (GQ_W, GKV_W, GKV_W, WQ_W, WKV_W, WKV_W))
    out_specs = [tok(w) for w in widths]
    out_shape = [jax.ShapeDtypeStruct((b, t, w), BF16) for w in widths]
    return pl.pallas_call(
        functools.partial(_inproj_kernel, rope=rope, chunks=chunks),
        grid=(b, t // tm),
        in_specs=[tok(d), row, row, const((1, d)), const((d, MIX_W)),
                  pl.BlockSpec((tm, LANES), lambda i, j: (j, 0)),
                  pl.BlockSpec((tm, LANES), lambda i, j: (j, 0)),
                  const((MXU_DIM, MXU_DIM)), const((FOURIER_W, 2 * FOURIER_W)),
                  const((1, GQ_W)), const((1, GKV_W))],
        out_specs=out_specs,
        out_shape=out_shape,
        compiler_params=_cparams("arbitrary", "arbitrary"),
        name="inproj_rope" if rope else "inproj_ctx",
    )(x, sh, sc, g, w_mix, cos_t, sin_t, bd, cdft, qn, kn)


def _fourier_kernel(cn_ref, sn_ref, fcs_ref, o_ref, cn_s, sn_s, *, scale):
    @pl.when(pl.program_id(1) == 0)
    def _():
        cn_s[...] = cn_ref[...].astype(BF16)
        sn_s[...] = sn_ref[...].astype(BF16)

    fcs = fcs_ref[0]
    re = _dot(cn_s[...], fcs[:, :FOURIER_W]) - _dot(sn_s[...], fcs[:, FOURIER_W:])
    o_ref[0] = (re * scale).astype(BF16)


def _fourier(fcs, cn, sn, *, tr):
    b, n, _ = fcs.shape
    scale = float((n * FOURIER_GROUP_CH) ** -0.5)
    return pl.pallas_call(
        functools.partial(_fourier_kernel, scale=scale),
        grid=(n // tr, b),
        in_specs=[pl.BlockSpec((tr, n), lambda r, i: (r, 0)),
                  pl.BlockSpec((tr, n), lambda r, i: (r, 0)),
                  pl.BlockSpec((1, n, 2 * FOURIER_W), lambda r, i: (i, 0, 0))],
        out_specs=pl.BlockSpec((1, tr, FOURIER_W), lambda r, i: (i, r, 0)),
        out_shape=jax.ShapeDtypeStruct((b, n, FOURIER_W), BF16),
        scratch_shapes=[pltpu.VMEM((tr, n), BF16), pltpu.VMEM((tr, n), BF16)],
        compiler_params=_cparams("arbitrary", "arbitrary"),
        name="fourier",
    )(cn, sn, fcs)


def _head(x, h):
    return x[:, h * LANES:(h + 1) * LANES]


def _stack_heads(q, first, count):
    return jnp.concatenate([_head(q, first + g) for g in range(count)], axis=0)


def _softmax_pv(parts, extra=None):
    m = functools.reduce(jnp.maximum, [jnp.max(s, axis=-1, keepdims=True) for s, _ in parts])
    if extra is not None:
        m = jnp.maximum(m, extra)
    acc = 0.0
    for s, v in parts:
        acc = acc + _dot(jnp.exp((s - m).astype(BF16)), v)
    den = acc[:, HEAD_DIM:HEAD_DIM + 1]
    if extra is not None:
        den = den + jnp.exp(extra - m)
    return acc[:, :HEAD_DIM] / den


def _unstack_heads(o_list, tq, count):
    cols = []
    for o in o_list:
        cols += [o[g * tq:(g + 1) * tq] for g in range(count)]
    return jnp.concatenate(cols, axis=1)


def _global_attn_kernel(q_ref, kl_ref, vl_ref, kc_ref, vc_ref, o_ref):
    q = q_ref[0]
    tq = q.shape[0]
    parts = []
    for kv in range(G_KV):
        qs = _stack_heads(q, kv * G_GROUP, G_GROUP)
        parts.append([(_dot_nt(qs, _head(kc_ref[0], kv)), _head(vc_ref[0], kv)),
                      (_dot_nt(qs, _head(kl_ref[0], kv)), _head(vl_ref[0], kv))])
    outs = [_softmax_pv(p) for p in parts]
    o_ref[0] = _unstack_heads(outs, tq, G_GROUP).astype(BF16)


def _global_attn(q, kl, vl, kc, vc, *, tq):
    b, t, _ = q.shape
    l = kc.shape[1]
    full = lambda n: pl.BlockSpec((1, n, HEAD_PAD * GKV_W), lambda i, j: (i, 0, 0))
    return pl.pallas_call(
        _global_attn_kernel,
        grid=(b, t // tq),
        in_specs=[pl.BlockSpec((1, tq, HEAD_PAD * GQ_W), lambda i, j: (i, j, 0)), full(t), full(t), full(l), full(l)],
        out_specs=pl.BlockSpec((1, tq, GQ_W), lambda i, j: (i, j, 0)),
        out_shape=jax.ShapeDtypeStruct((b, t, GQ_W), BF16),
        compiler_params=_cparams("arbitrary", "arbitrary"),
        name="global_attn",
    )(q, kl, vl, kc, vc)


def _sink_column(sink_ref, kv, rows_per_head):
    r = lax.broadcasted_iota(jnp.int32, (W_GROUP * rows_per_head, 1), 0)
    col = jnp.full(r.shape, sink_ref[kv * W_GROUP], F32)
    for g in range(1, W_GROUP):
        col = jnp.where(r >= g * rows_per_head, sink_ref[kv * W_GROUP + g], col)
    return col


def _window_attn_kernel(sink_ref, q_ref, kp_ref, k0_ref, kn_ref, vp_ref, v0_ref, vn_ref, kc_ref, vc_ref, o_ref,
                        *, n_tiles):
    j = pl.program_id(1)
    q = q_ref[0]
    kb = jnp.concatenate([kp_ref[0], k0_ref[0], kn_ref[0]], axis=0)
    vb = jnp.concatenate([vp_ref[0], v0_ref[0], vn_ref[0]], axis=0)
    rows = W_GROUP * Q_BLOCK
    qpos = lax.broadcasted_iota(jnp.int32, (rows, 3 * Q_BLOCK), 0) % Q_BLOCK
    kpos = lax.broadcasted_iota(jnp.int32, (rows, 3 * Q_BLOCK), 1)
    dist = jnp.abs(qpos - kpos + Q_BLOCK)
    in_window = dist <= WINDOW
    outside = jnp.full(dist.shape, WINDOW + 1, jnp.int32)
    first_ok = jnp.where((kpos < Q_BLOCK) & (j == 0), outside, dist) <= WINDOW
    last_ok = jnp.where((kpos >= 2 * Q_BLOCK) & (j == n_tiles - 1), outside, dist) <= WINDOW
    parts = []
    for i in range(WINDOW_TILE):
        valid = first_ok if i == 0 else (last_ok if i == WINDOW_TILE - 1 else in_window)
        q_i = q[i * Q_BLOCK:(i + 1) * Q_BLOCK]
        band = slice(i * Q_BLOCK, (i + 3) * Q_BLOCK)
        for kv in range(W_KV):
            qs = _stack_heads(q_i, kv * W_GROUP, W_GROUP)
            s_b = jnp.where(valid, _dot_nt(qs, _head(kb, kv)[band]), NEG_INF)
            parts.append([(_dot_nt(qs, _head(kc_ref[0], kv)), _head(vc_ref[0], kv)), (s_b, _head(vb, kv)[band])])
    sinks = [_sink_column(sink_ref, kv, Q_BLOCK) for kv in range(W_KV)]
    outs = [_softmax_pv(p, sinks[n % W_KV]) for n, p in enumerate(parts)]
    blocks = [_unstack_heads(outs[i * W_KV:(i + 1) * W_KV], Q_BLOCK, W_GROUP) for i in range(WINDOW_TILE)]
    o_ref[0] = jnp.concatenate(blocks, axis=0).astype(BF16)


def _window_attn(sink, q, k, v, kc, vc):
    b, t, _ = q.shape
    l = kc.shape[1]
    nb = t // Q_BLOCK
    tile = WINDOW_TILE * Q_BLOCK
    kvw = HEAD_PAD * WKV_W
    edge = lambda f: pl.BlockSpec((1, Q_BLOCK, kvw), lambda i, j: (i, f(j), 0))
    prev = lambda j: jnp.maximum(j * WINDOW_TILE - 1, 0)
    nxt = lambda j: jnp.minimum((j + 1) * WINDOW_TILE, nb - 1)
    mid = pl.BlockSpec((1, tile, kvw), lambda i, j: (i, j, 0))
    full = pl.BlockSpec((1, l, kvw), lambda i, j: (i, 0, 0))
    return pl.pallas_call(
        functools.partial(_window_attn_kernel, n_tiles=t // tile),
        grid=(b, t // tile),
        in_specs=[pl.BlockSpec(memory_space=pltpu.SMEM),
                  pl.BlockSpec((1, tile, HEAD_PAD * WQ_W), lambda i, j: (i, j, 0)),
                  edge(prev), mid, edge(nxt), edge(prev), mid, edge(nxt), full, full],
        out_specs=pl.BlockSpec((1, tile, WQ_W), lambda i, j: (i, j, 0)),
        out_shape=jax.ShapeDtypeStruct((b, t, WQ_W), BF16),
        compiler_params=_cparams("arbitrary", "arbitrary"),
        name="window_attn",
    )(sink, q, k, k, k, v, v, v, kc, vc)


def _ctx_attn_kernel(sink_ref, qg_ref, kg_ref, vg_ref, qw_ref, kw_ref, vw_ref, og_ref, ow_ref):
    l = qg_ref.shape[1]
    qg, qw = qg_ref[0], qw_ref[0]
    g_parts = [[(_dot_nt(_stack_heads(qg, kv * G_GROUP, G_GROUP), _head(kg_ref[0], kv)), _head(vg_ref[0], kv))]
               for kv in range(G_KV)]
    w_parts = [[(_dot_nt(_stack_heads(qw, kv * W_GROUP, W_GROUP), _head(kw_ref[0], kv)), _head(vw_ref[0], kv))]
               for kv in range(W_KV)]
    og_ref[0] = _unstack_heads([_softmax_pv(p) for p in g_parts], l, G_GROUP).astype(BF16)
    outs = [_softmax_pv(p, _sink_column(sink_ref, kv, l)) for kv, p in enumerate(w_parts)]
    ow_ref[0] = _unstack_heads(outs, l, W_GROUP).astype(BF16)


def _ctx_attn(sink, qg, kg, vg, qw, kw, vw):
    b, l, _ = qg.shape
    spec = lambda w: pl.BlockSpec((1, l, w), lambda i: (i, 0, 0))
    padded = lambda w: spec(HEAD_PAD * w)
    return pl.pallas_call(
        _ctx_attn_kernel,
        grid=(b,),
        in_specs=[pl.BlockSpec(memory_space=pltpu.SMEM),
                  padded(GQ_W), padded(GKV_W), padded(GKV_W), padded(WQ_W), padded(WKV_W), padded(WKV_W)],
        out_specs=[spec(GQ_W), spec(WQ_W)],
        out_shape=[jax.ShapeDtypeStruct((b, l, GQ_W), BF16), jax.ShapeDtypeStruct((b, l, WQ_W), BF16)],
        compiler_params=_cparams("arbitrary"),
        name="ctx_attn",
    )(sink, qg, kg, vg, qw, kw, vw)


def _merge_kernel(x_ref, f_ref, og_ref, ow_ref, sh1_ref, sc1_ref, g1_ref, sh2_ref, sc2_ref,
                  n1_ref, n2_ref, wgate_ref, wbf_ref, wbg_ref, wbw_ref, wout_ref, wr_ref,
                  x1_ref, h2_ref, aff_ref, *, chunks):
    tm = x_ref.shape[1]
    rows = [slice(c * (tm // chunks), (c + 1) * (tm // chunks)) for c in range(chunks)]
    d = D_MODEL
    heads = []
    for r in rows:
        h = _norm_mod(x_ref[0, r], n1_ref[...], sh1_ref[0], sc1_ref[0]).astype(BF16)
        heads.append((_dot(h, wgate_ref[...]), _dot(f_ref[0, r], wbf_ref[...]),
                      _dot(og_ref[0, r], wbg_ref[...]), _dot(ow_ref[0, r], wbw_ref[...])))
    for r, (gate_logits, bf, bg, bw) in zip(rows, heads):
        gate = jax.nn.sigmoid(gate_logits)
        m = gate[:, 0:d] * bf + gate[:, d:2 * d] * bg + gate[:, 2 * d:3 * d] * bw
        x1 = x_ref[0, r] + g1_ref[0] * _dot(m.astype(BF16), wout_ref[...])
        x1_ref[0, r] = x1
        h2 = _norm_mod(x1, n2_ref[...], sh2_ref[0], sc2_ref[0]).astype(BF16)
        h2_ref[0, r] = h2
        logits = _dot(h2, wr_ref[...]).T[:N_EXPERTS]
        e = jnp.exp(logits - jnp.max(logits, axis=0, keepdims=True))
        aff_ref[0, :, r] = e / jnp.sum(e, axis=0, keepdims=True)


def _merge(x, f, og, ow, sh1, sc1, g1, sh2, sc2, n1, n2, wgate, wbf, wbg, wbw, wout, wr, *, tm, chunks):
    b, t, d = x.shape
    tok = lambda w: pl.BlockSpec((1, tm, w), lambda i, j: (i, j, 0))
    row = pl.BlockSpec((1, 1, d), lambda i, j: (i, 0, 0))
    const = lambda shape: pl.BlockSpec(shape, lambda i, j: (0,) * len(shape))
    return pl.pallas_call(
        functools.partial(_merge_kernel, chunks=chunks),
        grid=(b, t // tm),
        in_specs=[tok(d), tok(FOURIER_W), tok(GQ_W), tok(WQ_W), row, row, row, row, row,
                  const((1, d)), const((1, d)), const((d, GATE_W)), const((FOURIER_W, d)),
                  const((GQ_W, d)), const((WQ_W, d)), const((d, d)), const((d, LANES))],
        out_specs=[tok(d), tok(d), pl.BlockSpec((1, N_EXPERTS, tm), lambda i, j: (i, 0, j))],
        out_shape=[jax.ShapeDtypeStruct((b, t, d), F32), jax.ShapeDtypeStruct((b, t, d), BF16),
                   jax.ShapeDtypeStruct((b, N_EXPERTS, t), F32)],
        compiler_params=_cparams("arbitrary", "arbitrary"),
        name="merge",
    )(x, f, og, ow, sh1, sc1, g1, sh2, sc2, n1, n2, wgate, wbf, wbg, wbw, wout, wr)


def _cumsum_lanes(m, tri):
    e, n = m.shape
    nch = n // LANES
    stacked = jnp.concatenate([m[:, j * LANES:(j + 1) * LANES] for j in range(nch)], axis=0).astype(BF16)
    w = _dot(stacked, tri)
    outs, off = [], jnp.zeros((e, 1), F32)
    for j in range(nch):
        wj = w[j * e:(j + 1) * e]
        outs.append(wj + off)
        off = off + wj[:, LANES - 1:LANES]
    return jnp.concatenate(outs, axis=1)


def _route_kernel(aff_ref, tri_ref, pos_ref, rt_ref, *win_refs, cap, n_exp):
    aff = aff_ref[...]
    e, n = aff.shape
    thr_bits = jnp.zeros((e, 1), jnp.int32)
    for bit in range(30, -1, -1):
        cand = thr_bits | (1 << bit)
        cnt = jnp.sum(jnp.where(aff >= pltpu.bitcast(cand, F32), 1.0, 0.0), axis=1, keepdims=True)
        thr_bits = jnp.where(cnt >= cap, cand, thr_bits)
    ge = jnp.where(aff >= pltpu.bitcast(thr_bits, F32), 1.0, 0.0)
    gt = jnp.where(aff >= pltpu.bitcast(thr_bits + 1, F32), 1.0, 0.0)
    eq = ge - gt
    room = cap - jnp.sum(gt, axis=1, keepdims=True)
    tri = tri_ref[...]
    sel = gt + jnp.where(_cumsum_lanes(eq, tri) <= room, eq, 0.0)
    cums = _cumsum_lanes(sel, tri)
    pos = jnp.where(sel > 0.0, cums - 1.0, -1.0)
    pos_ref[...] = pos.astype(jnp.int32)
    weight = sel * aff
    pad = jnp.zeros((ROUTE_ROWS - 2 * n_exp, n), F32)
    for i in range(e // n_exp):
        rows = slice(i * n_exp, (i + 1) * n_exp)
        rt_ref[i] = jnp.concatenate([pos[rows], weight[rows], pad], axis=0).T
    if win_refs:
        win_ref, ok_ref = win_refs
        n_tiles = n // SCATTER_TILE
        ends = jnp.concatenate([cums[:, (k + 1) * SCATTER_TILE - 1:(k + 1) * SCATTER_TILE] for k in range(n_tiles)],
                               axis=1)
        starts = jnp.concatenate([jnp.zeros((e, 1), F32), ends[:, :n_tiles - 1]], axis=1)
        align = 16.0
        first = jnp.minimum(jnp.floor(starts / align) * align, float(cap - SCATTER_WINDOW))
        fits = jnp.where(ends - first <= SCATTER_WINDOW, 1.0, 0.0)
        win_ref[...] = first.astype(jnp.int32)
        ok = [jnp.min(fits[i * n_exp:(i + 1) * n_exp], axis=0, keepdims=True) for i in range(e // n_exp)]
        ok_ref[...] = jnp.concatenate(ok, axis=0).astype(jnp.int32)


def _route(aff, tri, *, cap, windows):
    b, e, n = aff.shape
    whole = lambda shape: pl.BlockSpec(shape, lambda i: (0,) * len(shape))
    out_shape = [jax.ShapeDtypeStruct((b * e, n), jnp.int32), jax.ShapeDtypeStruct((b, n, ROUTE_ROWS), F32)]
    if windows:
        n_tiles = n // SCATTER_TILE
        out_shape += [jax.ShapeDtypeStruct((b * e, n_tiles), jnp.int32), jax.ShapeDtypeStruct((b, n_tiles), jnp.int32)]
    outs = pl.pallas_call(
        functools.partial(_route_kernel, cap=cap, n_exp=e),
        grid=(1,),
        in_specs=[whole((b * e, n)), whole((LANES, LANES))],
        out_specs=[whole(s.shape) for s in out_shape],
        out_shape=out_shape,
        compiler_params=_cparams("arbitrary"),
        name="route",
    )(aff.reshape(b * e, n), tri)
    pos, rt = outs[0].reshape(b, e, n), outs[1]
    if not windows:
        return pos, rt
    win = outs[2].reshape(b, e, n_tiles).transpose(0, 2, 1)
    return pos, rt, win, outs[3]


def _gather_kernel(pos_ref, h_ref, o_ref, *, cap, group):
    j = pl.program_id(1)
    n = h_ref.shape[1]
    slot = lax.broadcasted_iota(jnp.int32, (cap, n), 0)
    sel = [jnp.where(pos_ref[0, pl.ds(j * group + g, 1), :] == slot, 1.0, 0.0).astype(BF16) for g in range(group)]
    o_ref[0] = _dot(jnp.concatenate(sel, axis=0), h_ref[0]).astype(BF16)


def _gather(pos, h, *, cap, group):
    b, e, n = pos.shape
    d = h.shape[2]
    return pl.pallas_call(
        functools.partial(_gather_kernel, cap=cap, group=group),
        grid=(b, e // group),
        in_specs=[pl.BlockSpec((1, e, n), lambda i, j: (i, 0, 0)), pl.BlockSpec((1, n, d), lambda i, j: (i, 0, 0))],
        out_specs=pl.BlockSpec((1, group * cap, d), lambda i, j: (i, j, 0)),
        out_shape=jax.ShapeDtypeStruct((b, e * cap, d), BF16),
        compiler_params=_cparams("arbitrary", "arbitrary"),
        name="moe_gather",
    )(pos, h)


def _expert_kernel(*refs, n_sets):
    x_refs, (wg_ref, wu_ref, wd_ref) = refs[:n_sets], refs[n_sets:n_sets + 3]
    o_refs = refs[n_sets + 3:2 * n_sets + 3]
    d = x_refs[0].shape[2]
    rows = [r.shape[0] * r.shape[1] for r in x_refs]
    xs = [r[...].reshape(n, d) for r, n in zip(x_refs, rows)]
    x = xs[0] if n_sets == 1 else jnp.concatenate(xs, axis=0)
    a = _dot(x, wg_ref[0, 0].astype(BF16))
    u = _dot(x, wu_ref[0, 0].astype(BF16))
    y = _dot((a * jax.nn.sigmoid(a) * u).astype(BF16), wd_ref[0, 0].astype(BF16)).astype(BF16)
    start = 0
    for o_ref, n in zip(o_refs, rows):
        o_ref[...] = y[start:start + n].reshape(o_ref.shape)
        start += n


def _experts(xgs, caps, layer, wg, wu, wd, *, nb):
    b, _, d = xgs[0].shape
    _, e, _, f = wg.shape
    wspec = lambda r, c: pl.BlockSpec((1, 1, r, c), lambda i, j: (layer, i, 0, 0))
    xspecs = [pl.BlockSpec((nb, cap, d), lambda i, j: (j, i, 0)) for cap in caps]
    return pl.pallas_call(
        functools.partial(_expert_kernel, n_sets=len(xgs)),
        grid=(e, b // nb),
        in_specs=xspecs + [wspec(d, f), wspec(d, f), wspec(f, d)],
        out_specs=xspecs,
        out_shape=[jax.ShapeDtypeStruct(xg.shape, BF16) for xg in xgs],
        compiler_params=_cparams("arbitrary", "arbitrary"),
        name="moe_experts",
    )(*xgs, wg, wu, wd)


def _combine_matrix(rt, first, width):
    slot = lax.broadcasted_iota(jnp.int32, (rt.shape[0], width), 1).astype(F32)
    cols = [jnp.where(rt[:, e:e + 1] - first[e] == slot, rt[:, N_EXPERTS + e:N_EXPERTS + e + 1], 0.0).astype(BF16)
            for e in range(N_EXPERTS)]
    return jnp.concatenate(cols, axis=1)


def _scatter_kernel(*refs, cap, final, windowed):
    if windowed:
        win_ref, ok_ref, x_ref, rt_ref, y_ref, g2_ref, fg_ref, o_ref = refs
    else:
        x_ref, rt_ref, y_ref, g2_ref, fg_ref, o_ref = refs
    i, j = pl.program_id(0), pl.program_id(1)
    tn = x_ref.shape[1]

    def finish(rows, y):
        x = x_ref[0, rows] + g2_ref[0] * y
        if final:
            ms = jnp.mean(x * x, axis=-1, keepdims=True)
            x = x * lax.rsqrt(ms + EPS) * fg_ref[...]
        o_ref[0, rows] = x

    def full():
        rows = slice(0, tn)
        finish(rows, _dot(_combine_matrix(rt_ref[0], [0.0] * N_EXPERTS, cap), y_ref[0]))

    if not windowed:
        full()
        return
    subs = tn // SCATTER_TILE
    tiles = [j * subs + u for u in range(subs)]

    def windows():
        ys = []
        for u, tile in enumerate(tiles):
            rt = rt_ref[0, u * SCATTER_TILE:(u + 1) * SCATTER_TILE]
            first = [win_ref[i, tile, e] for e in range(N_EXPERTS)]
            picked = [y_ref[0, pl.ds(pl.multiple_of(e * cap + first[e], 16), SCATTER_WINDOW), :]
                      for e in range(N_EXPERTS)]
            a = _combine_matrix(rt, [f.astype(F32) for f in first], SCATTER_WINDOW)
            ys.append(_dot(a, jnp.concatenate(picked, axis=0)))
        for u, y in enumerate(ys):
            finish(slice(u * SCATTER_TILE, (u + 1) * SCATTER_TILE), y)

    all_fit = functools.reduce(jnp.minimum, [ok_ref[i, tile] for tile in tiles])
    lax.cond(all_fit > 0, windows, full)


def _scatter(x, rt, y, g2, fg, *, cap, final, tn, windows=None):
    b, t, d = x.shape
    smem = pl.BlockSpec(memory_space=pltpu.SMEM)
    windowed = windows is not None
    return pl.pallas_call(
        functools.partial(_scatter_kernel, cap=cap, final=final, windowed=windowed),
        grid=(b, t // tn),
        in_specs=([smem, smem] if windowed else []) + [
            pl.BlockSpec((1, tn, d), lambda i, j: (i, j, 0)),
            pl.BlockSpec((1, tn, ROUTE_ROWS), lambda i, j: (i, j, 0)),
            pl.BlockSpec((1, N_EXPERTS * cap, d), lambda i, j: (i, 0, 0)),
            pl.BlockSpec((1, 1, d), lambda i, j: (i, 0, 0)),
            pl.BlockSpec((1, d), lambda i, j: (0, 0))],
        out_specs=pl.BlockSpec((1, tn, d), lambda i, j: (i, j, 0)),
        out_shape=jax.ShapeDtypeStruct((b, t, d), F32),
        compiler_params=_cparams("arbitrary", "arbitrary"),
        name="moe_scatter",
    )(*(windows if windowed else ()), x, rt, y, g2, fg)


def _capacity(t):
    return CAPACITY_FACTOR * t // N_EXPERTS


def _dft_tables(n):
    k = np.arange(n, dtype=np.int64)
    ang = 2.0 * np.pi * ((k[:, None] * k[None, :]) % n).astype(np.float64) / n
    return np.cos(ang), np.sin(ang)


def _channel_dft():
    c, s = _dft_tables(FOURIER_GROUP_CH)
    eye = np.eye(FOURIER_GROUPS)
    return np.concatenate([np.kron(eye, c), np.kron(eye, s)], axis=1)


def _rope_tables(t):
    rows = t // GRID_W
    r, col = jnp.meshgrid(jnp.arange(rows), jnp.arange(GRID_W), indexing="ij")
    half = HEAD_DIM // 2
    inv = ROPE_THETA ** (-jnp.arange(0, half, 2, dtype=F32) / half)
    ang = jnp.concatenate([r.reshape(-1, 1).astype(F32) * inv, col.reshape(-1, 1).astype(F32) * inv], axis=-1)
    cos = jnp.repeat(jnp.cos(ang), 2, axis=1)
    sin = jnp.repeat(jnp.sin(ang), 2, axis=1) * jnp.tile(jnp.asarray([-1.0, 1.0], F32), half)
    return jnp.tile(cos, (1, HEAD_PAD)), jnp.tile(sin, (1, HEAD_PAD))


def kernel(x, c, ctx, c_ctx, w_ada, b_ada, norm1_g, w_in, q_norm_g, k_norm_g, sink, w_br_fourier, w_br_global,
           w_br_window, w_out, norm2_g, w_router, w_gate_e, w_up_e, w_down_e, final_g):
    b, t, d = x.shape
    l_ctx = ctx.shape[1]
    depth = w_ada.shape[0]
    cap_t, cap_c = _capacity(t), _capacity(l_ctx)

    cos_t, sin_t = _rope_tables(t)
    cos_c, sin_c = jnp.ones((l_ctx, LANES), F32), jnp.zeros((l_ctx, LANES), F32)
    head_avg = jnp.asarray(np.kron(np.eye(MXU_DIM // HEAD_DIM), np.full((HEAD_DIM, HEAD_DIM), 1.0 / HEAD_DIM)), BF16)
    cdft = jnp.asarray(_channel_dft(), F32)
    cn_t, sn_t = (jnp.asarray(a, F32) for a in _dft_tables(t))
    cn_c, sn_c = (jnp.asarray(a, F32) for a in _dft_tables(l_ctx))
    tri = jnp.asarray(np.triu(np.ones((LANES, LANES))), BF16)

    cvec = jnp.concatenate([c, c_ctx[None], jnp.zeros((MOD_ROWS - b - 1, d), F32)], axis=0)
    mods = _ada(cvec, w_ada, b_ada)

    xc = ctx
    for l in range(depth):
        need_ctx = l < depth - 1
        final = l == depth - 1
        lat = [mods[l, :b, i * d:(i + 1) * d].reshape(b, 1, d) for i in range(6)]
        cmod = [jnp.broadcast_to(mods[l, b, i * d:(i + 1) * d].reshape(1, 1, d), (b, 1, d)) for i in range(6)]
        w_mix = w_in[l, :, :MIX_W].astype(BF16)
        w_gate = w_in[l, :, MIX_W:].astype(BF16)
        n1, n2 = norm1_g[l].reshape(1, d), norm2_g[l].reshape(1, d)
        qn = jnp.tile(q_norm_g[l], G_HEADS).reshape(1, GQ_W)
        kn = jnp.tile(k_norm_g[l], G_KV).reshape(1, GKV_W)
        wbf, wbg, wbw = (w[l].astype(BF16) for w in (w_br_fourier, w_br_global, w_br_window))
        wout = w_out[l].astype(BF16)
        wr = jnp.pad(w_router[l], ((0, 0), (0, LANES - N_EXPERTS))).astype(BF16)
        fg = final_g.reshape(1, d)
        merge_w = (n1, n2, w_gate, wbf, wbg, wbw, wout, wr)

        cfcs, cqg, ckg, cvg, cqw, ckw, cvw = _inproj(xc, cmod[0], cmod[1], n1, w_mix, cos_c, sin_c, head_avg, cdft,
                                                     qn, kn, rope=False, tm=l_ctx, chunks=2)
        xgs, caps = [], []
        if need_ctx:
            cf_mix = _fourier(cfcs, cn_c, sn_c, tr=l_ctx)
            oc_g, oc_w = _ctx_attn(sink[l], cqg, ckg, cvg, cqw, ckw, cvw)
            xc1, hc2, caff = _merge(xc, cf_mix, oc_g, oc_w, *cmod[:5], *merge_w, tm=l_ctx, chunks=1)
            cpos, crt = _route(caff, tri, cap=cap_c, windows=False)
            xgs.append(_gather(cpos, hc2, cap=cap_c, group=4))
            caps.append(cap_c)

        fcs, qg, kg, vg, qw, kw, vw = _inproj(x, lat[0], lat[1], n1, w_mix, cos_t, sin_t, head_avg, cdft, qn, kn,
                                              rope=True, tm=512, chunks=4)
        f_mix = _fourier(fcs, cn_t, sn_t, tr=512)
        o_g = _global_attn(qg, kg, vg, ckg, cvg, tq=256)
        o_w = _window_attn(sink[l], qw, kw, vw, ckw, cvw)
        x1, h2, aff = _merge(x, f_mix, o_g, o_w, *lat[:5], *merge_w, tm=512, chunks=2)
        pos, rt, win, win_ok = _route(aff, tri, cap=cap_t, windows=True)
        xgs.insert(0, _gather(pos, h2, cap=cap_t, group=4))
        caps.insert(0, cap_t)

        ys = _experts(xgs, caps, l, w_gate_e, w_up_e, w_down_e, nb=4)
        x = _scatter(x1, rt, ys[0], lat[5], fg, cap=cap_t, final=final, tn=512, windows=(win, win_ok))
        if need_ctx:
            xc = _scatter(xc1, crt, ys[1], cmod[5], fg, cap=cap_c, final=False, tn=l_ctx)
    return x
```

```python
import functools

import numpy as np
import jax
import jax.numpy as jnp
from jax import lax
from jax.experimental import pallas as pl
from jax.experimental.pallas import tpu as pltpu

F32 = jnp.float32
BF16 = jnp.bfloat16

D_MODEL = 1024
HEAD_DIM = 64
GRID_W = 64
FOURIER_GROUPS = 4
FOURIER_GROUP_CH = 64
FOURIER_W = FOURIER_GROUPS * FOURIER_GROUP_CH
G_HEADS, G_KV = 8, 2
G_GROUP = G_HEADS // G_KV
W_HEADS, W_KV = 4, 2
W_GROUP = W_HEADS // W_KV
WINDOW = 128
Q_BLOCK = 128
N_BRANCH = 3
GQ_W = G_HEADS * HEAD_DIM
GKV_W = G_KV * HEAD_DIM
WQ_W = W_HEADS * HEAD_DIM
WKV_W = W_KV * HEAD_DIM
OFF_F = 0
OFF_GQ = OFF_F + FOURIER_W
OFF_GK = OFF_GQ + GQ_W
OFF_GV = OFF_GK + GKV_W
OFF_WQ = OFF_GV + GKV_W
OFF_WK = OFF_WQ + WQ_W
OFF_WV = OFF_WK + WKV_W
MIX_W = OFF_WV + WKV_W
GATE_W = N_BRANCH * D_MODEL
ROPE_THETA = 10000.0
N_EXPERTS = 16
CAPACITY_FACTOR = 2
EPS = 1e-6
NEG_INF = -1e30
LANES = 128
HEAD_PAD = LANES // HEAD_DIM
MXU_DIM = 256
MOD_ROWS = 16
SCATTER_TILE = 256
SCATTER_WINDOW = 64
ROUTE_ROWS = 128
WINDOW_TILE = 4
VMEM_LIMIT = 56 * 1024 * 1024


def _cparams(*sem):
    return pltpu.CompilerParams(dimension_semantics=sem, vmem_limit_bytes=VMEM_LIMIT)


def _norm_mod(x, g, sh, sc):
    ms = jnp.mean(x * x, axis=-1, keepdims=True)
    return (x * lax.rsqrt(ms + EPS) * g) * (1.0 + sc) + sh


def _dot(a, b):
    return jnp.dot(a, b, preferred_element_type=F32)


def _dot_nt(a, b):
    return lax.dot_general(a, b, (((1,), (1,)), ((), ())), preferred_element_type=F32)


def _ada_kernel(c_ref, w_ref, b_ref, o_ref):
    c = c_ref[...]
    s = (c * jax.nn.sigmoid(c)).astype(BF16)
    o_ref[0] = _dot(s, w_ref[0].astype(BF16)) + b_ref[0]


def _ada(cvec, w_ada, b_ada):
    depth, d, n = w_ada.shape
    tn = 1536
    return pl.pallas_call(
        _ada_kernel,
        grid=(depth, n // tn),
        in_specs=[pl.BlockSpec((MOD_ROWS, d), lambda l, j: (0, 0)),
                  pl.BlockSpec((1, d, tn), lambda l, j: (l, 0, j)),
                  pl.BlockSpec((1, 1, tn), lambda l, j: (l, 0, j))],
        out_specs=pl.BlockSpec((1, MOD_ROWS, tn), lambda l, j: (l, 0, j)),
        out_shape=jax.ShapeDtypeStruct((depth, MOD_ROWS, n), F32),
        compiler_params=_cparams("arbitrary", "arbitrary"),
        name="ada",
    )(cvec, w_ada, b_ada.reshape(depth, 1, n))


def _rope(x, cos, sin_signed, even):
    outs = []
    for j in range(x.shape[1] // LANES):
        xb = x[:, j * LANES:(j + 1) * LANES]
        swap = jnp.where(even, pltpu.roll(xb, LANES - 1, 1), pltpu.roll(xb, 1, 1))
        outs.append(xb * cos + swap * sin_signed)
    return outs[0] if len(outs) == 1 else jnp.concatenate(outs, axis=1)


def _head_mean_square(z, bd):
    zz = (z * z).astype(BF16)
    blk = bd.shape[0]
    if z.shape[1] < blk:
        return _dot(zz, bd[:z.shape[1], :z.shape[1]])
    outs = [_dot(zz[:, j * blk:(j + 1) * blk], bd) for j in range(z.shape[1] // blk)]
    return outs[0] if len(outs) == 1 else jnp.concatenate(outs, axis=1)


def _pad_heads_f32(x, fill):
    blk = jnp.full((x.shape[0], LANES - HEAD_DIM), fill, F32)
    parts = []
    for h in range(x.shape[1] // HEAD_DIM):
        parts += [x[:, h * HEAD_DIM:(h + 1) * HEAD_DIM], blk]
    return jnp.concatenate(parts, axis=1)


def _pad_heads(x, fill):
    return _pad_heads_f32(x, fill).astype(BF16)


def _inproj_kernel(x_ref, sh_ref, sc_ref, g_ref, w_ref, cos_ref, sin_ref, bd_ref, cdft_ref, qn_ref, kn_ref,
                   fcs_ref, qg_ref, kg_ref, vg_ref, qw_ref, kw_ref, vw_ref, *, rope, chunks):
    tm = x_ref.shape[1]
    rows = [slice(c * (tm // chunks), (c + 1) * (tm // chunks)) for c in range(chunks)]
    zs = [_dot(_norm_mod(x_ref[0, r], g_ref[...], sh_ref[0], sc_ref[0]).astype(BF16), w_ref[...]) for r in rows]
    bd = bd_ref[...]
    scale = HEAD_DIM ** -0.5
    for r, z in zip(rows, zs):
        fcs_ref[0, r] = _dot(z[:, OFF_F:OFF_F + FOURIER_W].astype(BF16), cdft_ref[...].astype(BF16)).astype(BF16)
        q = z[:, OFF_GQ:OFF_GQ + GQ_W]
        k = z[:, OFF_GK:OFF_GK + GKV_W]
        q = q * lax.rsqrt(_head_mean_square(q, bd) + EPS) * qn_ref[...]
        k = k * lax.rsqrt(_head_mean_square(k, bd) + EPS) * kn_ref[...]
        qw = z[:, OFF_WQ:OFF_WQ + WQ_W]
        kw = z[:, OFF_WK:OFF_WK + WKV_W]
        if rope:
            cos, sin = cos_ref[r], sin_ref[r]
            even = (lax.broadcasted_iota(jnp.int32, cos.shape, 1) % 2) == 0
            q, k = _rope(q, cos, sin, even), _rope(k, cos, sin, even)
            qw, kw = _rope(qw, cos, sin, even), _rope(kw, cos, sin, even)
        qg_ref[0, r] = _pad_heads(q * scale, 0.0)
        kg_ref[0, r] = _pad_heads(k, 0.0)
        vg_ref[0, r] = _pad_heads(z[:, OFF_GV:OFF_GV + GKV_W], 1.0)
        qw_ref[0, r] = _pad_heads(qw * scale, 0.0)
        kw_ref[0, r] = _pad_heads(kw, 0.0)
        vw_ref[0, r] = _pad_heads(z[:, OFF_WV:OFF_WV + WKV_W], 1.0)


def _inproj(x, sh, sc, g, w_mix, cos_t, sin_t, bd, cdft, qn, kn, *, rope, tm, chunks):
    b, t, d = x.shape
    tok = lambda w: pl.BlockSpec((1, tm, w), lambda i, j: (i, j, 0))
    row = pl.BlockSpec((1, 1, d), lambda i, j: (i, 0, 0))
    const = lambda shape: pl.BlockSpec(shape, lambda i, j: (0,) * len(shape))
    widths = (2 * FOURIER_W,) + tuple(HEAD_PAD * w for w in (GQ_W, GKV_W, GKV_W, WQ_W, WKV_W, WKV_W))
    out_specs = [tok(w) for w in widths]
    out_shape = [jax.ShapeDtypeStruct((b, t, w), BF16) for w in widths]
    return pl.pallas_call(
        functools.partial(_inproj_kernel, rope=rope, chunks=chunks),
        grid=(b, t // tm),
        in_specs=[tok(d), row, row, const((1, d)), const((d, MIX_W)),
                  pl.BlockSpec((tm, LANES), lambda i, j: (j, 0)),
                  pl.BlockSpec((tm, LANES), lambda i, j: (j, 0)),
                  const((MXU_DIM, MXU_DIM)), const((FOURIER_W, 2 * FOURIER_W)),
                  const((1, GQ_W)), const((1, GKV_W))],
        out_specs=out_specs,
        out_shape=out_shape,
        compiler_params=_cparams("arbitrary", "arbitrary"),
        name="inproj_rope" if rope else "inproj_ctx",
    )(x, sh, sc, g, w_mix, cos_t, sin_t, bd, cdft, qn, kn)


def _fourier_kernel(cn_ref, sn_ref, fcs_ref, o_ref, cn_s, sn_s, *, scale):
    @pl.when(pl.program_id(1) == 0)
    def _():
        cn_s[...] = cn_ref[...].astype(BF16)
        sn_s[...] = sn_ref[...].astype(BF16)

    fcs = fcs_ref[0]
    re = _dot(cn_s[...], fcs[:, :FOURIER_W]) - _dot(sn_s[...], fcs[:, FOURIER_W:])
    o_ref[0] = (re * scale).astype(BF16)


def _fourier(fcs, cn, sn, *, tr):
    b, n, _ = fcs.shape
    scale = float((n * FOURIER_GROUP_CH) ** -0.5)
    return pl.pallas_call(
        functools.partial(_fourier_kernel, scale=scale),
        grid=(n // tr, b),
        in_specs=[pl.BlockSpec((tr, n), lambda r, i: (r, 0)),
                  pl.BlockSpec((tr, n), lambda r, i: (r, 0)),
                  pl.BlockSpec((1, n, 2 * FOURIER_W), lambda r, i: (i, 0, 0))],
        out_specs=pl.BlockSpec((1, tr, FOURIER_W), lambda r, i: (i, r, 0)),
        out_shape=jax.ShapeDtypeStruct((b, n, FOURIER_W), BF16),
        scratch_shapes=[pltpu.VMEM((tr, n), BF16), pltpu.VMEM((tr, n), BF16)],
        compiler_params=_cparams("arbitrary", "arbitrary"),
        name="fourier",
    )(cn, sn, fcs)


def _head(x, h):
    return x[:, h * LANES:(h + 1) * LANES]


def _stack_heads(q, first, count):
    return jnp.concatenate([_head(q, first + g) for g in range(count)], axis=0)


def _softmax_pv(parts, extra=None):
    m = functools.reduce(jnp.maximum, [jnp.max(s, axis=-1, keepdims=True) for s, _ in parts])
    if extra is not None:
        m = jnp.maximum(m, extra)
    acc = 0.0
    for s, v in parts:
        acc = acc + _dot(jnp.exp((s - m).astype(BF16)), v)
    den = acc[:, HEAD_DIM:HEAD_DIM + 1]
    if extra is not None:
        den = den + jnp.exp(extra - m)
    return acc[:, :HEAD_DIM] / den


def _unstack_heads(o_list, tq, count):
    cols = []
    for o in o_list:
        cols += [o[g * tq:(g + 1) * tq] for g in range(count)]
    return jnp.concatenate(cols, axis=1)


def _global_attn_kernel(q_ref, kl_ref, vl_ref, kc_ref, vc_ref, o_ref):
    q = q_ref[0]
    tq = q.shape[0]
    parts = []
    for kv in range(G_KV):
        qs = _stack_heads(q, kv * G_GROUP, G_GROUP)
        parts.append([(_dot_nt(qs, _head(kc_ref[0], kv)), _head(vc_ref[0], kv)),
                      (_dot_nt(qs, _head(kl_ref[0], kv)), _head(vl_ref[0], kv))])
    outs = [_softmax_pv(p) for p in parts]
    o_ref[0] = _unstack_heads(outs, tq, G_GROUP).astype(BF16)


def _global_attn(q, kl, vl, kc, vc, *, tq):
    b, t, _ = q.shape
    l = kc.shape[1]
    full = lambda n: pl.BlockSpec((1, n, HEAD_PAD * GKV_W), lambda i, j: (i, 0, 0))
    return pl.pallas_call(
        _global_attn_kernel,
        grid=(b, t // tq),
        in_specs=[pl.BlockSpec((1, tq, HEAD_PAD * GQ_W), lambda i, j: (i, j, 0)), full(t), full(t), full(l), full(l)],
        out_specs=pl.BlockSpec((1, tq, GQ_W), lambda i, j: (i, j, 0)),
        out_shape=jax.ShapeDtypeStruct((b, t, GQ_W), BF16),
        compiler_params=_cparams("arbitrary", "arbitrary"),
        name="global_attn",
    )(q, kl, vl, kc, vc)


def _sink_column(sink_ref, kv, rows_per_head):
    r = lax.broadcasted_iota(jnp.int32, (W_GROUP * rows_per_head, 1), 0)
    col = jnp.full(r.shape, sink_ref[kv * W_GROUP], F32)
    for g in range(1, W_GROUP):
        col = jnp.where(r >= g * rows_per_head, sink_ref[kv * W_GROUP + g], col)
    return col


def _window_attn_kernel(sink_ref, q_ref, kp_ref, k0_ref, kn_ref, vp_ref, v0_ref, vn_ref, kc_ref, vc_ref, o_ref,
                        *, n_tiles):
    j = pl.program_id(1)
    q = q_ref[0]
    kb = jnp.concatenate([kp_ref[0], k0_ref[0], kn_ref[0]], axis=0)
    vb = jnp.concatenate([vp_ref[0], v0_ref[0], vn_ref[0]], axis=0)
    rows = W_GROUP * Q_BLOCK
    qpos = lax.broadcasted_iota(jnp.int32, (rows, 3 * Q_BLOCK), 0) % Q_BLOCK
    kpos = lax.broadcasted_iota(jnp.int32, (rows, 3 * Q_BLOCK), 1)
    dist = jnp.abs(qpos - kpos + Q_BLOCK)
    in_window = dist <= WINDOW
    outside = jnp.full(dist.shape, WINDOW + 1, jnp.int32)
    first_ok = jnp.where((kpos < Q_BLOCK) & (j == 0), outside, dist) <= WINDOW
    last_ok = jnp.where((kpos >= 2 * Q_BLOCK) & (j == n_tiles - 1), outside, dist) <= WINDOW
    parts = []
    for i in range(WINDOW_TILE):
        valid = first_ok if i == 0 else (last_ok if i == WINDOW_TILE - 1 else in_window)
        q_i = q[i * Q_BLOCK:(i + 1) * Q_BLOCK]
        band = slice(i * Q_BLOCK, (i + 3) * Q_BLOCK)
        for kv in range(W_KV):
            qs = _stack_heads(q_i, kv * W_GROUP, W_GROUP)
            s_b = jnp.where(valid, _dot_nt(qs, _head(kb, kv)[band]), NEG_INF)
            parts.append([(_dot_nt(qs, _head(kc_ref[0], kv)), _head(vc_ref[0], kv)), (s_b, _head(vb, kv)[band])])
    sinks = [_sink_column(sink_ref, kv, Q_BLOCK) for kv in range(W_KV)]
    outs = [_softmax_pv(p, sinks[n % W_KV]) for n, p in enumerate(parts)]
    blocks = [_unstack_heads(outs[i * W_KV:(i + 1) * W_KV], Q_BLOCK, W_GROUP) for i in range(WINDOW_TILE)]
    o_ref[0] = jnp.concatenate(blocks, axis=0).astype(BF16)


def _window_attn(sink, q, k, v, kc, vc):
    b, t, _ = q.shape
    l = kc.shape[1]
    nb = t // Q_BLOCK
    tile = WINDOW_TILE * Q_BLOCK
    kvw = HEAD_PAD * WKV_W
    edge = lambda f: pl.BlockSpec((1, Q_BLOCK, kvw), lambda i, j: (i, f(j), 0))
    prev = lambda j: jnp.maximum(j * WINDOW_TILE - 1, 0)
    nxt = lambda j: jnp.minimum((j + 1) * WINDOW_TILE, nb - 1)
    mid = pl.BlockSpec((1, tile, kvw), lambda i, j: (i, j, 0))
    full = pl.BlockSpec((1, l, kvw), lambda i, j: (i, 0, 0))
    return pl.pallas_call(
        functools.partial(_window_attn_kernel, n_tiles=t // tile),
        grid=(b, t // tile),
        in_specs=[pl.BlockSpec(memory_space=pltpu.SMEM),
                  pl.BlockSpec((1, tile, HEAD_PAD * WQ_W), lambda i, j: (i, j, 0)),
                  edge(prev), mid, edge(nxt), edge(prev), mid, edge(nxt), full, full],
        out_specs=pl.BlockSpec((1, tile, WQ_W), lambda i, j: (i, j, 0)),
        out_shape=jax.ShapeDtypeStruct((b, t, WQ_W), BF16),
        compiler_params=_cparams("arbitrary", "arbitrary"),
        name="window_attn",
    )(sink, q, k, k, k, v, v, v, kc, vc)


def _ctx_attn_kernel(sink_ref, qg_ref, kg_ref, vg_ref, qw_ref, kw_ref, vw_ref, og_ref, ow_ref):
    l = qg_ref.shape[1]
    qg, qw = qg_ref[0], qw_ref[0]
    g_parts = [[(_dot_nt(_stack_heads(qg, kv * G_GROUP, G_GROUP), _head(kg_ref[0], kv)), _head(vg_ref[0], kv))]
               for kv in range(G_KV)]
    w_parts = [[(_dot_nt(_stack_heads(qw, kv * W_GROUP, W_GROUP), _head(kw_ref[0], kv)), _head(vw_ref[0], kv))]
               for kv in range(W_KV)]
    og_ref[0] = _unstack_heads([_softmax_pv(p) for p in g_parts], l, G_GROUP).astype(BF16)
    outs = [_softmax_pv(p, _sink_column(sink_ref, kv, l)) for kv, p in enumerate(w_parts)]
    ow_ref[0] = _unstack_heads(outs, l, W_GROUP).astype(BF16)


def _ctx_attn(sink, qg, kg, vg, qw, kw, vw):
    b, l, _ = qg.shape
    spec = lambda w: pl.BlockSpec((1, l, w), lambda i: (i, 0, 0))
    padded = lambda w: spec(HEAD_PAD * w)
    return pl.pallas_call(
        _ctx_attn_kernel,
        grid=(b,),
        in_specs=[pl.BlockSpec(memory_space=pltpu.SMEM),
                  padded(GQ_W), padded(GKV_W), padded(GKV_W), padded(WQ_W), padded(WKV_W), padded(WKV_W)],
        out_specs=[spec(GQ_W), spec(WQ_W)],
        out_shape=[jax.ShapeDtypeStruct((b, l, GQ_W), BF16), jax.ShapeDtypeStruct((b, l, WQ_W), BF16)],
        compiler_params=_cparams("arbitrary"),
        name="ctx_attn",
    )(sink, qg, kg, vg, qw, kw, vw)


def _merge_kernel(x_ref, f_ref, og_ref, ow_ref, sh1_ref, sc1_ref, g1_ref, sh2_ref, sc2_ref,
                  n1_ref, n2_ref, wgate_ref, wbf_ref, wbg_ref, wbw_ref, wout_ref, wr_ref,
                  x1_ref, h2_ref, aff_ref, *, chunks):
    tm = x_ref.shape[1]
    rows = [slice(c * (tm // chunks), (c + 1) * (tm // chunks)) for c in range(chunks)]
    d = D_MODEL
    heads = []
    for r in rows:
        h = _norm_mod(x_ref[0, r], n1_ref[...], sh1_ref[0], sc1_ref[0]).astype(BF16)
        heads.append((_dot(h, wgate_ref[...]), _dot(f_ref[0, r], wbf_ref[...]),
                      _dot(og_ref[0, r], wbg_ref[...]), _dot(ow_ref[0, r], wbw_ref[...])))
    for r, (gate_logits, bf, bg, bw) in zip(rows, heads):
        gate = jax.nn.sigmoid(gate_logits)
        m = gate[:, 0:d] * bf + gate[:, d:2 * d] * bg + gate[:, 2 * d:3 * d] * bw
        x1 = x_ref[0, r] + g1_ref[0] * _dot(m.astype(BF16), wout_ref[...])
        x1_ref[0, r] = x1
        h2 = _norm_mod(x1, n2_ref[...], sh2_ref[0], sc2_ref[0]).astype(BF16)
        h2_ref[0, r] = h2
        logits = _dot(h2, wr_ref[...]).T[:N_EXPERTS]
        e = jnp.exp(logits - jnp.max(logits, axis=0, keepdims=True))
        aff_ref[0, :, r] = e / jnp.sum(e, axis=0, keepdims=True)


def _merge(x, f, og, ow, sh1, sc1, g1, sh2, sc2, n1, n2, wgate, wbf, wbg, wbw, wout, wr, *, tm, chunks):
    b, t, d = x.shape
    tok = lambda w: pl.BlockSpec((1, tm, w), lambda i, j: (i, j, 0))
    row = pl.BlockSpec((1, 1, d), lambda i, j: (i, 0, 0))
    const = lambda shape: pl.BlockSpec(shape, lambda i, j: (0,) * len(shape))
    return pl.pallas_call(
        functools.partial(_merge_kernel, chunks=chunks),
        grid=(b, t // tm),
        in_specs=[tok(d), tok(FOURIER_W), tok(GQ_W), tok(WQ_W), row, row, row, row, row,
                  const((1, d)), const((1, d)), const((d, GATE_W)), const((FOURIER_W, d)),
                  const((GQ_W, d)), const((WQ_W, d)), const((d, d)), const((d, LANES))],
        out_specs=[tok(d), tok(d), pl.BlockSpec((1, N_EXPERTS, tm), lambda i, j: (i, 0, j))],
        out_shape=[jax.ShapeDtypeStruct((b, t, d), F32), jax.ShapeDtypeStruct((b, t, d), BF16),
                   jax.ShapeDtypeStruct((b, N_EXPERTS, t), F32)],
        compiler_params=_cparams("arbitrary", "arbitrary"),
        name="merge",
    )(x, f, og, ow, sh1, sc1, g1, sh2, sc2, n1, n2, wgate, wbf, wbg, wbw, wout, wr)


def _cumsum_lanes(m, tri):
    e, n = m.shape
    nch = n // LANES
    stacked = jnp.concatenate([m[:, j * LANES:(j + 1) * LANES] for j in range(nch)], axis=0).astype(BF16)
    w = _dot(stacked, tri)
    outs, off = [], jnp.zeros((e, 1), F32)
    for j in range(nch):
        wj = w[j * e:(j + 1) * e]
        outs.append(wj + off)
        off = off + wj[:, LANES - 1:LANES]
    return jnp.concatenate(outs, axis=1)


def _route_kernel(aff_ref, tri_ref, pos_ref, rt_ref, *win_refs, cap, n_exp):
    aff = aff_ref[...]
    e, n = aff.shape
    thr_bits = jnp.zeros((e, 1), jnp.int32)
    for bit in range(30, -1, -1):
        cand = thr_bits | (1 << bit)
        cnt = jnp.sum(jnp.where(aff >= pltpu.bitcast(cand, F32), 1.0, 0.0), axis=1, keepdims=True)
        thr_bits = jnp.where(cnt >= cap, cand, thr_bits)
    ge = jnp.where(aff >= pltpu.bitcast(thr_bits, F32), 1.0, 0.0)
    gt = jnp.where(aff >= pltpu.bitcast(thr_bits + 1, F32), 1.0, 0.0)
    eq = ge - gt
    room = cap - jnp.sum(gt, axis=1, keepdims=True)
    tri = tri_ref[...]
    sel = gt + jnp.where(_cumsum_lanes(eq, tri) <= room, eq, 0.0)
    cums = _cumsum_lanes(sel, tri)
    pos = jnp.where(sel > 0.0, cums - 1.0, -1.0)
    pos_ref[...] = pos.astype(jnp.int32)
    weight = sel * aff
    pad = jnp.zeros((ROUTE_ROWS - 2 * n_exp, n), F32)
    for i in range(e // n_exp):
        rows = slice(i * n_exp, (i + 1) * n_exp)
        rt_ref[i] = jnp.concatenate([pos[rows], weight[rows], pad], axis=0).T
    if win_refs:
        win_ref, ok_ref = win_refs
        n_tiles = n // SCATTER_TILE
        ends = jnp.concatenate([cums[:, (k + 1) * SCATTER_TILE - 1:(k + 1) * SCATTER_TILE] for k in range(n_tiles)],
                               axis=1)
        starts = jnp.concatenate([jnp.zeros((e, 1), F32), ends[:, :n_tiles - 1]], axis=1)
        align = 16.0
        first = jnp.minimum(jnp.floor(starts / align) * align, float(cap - SCATTER_WINDOW))
        fits = jnp.where(ends - first <= SCATTER_WINDOW, 1.0, 0.0)
        win_ref[...] = first.astype(jnp.int32)
        ok = [jnp.min(fits[i * n_exp:(i + 1) * n_exp], axis=0, keepdims=True) for i in range(e // n_exp)]
        ok_ref[...] = jnp.concatenate(ok, axis=0).astype(jnp.int32)


def _route(aff, tri, *, cap, windows):
    b, e, n = aff.shape
    whole = lambda shape: pl.BlockSpec(shape, lambda i: (0,) * len(shape))
    out_shape = [jax.ShapeDtypeStruct((b * e, n), jnp.int32), jax.ShapeDtypeStruct((b, n, ROUTE_ROWS), F32)]
    if windows:
        n_tiles = n // SCATTER_TILE
        out_shape += [jax.ShapeDtypeStruct((b * e, n_tiles), jnp.int32), jax.ShapeDtypeStruct((b, n_tiles), jnp.int32)]
    outs = pl.pallas_call(
        functools.partial(_route_kernel, cap=cap, n_exp=e),
        grid=(1,),
        in_specs=[whole((b * e, n)), whole((LANES, LANES))],
        out_specs=[whole(s.shape) for s in out_shape],
        out_shape=out_shape,
        compiler_params=_cparams("arbitrary"),
        name="route",
    )(aff.reshape(b * e, n), tri)
    pos, rt = outs[0].reshape(b, e, n), outs[1]
    if not windows:
        return pos, rt
    win = outs[2].reshape(b, e, n_tiles).transpose(0, 2, 1)
    return pos, rt, win, outs[3]


def _gather_kernel(pos_ref, h_ref, o_ref, *, cap, group):
    j = pl.program_id(1)
    n = h_ref.shape[1]
    slot = lax.broadcasted_iota(jnp.int32, (cap, n), 0)
    sel = [jnp.where(pos_ref[0, pl.ds(j * group + g, 1), :] == slot, 1.0, 0.0).astype(BF16) for g in range(group)]
    o_ref[0] = _dot(jnp.concatenate(sel, axis=0), h_ref[0]).astype(BF16)


def _gather(pos, h, *, cap, group):
    b, e, n = pos.shape
    d = h.shape[2]
    return pl.pallas_call(
        functools.partial(_gather_kernel, cap=cap, group=group),
        grid=(b, e // group),
        in_specs=[pl.BlockSpec((1, e, n), lambda i, j: (i, 0, 0)), pl.BlockSpec((1, n, d), lambda i, j: (i, 0, 0))],
        out_specs=pl.BlockSpec((1, group * cap, d), lambda i, j: (i, j, 0)),
        out_shape=jax.ShapeDtypeStruct((b, e * cap, d), BF16),
        compiler_params=_cparams("arbitrary", "arbitrary"),
        name="moe_gather",
    )(pos, h)


def _gather_win_kernel(win_ref, ok_ref, pos_ref, h_ref, o_ref, *, cap):
    i, kt = pl.program_id(0), pl.program_id(1)

    @pl.when(kt == 0)
    def _():
        o_ref[...] = jnp.zeros_like(o_ref)

    pos = pos_ref[0]
    tile = pos.shape[1]

    def windows():
        slot = lax.broadcasted_iota(jnp.int32, (SCATTER_WINDOW, tile), 0)
        first = [win_ref[i, kt, e] for e in range(N_EXPERTS)]
        pick = [jnp.where(pos[e:e + 1] - first[e] == slot, 1.0, 0.0).astype(BF16) for e in range(N_EXPERTS)]
        z = _dot(jnp.concatenate(pick, axis=0), h_ref[0])
        for e in range(N_EXPERTS):
            rows = pl.ds(pl.multiple_of(e * cap + first[e], 16), SCATTER_WINDOW)
            o_ref[0, rows, :] += z[e * SCATTER_WINDOW:(e + 1) * SCATTER_WINDOW].astype(BF16)

    def full():
        slot = lax.broadcasted_iota(jnp.int32, (cap, tile), 0)
        for e in range(N_EXPERTS):
            pick = jnp.where(pos[e:e + 1] == slot, 1.0, 0.0).astype(BF16)
            o_ref[0, e * cap:(e + 1) * cap, :] += _dot(pick, h_ref[0]).astype(BF16)

    lax.cond(ok_ref[i, kt] > 0, windows, full)


def _gather_win(win, ok, pos, h, *, cap):
    b, e, n = pos.shape
    d = h.shape[2]
    smem = pl.BlockSpec(memory_space=pltpu.SMEM)
    return pl.pallas_call(
        functools.partial(_gather_win_kernel, cap=cap),
        grid=(b, n // SCATTER_TILE),
        in_specs=[smem, smem,
                  pl.BlockSpec((1, e, SCATTER_TILE), lambda i, j: (i, 0, j)),
                  pl.BlockSpec((1, SCATTER_TILE, d), lambda i, j: (i, j, 0))],
        out_specs=pl.BlockSpec((1, e * cap, d), lambda i, j: (i, 0, 0)),
        out_shape=jax.ShapeDtypeStruct((b, e * cap, d), BF16),
        compiler_params=_cparams("arbitrary", "arbitrary"),
        name="moe_gather_win",
    )(win, ok, pos, h)


def _expert_kernel(*refs, n_sets):
    x_refs, (wg_ref, wu_ref, wd_ref) = refs[:n_sets], refs[n_sets:n_sets + 3]
    o_refs = refs[n_sets + 3:2 * n_sets + 3]
    d = x_refs[0].shape[2]
    rows = [r.shape[0] * r.shape[1] for r in x_refs]
    xs = [r[...].reshape(n, d) for r, n in zip(x_refs, rows)]
    x = xs[0] if n_sets == 1 else jnp.concatenate(xs, axis=0)
    a = _dot(x, wg_ref[0, 0].astype(BF16))
    u = _dot(x, wu_ref[0, 0].astype(BF16))
    y = _dot((a * jax.nn.sigmoid(a) * u).astype(BF16), wd_ref[0, 0].astype(BF16)).astype(BF16)
    start = 0
    for o_ref, n in zip(o_refs, rows):
        o_ref[...] = y[start:start + n].reshape(o_ref.shape)
        start += n


def _experts(xgs, caps, layer, wg, wu, wd, *, nb):
    b, _, d = xgs[0].shape
    _, e, _, f = wg.shape
    wspec = lambda r, c: pl.BlockSpec((1, 1, r, c), lambda i, j: (layer, i, 0, 0))
    xspecs = [pl.BlockSpec((nb, cap, d), lambda i, j: (j, i, 0)) for cap in caps]
    return pl.pallas_call(
        functools.partial(_expert_kernel, n_sets=len(xgs)),
        grid=(e, b // nb),
        in_specs=xspecs + [wspec(d, f), wspec(d, f), wspec(f, d)],
        out_specs=xspecs,
        out_shape=[jax.ShapeDtypeStruct(xg.shape, BF16) for xg in xgs],
        compiler_params=_cparams("arbitrary", "arbitrary"),
        name="moe_experts",
    )(*xgs, wg, wu, wd)


def _combine_matrix(rt, first, width):
    slot = lax.broadcasted_iota(jnp.int32, (rt.shape[0], width), 1).astype(F32)
    cols = [jnp.where(rt[:, e:e + 1] - first[e] == slot, rt[:, N_EXPERTS + e:N_EXPERTS + e + 1], 0.0).astype(BF16)
            for e in range(N_EXPERTS)]
    return jnp.concatenate(cols, axis=1)


def _scatter_kernel(*refs, cap, final, windowed):
    if windowed:
        win_ref, ok_ref, x_ref, rt_ref, y_ref, g2_ref, fg_ref, o_ref = refs
    else:
        x_ref, rt_ref, y_ref, g2_ref, fg_ref, o_ref = refs
    i, j = pl.program_id(0), pl.program_id(1)
    tn = x_ref.shape[1]

    def finish(rows, y):
        x = x_ref[0, rows] + g2_ref[0] * y
        if final:
            ms = jnp.mean(x * x, axis=-1, keepdims=True)
            x = x * lax.rsqrt(ms + EPS) * fg_ref[...]
        o_ref[0, rows] = x

    def full():
        rows = slice(0, tn)
        finish(rows, _dot(_combine_matrix(rt_ref[0], [0.0] * N_EXPERTS, cap), y_ref[0]))

    if not windowed:
        full()
        return
    subs = tn // SCATTER_TILE
    tiles = [j * subs + u for u in range(subs)]

    def windows():
        ys = []
        for u, tile in enumerate(tiles):
            rt = rt_ref[0, u * SCATTER_TILE:(u + 1) * SCATTER_TILE]
            first = [win_ref[i, tile, e] for e in range(N_EXPERTS)]
            picked = [y_ref[0, pl.ds(pl.multiple_of(e * cap + first[e], 16), SCATTER_WINDOW), :]
                      for e in range(N_EXPERTS)]
            a = _combine_matrix(rt, [f.astype(F32) for f in first], SCATTER_WINDOW)
            ys.append(_dot(a, jnp.concatenate(picked, axis=0)))
        for u, y in enumerate(ys):
            finish(slice(u * SCATTER_TILE, (u + 1) * SCATTER_TILE), y)

    all_fit = functools.reduce(jnp.minimum, [ok_ref[i, tile] for tile in tiles])
    lax.cond(all_fit > 0, windows, full)


def _scatter(x, rt, y, g2, fg, *, cap, final, tn, windows=None):
    b, t, d = x.shape
    smem = pl.BlockSpec(memory_space=pltpu.SMEM)
    windowed = windows is not None
    return pl.pallas_call(
        functools.partial(_scatter_kernel, cap=cap, final=final, windowed=windowed),
        grid=(b, t // tn),
        in_specs=([smem, smem] if windowed else []) + [
            pl.BlockSpec((1, tn, d), lambda i, j: (i, j, 0)),
            pl.BlockSpec((1, tn, ROUTE_ROWS), lambda i, j: (i, j, 0)),
            pl.BlockSpec((1, N_EXPERTS * cap, d), lambda i, j: (i, 0, 0)),
            pl.BlockSpec((1, 1, d), lambda i, j: (i, 0, 0)),
            pl.BlockSpec((1, d), lambda i, j: (0, 0))],
        out_specs=pl.BlockSpec((1, tn, d), lambda i, j: (i, j, 0)),
        out_shape=jax.ShapeDtypeStruct((b, t, d), F32),
        compiler_params=_cparams("arbitrary", "arbitrary"),
        name="moe_scatter",
    )(*(windows if windowed else ()), x, rt, y, g2, fg)


def _capacity(t):
    return CAPACITY_FACTOR * t // N_EXPERTS


def _dft_tables(n):
    k = np.arange(n, dtype=np.int64)
    ang = 2.0 * np.pi * ((k[:, None] * k[None, :]) % n).astype(np.float64) / n
    return np.cos(ang), np.sin(ang)


def _channel_dft():
    c, s = _dft_tables(FOURIER_GROUP_CH)
    eye = np.eye(FOURIER_GROUPS)
    return np.concatenate([np.kron(eye, c), np.kron(eye, s)], axis=1)


def _rope_tables(t):
    rows = t // GRID_W
    r, col = jnp.meshgrid(jnp.arange(rows), jnp.arange(GRID_W), indexing="ij")
    half = HEAD_DIM // 2
    inv = ROPE_THETA ** (-jnp.arange(0, half, 2, dtype=F32) / half)
    ang = jnp.concatenate([r.reshape(-1, 1).astype(F32) * inv, col.reshape(-1, 1).astype(F32) * inv], axis=-1)
    cos = jnp.repeat(jnp.cos(ang), 2, axis=1)
    sin = jnp.repeat(jnp.sin(ang), 2, axis=1) * jnp.tile(jnp.asarray([-1.0, 1.0], F32), half)
    return jnp.tile(cos, (1, HEAD_PAD)), jnp.tile(sin, (1, HEAD_PAD))


def kernel(x, c, ctx, c_ctx, w_ada, b_ada, norm1_g, w_in, q_norm_g, k_norm_g, sink, w_br_fourier, w_br_global,
           w_br_window, w_out, norm2_g, w_router, w_gate_e, w_up_e, w_down_e, final_g):
    b, t, d = x.shape
    l_ctx = ctx.shape[1]
    depth = w_ada.shape[0]
    cap_t, cap_c = _capacity(t), _capacity(l_ctx)

    cos_t, sin_t = _rope_tables(t)
    cos_c, sin_c = jnp.ones((l_ctx, LANES), F32), jnp.zeros((l_ctx, LANES), F32)
    head_avg = jnp.asarray(np.kron(np.eye(MXU_DIM // HEAD_DIM), np.full((HEAD_DIM, HEAD_DIM), 1.0 / HEAD_DIM)), BF16)
    cdft = jnp.asarray(_channel_dft(), F32)
    cn_t, sn_t = (jnp.asarray(a, F32) for a in _dft_tables(t))
    cn_c, sn_c = (jnp.asarray(a, F32) for a in _dft_tables(l_ctx))
    tri = jnp.asarray(np.triu(np.ones((LANES, LANES))), BF16)

    cvec = jnp.concatenate([c, c_ctx[None], jnp.zeros((MOD_ROWS - b - 1, d), F32)], axis=0)
    mods = _ada(cvec, w_ada, b_ada)

    xc = ctx
    for l in range(depth):
        need_ctx = l < depth - 1
        final = l == depth - 1
        lat = [mods[l, :b, i * d:(i + 1) * d].reshape(b, 1, d) for i in range(6)]
        cmod = [jnp.broadcast_to(mods[l, b, i * d:(i + 1) * d].reshape(1, 1, d), (b, 1, d)) for i in range(6)]
        w_mix = w_in[l, :, :MIX_W].astype(BF16)
        w_gate = w_in[l, :, MIX_W:].astype(BF16)
        n1, n2 = norm1_g[l].reshape(1, d), norm2_g[l].reshape(1, d)
        qn = jnp.tile(q_norm_g[l], G_HEADS).reshape(1, GQ_W)
        kn = jnp.tile(k_norm_g[l], G_KV).reshape(1, GKV_W)
        wbf, wbg, wbw = (w[l].astype(BF16) for w in (w_br_fourier, w_br_global, w_br_window))
        wout = w_out[l].astype(BF16)
        wr = jnp.pad(w_router[l], ((0, 0), (0, LANES - N_EXPERTS))).astype(BF16)
        fg = final_g.reshape(1, d)
        merge_w = (n1, n2, w_gate, wbf, wbg, wbw, wout, wr)

        cfcs, cqg, ckg, cvg, cqw, ckw, cvw = _inproj(xc, cmod[0], cmod[1], n1, w_mix, cos_c, sin_c, head_avg, cdft,
                                                     qn, kn, rope=False, tm=l_ctx, chunks=2)
        xgs, caps = [], []
        if need_ctx:
            cf_mix = _fourier(cfcs, cn_c, sn_c, tr=l_ctx)
            oc_g, oc_w = _ctx_attn(sink[l], cqg, ckg, cvg, cqw, ckw, cvw)
            xc1, hc2, caff = _merge(xc, cf_mix, oc_g, oc_w, *cmod[:5], *merge_w, tm=l_ctx, chunks=1)
            cpos, crt = _route(caff, tri, cap=cap_c, windows=False)
            xgs.append(_gather(cpos, hc2, cap=cap_c, group=4))
            caps.append(cap_c)

        fcs, qg, kg, vg, qw, kw, vw = _inproj(x, lat[0], lat[1], n1, w_mix, cos_t, sin_t, head_avg, cdft, qn, kn,
                                              rope=True, tm=512, chunks=4)
        f_mix = _fourier(fcs, cn_t, sn_t, tr=512)
        o_g = _global_attn(qg, kg, vg, ckg, cvg, tq=256)
        o_w = _window_attn(sink[l], qw, kw, vw, ckw, cvw)
        x1, h2, aff = _merge(x, f_mix, o_g, o_w, *lat[:5], *merge_w, tm=512, chunks=2)
        pos, rt, win, win_ok = _route(aff, tri, cap=cap_t, windows=True)
        xgs.insert(0, _gather_win(win, win_ok, pos, h2, cap=cap_t))
        caps.insert(0, cap_t)

        ys = _experts(xgs, caps, l, w_gate_e, w_up_e, w_down_e, nb=4)
        x = _scatter(x1, rt, ys[0], lat[5], fg, cap=cap_t, final=final, tn=512, windows=(win, win_ok))
        if need_ctx:
            xc = _scatter(xc1, crt, ys[1], cmod[5], fg, cap=cap_c, final=False, tn=l_ctx)
    return x
```

```python
import functools

import numpy as np
import jax
import jax.numpy as jnp
from jax import lax
from jax.experimental import pallas as pl
from jax.experimental.pallas import tpu as pltpu

F32 = jnp.float32
BF16 = jnp.bfloat16

D_MODEL = 1024
HEAD_DIM = 64
GRID_W = 64
FOURIER_GROUPS = 4
FOURIER_GROUP_CH = 64
FOURIER_W = FOURIER_GROUPS * FOURIER_GROUP_CH
G_HEADS, G_KV = 8, 2
G_GROUP = G_HEADS // G_KV
W_HEADS, W_KV = 4, 2
W_GROUP = W_HEADS // W_KV
WINDOW = 128
Q_BLOCK = 128
N_BRANCH = 3
GQ_W = G_HEADS * HEAD_DIM
GKV_W = G_KV * HEAD_DIM
WQ_W = W_HEADS * HEAD_DIM
WKV_W = W_KV * HEAD_DIM
OFF_F = 0
OFF_GQ = OFF_F + FOURIER_W
OFF_GK = OFF_GQ + GQ_W
OFF_GV = OFF_GK + GKV_W
OFF_WQ = OFF_GV + GKV_W
OFF_WK = OFF_WQ + WQ_W
OFF_WV = OFF_WK + WKV_W
MIX_W = OFF_WV + WKV_W
GATE_W = N_BRANCH * D_MODEL
ROPE_THETA = 10000.0
N_EXPERTS = 16
CAPACITY_FACTOR = 2
EPS = 1e-6
NEG_INF = -1e30
LANES = 128
HEAD_PAD = LANES // HEAD_DIM
MXU_DIM = 256
MOD_ROWS = 16
SCATTER_TILE = 256
SCATTER_WINDOW = 64
ROUTE_ROWS = 128
WINDOW_TILE = 4
GLOBAL_CHAINS_PER_KV = 1
VMEM_LIMIT = 56 * 1024 * 1024


def _cparams(*sem):
    return pltpu.CompilerParams(dimension_semantics=sem, vmem_limit_bytes=VMEM_LIMIT)


def _norm_mod(x, g, sh, sc):
    ms = jnp.mean(x * x, axis=-1, keepdims=True)
    return (x * lax.rsqrt(ms + EPS) * g) * (1.0 + sc) + sh


def _dot(a, b):
    return jnp.dot(a, b, preferred_element_type=F32)


def _dot_nt(a, b):
    return lax.dot_general(a, b, (((1,), (1,)), ((), ())), preferred_element_type=F32)


def _ada_kernel(c_ref, w_ref, b_ref, o_ref):
    c = c_ref[...]
    s = (c * jax.nn.sigmoid(c)).astype(BF16)
    o_ref[0] = _dot(s, w_ref[0].astype(BF16)) + b_ref[0]


def _ada(cvec, w_ada, b_ada):
    depth, d, n = w_ada.shape
    tn = 1536
    return pl.pallas_call(
        _ada_kernel,
        grid=(depth, n // tn),
        in_specs=[pl.BlockSpec((MOD_ROWS, d), lambda l, j: (0, 0)),
                  pl.BlockSpec((1, d, tn), lambda l, j: (l, 0, j)),
                  pl.BlockSpec((1, 1, tn), lambda l, j: (l, 0, j))],
        out_specs=pl.BlockSpec((1, MOD_ROWS, tn), lambda l, j: (l, 0, j)),
        out_shape=jax.ShapeDtypeStruct((depth, MOD_ROWS, n), F32),
        compiler_params=_cparams("arbitrary", "arbitrary"),
        name="ada",
    )(cvec, w_ada, b_ada.reshape(depth, 1, n))


def _rope(x, cos, sin_signed, even):
    outs = []
    for j in range(x.shape[1] // LANES):
        xb = x[:, j * LANES:(j + 1) * LANES]
        swap = jnp.where(even, pltpu.roll(xb, LANES - 1, 1), pltpu.roll(xb, 1, 1))
        outs.append(xb * cos + swap * sin_signed)
    return outs[0] if len(outs) == 1 else jnp.concatenate(outs, axis=1)


def _head_mean_square(z, bd):
    zz = (z * z).astype(BF16)
    blk = bd.shape[0]
    if z.shape[1] < blk:
        return _dot(zz, bd[:z.shape[1], :z.shape[1]])
    outs = [_dot(zz[:, j * blk:(j + 1) * blk], bd) for j in range(z.shape[1] // blk)]
    return outs[0] if len(outs) == 1 else jnp.concatenate(outs, axis=1)


def _pad_heads_f32(x, fill):
    blk = jnp.full((x.shape[0], LANES - HEAD_DIM), fill, F32)
    parts = []
    for h in range(x.shape[1] // HEAD_DIM):
        parts += [x[:, h * HEAD_DIM:(h + 1) * HEAD_DIM], blk]
    return jnp.concatenate(parts, axis=1)


def _pad_heads(x, fill):
    return _pad_heads_f32(x, fill).astype(BF16)


def _inproj_kernel(x_ref, sh_ref, sc_ref, g_ref, w_ref, cos_ref, sin_ref, bd_ref, cdft_ref, qn_ref, kn_ref,
                   fcs_ref, qg_ref, kg_ref, vg_ref, qw_ref, kw_ref, vw_ref, *, rope, chunks):
    tm = x_ref.shape[1]
    rows = [slice(c * (tm // chunks), (c + 1) * (tm // chunks)) for c in range(chunks)]
    zs = [_dot(_norm_mod(x_ref[0, r], g_ref[...], sh_ref[0], sc_ref[0]).astype(BF16), w_ref[...]) for r in rows]
    bd = bd_ref[...]
    scale = HEAD_DIM ** -0.5
    for r, z in zip(rows, zs):
        fcs_ref[0, r] = _dot(z[:, OFF_F:OFF_F + FOURIER_W].astype(BF16), cdft_ref[...].astype(BF16)).astype(BF16)
        q = z[:, OFF_GQ:OFF_GQ + GQ_W]
        k = z[:, OFF_GK:OFF_GK + GKV_W]
        q = q * lax.rsqrt(_head_mean_square(q, bd) + EPS) * qn_ref[...]
        k = k * lax.rsqrt(_head_mean_square(k, bd) + EPS) * kn_ref[...]
        qw = z[:, OFF_WQ:OFF_WQ + WQ_W]
        kw = z[:, OFF_WK:OFF_WK + WKV_W]
        if rope:
            cos, sin = cos_ref[r], sin_ref[r]
            even = (lax.broadcasted_iota(jnp.int32, cos.shape, 1) % 2) == 0
            q, k = _rope(q, cos, sin, even), _rope(k, cos, sin, even)
            qw, kw = _rope(qw, cos, sin, even), _rope(kw, cos, sin, even)
        qg_ref[0, r] = _pad_heads(q * scale, 0.0)
        kg_ref[0, r] = _pad_heads(k, 0.0)
        vg_ref[0, r] = _pad_heads(z[:, OFF_GV:OFF_GV + GKV_W], 1.0)
        qw_ref[0, r] = _pad_heads(qw * scale, 0.0)
        kw_ref[0, r] = _pad_heads(kw, 0.0)
        vw_ref[0, r] = _pad_heads(z[:, OFF_WV:OFF_WV + WKV_W], 1.0)


def _inproj(x, sh, sc, g, w_mix, cos_t, sin_t, bd, cdft, qn, kn, *, rope, tm, chunks):
    b, t, d = x.shape
    tok = lambda w: pl.BlockSpec((1, tm, w), lambda i, j: (i, j, 0))
    row = pl.BlockSpec((1, 1, d), lambda i, j: (i, 0, 0))
    const = lambda shape: pl.BlockSpec(shape, lambda i, j: (0,) * len(shape))
    widths = (2 * FOURIER_W,) + tuple(HEAD_PAD * w for w in (GQ_W, GKV_W, GKV_W, WQ_W, WKV_W, WKV_W))
    out_specs = [tok(w) for w in widths]
    out_shape = [jax.ShapeDtypeStruct((b, t, w), BF16) for w in widths]
    return pl.pallas_call(
        functools.partial(_inproj_kernel, rope=rope, chunks=chunks),
        grid=(b, t // tm),
        in_specs=[tok(d), row, row, const((1, d)), const((d, MIX_W)),
                  pl.BlockSpec((tm, LANES), lambda i, j: (j, 0)),
                  pl.BlockSpec((tm, LANES), lambda i, j: (j, 0)),
                  const((MXU_DIM, MXU_DIM)), const((FOURIER_W, 2 * FOURIER_W)),
                  const((1, GQ_W)), const((1, GKV_W))],
        out_specs=out_specs,
        out_shape=out_shape,
        compiler_params=_cparams("arbitrary", "arbitrary"),
        name="inproj_rope" if rope else "inproj_ctx",
    )(x, sh, sc, g, w_mix, cos_t, sin_t, bd, cdft, qn, kn)


def _fourier_kernel(cn_ref, sn_ref, fcs_ref, o_ref, cn_s, sn_s, *, scale):
    @pl.when(pl.program_id(1) == 0)
    def _():
        cn_s[...] = cn_ref[...].astype(BF16)
        sn_s[...] = sn_ref[...].astype(BF16)

    fcs = fcs_ref[0]
    re = _dot(cn_s[...], fcs[:, :FOURIER_W]) - _dot(sn_s[...], fcs[:, FOURIER_W:])
    o_ref[0] = (re * scale).astype(BF16)


def _fourier(fcs, cn, sn, *, tr):
    b, n, _ = fcs.shape
    scale = float((n * FOURIER_GROUP_CH) ** -0.5)
    return pl.pallas_call(
        functools.partial(_fourier_kernel, scale=scale),
        grid=(n // tr, b),
        in_specs=[pl.BlockSpec((tr, n), lambda r, i: (r, 0)),
                  pl.BlockSpec((tr, n), lambda r, i: (r, 0)),
                  pl.BlockSpec((1, n, 2 * FOURIER_W), lambda r, i: (i, 0, 0))],
        out_specs=pl.BlockSpec((1, tr, FOURIER_W), lambda r, i: (i, r, 0)),
        out_shape=jax.ShapeDtypeStruct((b, n, FOURIER_W), BF16),
        scratch_shapes=[pltpu.VMEM((tr, n), BF16), pltpu.VMEM((tr, n), BF16)],
        compiler_params=_cparams("arbitrary", "arbitrary"),
        name="fourier",
    )(cn, sn, fcs)


def _head(x, h):
    return x[:, h * LANES:(h + 1) * LANES]


def _stack_heads(q, first, count):
    return jnp.concatenate([_head(q, first + g) for g in range(count)], axis=0)


def _softmax_pv(parts, extra=None):
    m = functools.reduce(jnp.maximum, [jnp.max(s, axis=-1, keepdims=True) for s, _ in parts])
    if extra is not None:
        m = jnp.maximum(m, extra)
    acc = 0.0
    for s, v in parts:
        acc = acc + _dot(jnp.exp((s - m).astype(BF16)), v)
    den = acc[:, HEAD_DIM:HEAD_DIM + 1]
    if extra is not None:
        den = den + jnp.exp(extra - m)
    return acc[:, :HEAD_DIM] / den


def _unstack_heads(o_list, tq, count):
    cols = []
    for o in o_list:
        cols += [o[g * tq:(g + 1) * tq] for g in range(count)]
    return jnp.concatenate(cols, axis=1)


def _global_attn_kernel(q_ref, kl_ref, vl_ref, kc_ref, vc_ref, o_ref):
    q = q_ref[0]
    tq = q.shape[0]
    per_chain = G_GROUP // GLOBAL_CHAINS_PER_KV
    parts = []
    for kv in range(G_KV):
        for c in range(GLOBAL_CHAINS_PER_KV):
            qs = _stack_heads(q, kv * G_GROUP + c * per_chain, per_chain)
            parts.append([(_dot_nt(qs, _head(kc_ref[0], kv)), _head(vc_ref[0], kv)),
                          (_dot_nt(qs, _head(kl_ref[0], kv)), _head(vl_ref[0], kv))])
    outs = [_softmax_pv(p) for p in parts]
    o_ref[0] = _unstack_heads(outs, tq, per_chain).astype(BF16)


def _global_attn(q, kl, vl, kc, vc, *, tq):
    b, t, _ = q.shape
    l = kc.shape[1]
    full = lambda n: pl.BlockSpec((1, n, HEAD_PAD * GKV_W), lambda i, j: (i, 0, 0))
    return pl.pallas_call(
        _global_attn_kernel,
        grid=(b, t // tq),
        in_specs=[pl.BlockSpec((1, tq, HEAD_PAD * GQ_W), lambda i, j: (i, j, 0)), full(t), full(t), full(l), full(l)],
        out_specs=pl.BlockSpec((1, tq, GQ_W), lambda i, j: (i, j, 0)),
        out_shape=jax.ShapeDtypeStruct((b, t, GQ_W), BF16),
        compiler_params=_cparams("arbitrary", "arbitrary"),
        name="global_attn",
    )(q, kl, vl, kc, vc)


def _sink_column(sink_ref, kv, rows_per_head):
    r = lax.broadcasted_iota(jnp.int32, (W_GROUP * rows_per_head, 1), 0)
    col = jnp.full(r.shape, sink_ref[kv * W_GROUP], F32)
    for g in range(1, W_GROUP):
        col = jnp.where(r >= g * rows_per_head, sink_ref[kv * W_GROUP + g], col)
    return col


def _window_attn_kernel(sink_ref, q_ref, kp_ref, k0_ref, kn_ref, vp_ref, v0_ref, vn_ref, kc_ref, vc_ref, o_ref,
                        *, n_tiles):
    j = pl.program_id(1)
    q = q_ref[0]
    kb = jnp.concatenate([kp_ref[0], k0_ref[0], kn_ref[0]], axis=0)
    vb = jnp.concatenate([vp_ref[0], v0_ref[0], vn_ref[0]], axis=0)
    rows = W_GROUP * Q_BLOCK
    qpos = lax.broadcasted_iota(jnp.int32, (rows, 3 * Q_BLOCK), 0) % Q_BLOCK
    kpos = lax.broadcasted_iota(jnp.int32, (rows, 3 * Q_BLOCK), 1)
    dist = jnp.abs(qpos - kpos + Q_BLOCK)
    in_window = dist <= WINDOW
    outside = jnp.full(dist.shape, WINDOW + 1, jnp.int32)
    first_ok = jnp.where((kpos < Q_BLOCK) & (j == 0), outside, dist) <= WINDOW
    last_ok = jnp.where((kpos >= 2 * Q_BLOCK) & (j == n_tiles - 1), outside, dist) <= WINDOW
    parts = []
    for i in range(WINDOW_TILE):
        valid = first_ok if i == 0 else (last_ok if i == WINDOW_TILE - 1 else in_window)
        q_i = q[i * Q_BLOCK:(i + 1) * Q_BLOCK]
        band = slice(i * Q_BLOCK, (i + 3) * Q_BLOCK)
        for kv in range(W_KV):
            qs = _stack_heads(q_i, kv * W_GROUP, W_GROUP)
            s_b = jnp.where(valid, _dot_nt(qs, _head(kb, kv)[band]), NEG_INF)
            parts.append([(_dot_nt(qs, _head(kc_ref[0], kv)), _head(vc_ref[0], kv)), (s_b, _head(vb, kv)[band])])
    sinks = [_sink_column(sink_ref, kv, Q_BLOCK) for kv in range(W_KV)]
    outs = [_softmax_pv(p, sinks[n % W_KV]) for n, p in enumerate(parts)]
    blocks = [_unstack_heads(outs[i * W_KV:(i + 1) * W_KV], Q_BLOCK, W_GROUP) for i in range(WINDOW_TILE)]
    o_ref[0] = jnp.concatenate(blocks, axis=0).astype(BF16)


def _window_attn(sink, q, k, v, kc, vc):
    b, t, _ = q.shape
    l = kc.shape[1]
    nb = t // Q_BLOCK
    tile = WINDOW_TILE * Q_BLOCK
    kvw = HEAD_PAD * WKV_W
    edge = lambda f: pl.BlockSpec((1, Q_BLOCK, kvw), lambda i, j: (i, f(j), 0))
    prev = lambda j: jnp.maximum(j * WINDOW_TILE - 1, 0)
    nxt = lambda j: jnp.minimum((j + 1) * WINDOW_TILE, nb - 1)
    mid = pl.BlockSpec((1, tile, kvw), lambda i, j: (i, j, 0))
    full = pl.BlockSpec((1, l, kvw), lambda i, j: (i, 0, 0))
    return pl.pallas_call(
        functools.partial(_window_attn_kernel, n_tiles=t // tile),
        grid=(b, t // tile),
        in_specs=[pl.BlockSpec(memory_space=pltpu.SMEM),
                  pl.BlockSpec((1, tile, HEAD_PAD * WQ_W), lambda i, j: (i, j, 0)),
                  edge(prev), mid, edge(nxt), edge(prev), mid, edge(nxt), full, full],
        out_specs=pl.BlockSpec((1, tile, WQ_W), lambda i, j: (i, j, 0)),
        out_shape=jax.ShapeDtypeStruct((b, t, WQ_W), BF16),
        compiler_params=_cparams("arbitrary", "arbitrary"),
        name="window_attn",
    )(sink, q, k, k, k, v, v, v, kc, vc)


def _ctx_attn_kernel(sink_ref, qg_ref, kg_ref, vg_ref, qw_ref, kw_ref, vw_ref, og_ref, ow_ref):
    l = qg_ref.shape[1]
    qg, qw = qg_ref[0], qw_ref[0]
    g_parts = [[(_dot_nt(_stack_heads(qg, kv * G_GROUP, G_GROUP), _head(kg_ref[0], kv)), _head(vg_ref[0], kv))]
               for kv in range(G_KV)]
    w_parts = [[(_dot_nt(_stack_heads(qw, kv * W_GROUP, W_GROUP), _head(kw_ref[0], kv)), _head(vw_ref[0], kv))]
               for kv in range(W_KV)]
    og_ref[0] = _unstack_heads([_softmax_pv(p) for p in g_parts], l, G_GROUP).astype(BF16)
    outs = [_softmax_pv(p, _sink_column(sink_ref, kv, l)) for kv, p in enumerate(w_parts)]
    ow_ref[0] = _unstack_heads(outs, l, W_GROUP).astype(BF16)


def _ctx_attn(sink, qg, kg, vg, qw, kw, vw):
    b, l, _ = qg.shape
    spec = lambda w: pl.BlockSpec((1, l, w), lambda i: (i, 0, 0))
    padded = lambda w: spec(HEAD_PAD * w)
    return pl.pallas_call(
        _ctx_attn_kernel,
        grid=(b,),
        in_specs=[pl.BlockSpec(memory_space=pltpu.SMEM),
                  padded(GQ_W), padded(GKV_W), padded(GKV_W), padded(WQ_W), padded(WKV_W), padded(WKV_W)],
        out_specs=[spec(GQ_W), spec(WQ_W)],
        out_shape=[jax.ShapeDtypeStruct((b, l, GQ_W), BF16), jax.ShapeDtypeStruct((b, l, WQ_W), BF16)],
        compiler_params=_cparams("arbitrary"),
        name="ctx_attn",
    )(sink, qg, kg, vg, qw, kw, vw)


def _merge_kernel(x_ref, f_ref, og_ref, ow_ref, sh1_ref, sc1_ref, g1_ref, sh2_ref, sc2_ref,
                  n1_ref, n2_ref, wgate_ref, wbf_ref, wbg_ref, wbw_ref, wout_ref, wr_ref,
                  x1_ref, h2_ref, aff_ref, *, chunks):
    tm = x_ref.shape[1]
    rows = [slice(c * (tm // chunks), (c + 1) * (tm // chunks)) for c in range(chunks)]
    d = D_MODEL
    heads = []
    for r in rows:
        h = _norm_mod(x_ref[0, r], n1_ref[...], sh1_ref[0], sc1_ref[0]).astype(BF16)
        heads.append((_dot(h, wgate_ref[...]), _dot(f_ref[0, r], wbf_ref[...]),
                      _dot(og_ref[0, r], wbg_ref[...]), _dot(ow_ref[0, r], wbw_ref[...])))
    for r, (gate_logits, bf, bg, bw) in zip(rows, heads):
        gate = jax.nn.sigmoid(gate_logits)
        m = gate[:, 0:d] * bf + gate[:, d:2 * d] * bg + gate[:, 2 * d:3 * d] * bw
        x1 = x_ref[0, r] + g1_ref[0] * _dot(m.astype(BF16), wout_ref[...])
        x1_ref[0, r] = x1
        h2 = _norm_mod(x1, n2_ref[...], sh2_ref[0], sc2_ref[0]).astype(BF16)
        h2_ref[0, r] = h2
        logits = _dot(h2, wr_ref[...]).T[:N_EXPERTS]
        e = jnp.exp(logits - jnp.max(logits, axis=0, keepdims=True))
        aff_ref[0, :, r] = e / jnp.sum(e, axis=0, keepdims=True)


def _merge(x, f, og, ow, sh1, sc1, g1, sh2, sc2, n1, n2, wgate, wbf, wbg, wbw, wout, wr, *, tm, chunks):
    b, t, d = x.shape
    tok = lambda w: pl.BlockSpec((1, tm, w), lambda i, j: (i, j, 0))
    row = pl.BlockSpec((1, 1, d), lambda i, j: (i, 0, 0))
    const = lambda shape: pl.BlockSpec(shape, lambda i, j: (0,) * len(shape))
    return pl.pallas_call(
        functools.partial(_merge_kernel, chunks=chunks),
        grid=(b, t // tm),
        in_specs=[tok(d), tok(FOURIER_W), tok(GQ_W), tok(WQ_W), row, row, row, row, row,
                  const((1, d)), const((1, d)), const((d, GATE_W)), const((FOURIER_W, d)),
                  const((GQ_W, d)), const((WQ_W, d)), const((d, d)), const((d, LANES))],
        out_specs=[tok(d), tok(d), pl.BlockSpec((1, N_EXPERTS, tm), lambda i, j: (i, 0, j))],
        out_shape=[jax.ShapeDtypeStruct((b, t, d), F32), jax.ShapeDtypeStruct((b, t, d), BF16),
                   jax.ShapeDtypeStruct((b, N_EXPERTS, t), F32)],
        compiler_params=_cparams("arbitrary", "arbitrary"),
        name="merge",
    )(x, f, og, ow, sh1, sc1, g1, sh2, sc2, n1, n2, wgate, wbf, wbg, wbw, wout, wr)


def _cumsum_lanes(m, tri):
    e, n = m.shape
    nch = n // LANES
    stacked = jnp.concatenate([m[:, j * LANES:(j + 1) * LANES] for j in range(nch)], axis=0).astype(BF16)
    w = _dot(stacked, tri)
    outs, off = [], jnp.zeros((e, 1), F32)
    for j in range(nch):
        wj = w[j * e:(j + 1) * e]
        outs.append(wj + off)
        off = off + wj[:, LANES - 1:LANES]
    return jnp.concatenate(outs, axis=1)


def _route_kernel(aff_ref, tri_ref, pos_ref, rt_ref, *win_refs, cap, n_exp):
    aff = aff_ref[...]
    e, n = aff.shape
    thr_bits = jnp.zeros((e, 1), jnp.int32)
    for bit in range(30, -1, -1):
        cand = thr_bits | (1 << bit)
        cnt = jnp.sum(jnp.where(aff >= pltpu.bitcast(cand, F32), 1.0, 0.0), axis=1, keepdims=True)
        thr_bits = jnp.where(cnt >= cap, cand, thr_bits)
    ge = jnp.where(aff >= pltpu.bitcast(thr_bits, F32), 1.0, 0.0)
    gt = jnp.where(aff >= pltpu.bitcast(thr_bits + 1, F32), 1.0, 0.0)
    eq = ge - gt
    room = cap - jnp.sum(gt, axis=1, keepdims=True)
    tri = tri_ref[...]
    sel = gt + jnp.where(_cumsum_lanes(eq, tri) <= room, eq, 0.0)
    cums = _cumsum_lanes(sel, tri)
    pos = jnp.where(sel > 0.0, cums - 1.0, -1.0)
    pos_ref[...] = pos.astype(jnp.int32)
    weight = sel * aff
    pad = jnp.zeros((ROUTE_ROWS - 2 * n_exp, n), F32)
    for i in range(e // n_exp):
        rows = slice(i * n_exp, (i + 1) * n_exp)
        rt_ref[i] = jnp.concatenate([pos[rows], weight[rows], pad], axis=0).T
    if win_refs:
        win_ref, ok_ref = win_refs
        n_tiles = n // SCATTER_TILE
        ends = jnp.concatenate([cums[:, (k + 1) * SCATTER_TILE - 1:(k + 1) * SCATTER_TILE] for k in range(n_tiles)],
                               axis=1)
        starts = jnp.concatenate([jnp.zeros((e, 1), F32), ends[:, :n_tiles - 1]], axis=1)
        align = 16.0
        first = jnp.minimum(jnp.floor(starts / align) * align, float(cap - SCATTER_WINDOW))
        fits = jnp.where(ends - first <= SCATTER_WINDOW, 1.0, 0.0)
        win_ref[...] = first.astype(jnp.int32)
        ok = [jnp.min(fits[i * n_exp:(i + 1) * n_exp], axis=0, keepdims=True) for i in range(e // n_exp)]
        ok_ref[...] = jnp.concatenate(ok, axis=0).astype(jnp.int32)


def _route(aff, tri, *, cap, windows):
    b, e, n = aff.shape
    whole = lambda shape: pl.BlockSpec(shape, lambda i: (0,) * len(shape))
    out_shape = [jax.ShapeDtypeStruct((b * e, n), jnp.int32), jax.ShapeDtypeStruct((b, n, ROUTE_ROWS), F32)]
    if windows:
        n_tiles = n // SCATTER_TILE
        out_shape += [jax.ShapeDtypeStruct((b * e, n_tiles), jnp.int32), jax.ShapeDtypeStruct((b, n_tiles), jnp.int32)]
    outs = pl.pallas_call(
        functools.partial(_route_kernel, cap=cap, n_exp=e),
        grid=(1,),
        in_specs=[whole((b * e, n)), whole((LANES, LANES))],
        out_specs=[whole(s.shape) for s in out_shape],
        out_shape=out_shape,
        compiler_params=_cparams("arbitrary"),
        name="route",
    )(aff.reshape(b * e, n), tri)
    pos, rt = outs[0].reshape(b, e, n), outs[1]
    if not windows:
        return pos, rt
    win = outs[2].reshape(b, e, n_tiles).transpose(0, 2, 1)
    return pos, rt, win, outs[3]


def _gather_kernel(pos_ref, h_ref, o_ref, *, cap, group):
    j = pl.program_id(1)
    n = h_ref.shape[1]
    slot = lax.broadcasted_iota(jnp.int32, (cap, n), 0)
    sel = [jnp.where(pos_ref[0, pl.ds(j * group + g, 1), :] == slot, 1.0, 0.0).astype(BF16) for g in range(group)]
    o_ref[0] = _dot(jnp.concatenate(sel, axis=0), h_ref[0]).astype(BF16)


def _gather(pos, h, *, cap, group):
    b, e, n = pos.shape
    d = h.shape[2]
    return pl.pallas_call(
        functools.partial(_gather_kernel, cap=cap, group=group),
        grid=(b, e // group),
        in_specs=[pl.BlockSpec((1, e, n), lambda i, j: (i, 0, 0)), pl.BlockSpec((1, n, d), lambda i, j: (i, 0, 0))],
        out_specs=pl.BlockSpec((1, group * cap, d), lambda i, j: (i, j, 0)),
        out_shape=jax.ShapeDtypeStruct((b, e * cap, d), BF16),
        compiler_params=_cparams("arbitrary", "arbitrary"),
        name="moe_gather",
    )(pos, h)


def _gather_win_kernel(win_ref, ok_ref, pos_ref, h_ref, o_ref, *, tiles):
    i, j = pl.program_id(0), pl.program_id(1)
    cap = o_ref.shape[2]

    @pl.when(j == 0)
    def _():
        o_ref[...] = jnp.zeros_like(o_ref)

    kts = [j * tiles + u for u in range(tiles)]
    cols = [slice(u * SCATTER_TILE, (u + 1) * SCATTER_TILE) for u in range(tiles)]

    def windows():
        slot = lax.broadcasted_iota(jnp.int32, (SCATTER_WINDOW, SCATTER_TILE), 0)
        firsts, zs = [], []
        for kt, c in zip(kts, cols):
            pos = pos_ref[0, :, c]
            first = [win_ref[i, kt, e] for e in range(N_EXPERTS)]
            pick = [jnp.where(pos[e:e + 1] - first[e] == slot, 1.0, 0.0).astype(BF16) for e in range(N_EXPERTS)]
            zs.append(_dot(jnp.concatenate(pick, axis=0), h_ref[0, c]))
            firsts.append(first)
        for first, z in zip(firsts, zs):
            for e in range(N_EXPERTS):
                rows = pl.ds(pl.multiple_of(first[e], 16), SCATTER_WINDOW)
                o_ref[0, e, rows, :] += z[e * SCATTER_WINDOW:(e + 1) * SCATTER_WINDOW].astype(BF16)

    def full():
        slot = lax.broadcasted_iota(jnp.int32, (cap, SCATTER_TILE), 0)
        for c in cols:
            for e in range(N_EXPERTS):
                pick = jnp.where(pos_ref[0, e:e + 1, c] == slot, 1.0, 0.0).astype(BF16)
                o_ref[0, e] += _dot(pick, h_ref[0, c]).astype(BF16)

    all_fit = functools.reduce(jnp.minimum, [ok_ref[i, kt] for kt in kts])
    lax.cond(all_fit > 0, windows, full)


def _gather_win(win, ok, pos, h, *, cap, tiles):
    b, e, n = pos.shape
    d = h.shape[2]
    smem = pl.BlockSpec(memory_space=pltpu.SMEM)
    step = tiles * SCATTER_TILE
    out = pl.pallas_call(
        functools.partial(_gather_win_kernel, tiles=tiles),
        grid=(b, n // step),
        in_specs=[smem, smem,
                  pl.BlockSpec((1, e, step), lambda i, j: (i, 0, j)),
                  pl.BlockSpec((1, step, d), lambda i, j: (i, j, 0))],
        out_specs=pl.BlockSpec((1, e, cap, d), lambda i, j: (i, 0, 0, 0)),
        out_shape=jax.ShapeDtypeStruct((b, e, cap, d), BF16),
        compiler_params=_cparams("arbitrary", "arbitrary"),
        name="moe_gather_win",
    )(win, ok, pos, h)
    return out.reshape(b, e * cap, d)


def _expert_kernel(*refs, n_sets):
    x_refs, (wg_ref, wu_ref, wd_ref) = refs[:n_sets], refs[n_sets:n_sets + 3]
    o_refs = refs[n_sets + 3:2 * n_sets + 3]
    d = x_refs[0].shape[2]
    rows = [r.shape[0] * r.shape[1] for r in x_refs]
    xs = [r[...].reshape(n, d) for r, n in zip(x_refs, rows)]
    x = xs[0] if n_sets == 1 else jnp.concatenate(xs, axis=0)
    a = _dot(x, wg_ref[0, 0].astype(BF16))
    u = _dot(x, wu_ref[0, 0].astype(BF16))
    y = _dot((a * jax.nn.sigmoid(a) * u).astype(BF16), wd_ref[0, 0].astype(BF16)).astype(BF16)
    start = 0
    for o_ref, n in zip(o_refs, rows):
        o_ref[...] = y[start:start + n].reshape(o_ref.shape)
        start += n


def _experts(xgs, caps, layer, wg, wu, wd, *, nb):
    b, _, d = xgs[0].shape
    _, e, _, f = wg.shape
    wspec = lambda r, c: pl.BlockSpec((1, 1, r, c), lambda i, j: (layer, i, 0, 0))
    xspecs = [pl.BlockSpec((nb, cap, d), lambda i, j: (j, i, 0)) for cap in caps]
    return pl.pallas_call(
        functools.partial(_expert_kernel, n_sets=len(xgs)),
        grid=(e, b // nb),
        in_specs=xspecs + [wspec(d, f), wspec(d, f), wspec(f, d)],
        out_specs=xspecs,
        out_shape=[jax.ShapeDtypeStruct(xg.shape, BF16) for xg in xgs],
        compiler_params=_cparams("arbitrary", "arbitrary"),
        name="moe_experts",
    )(*xgs, wg, wu, wd)


def _combine_matrix(rt, first, width):
    slot = lax.broadcasted_iota(jnp.int32, (rt.shape[0], width), 1).astype(F32)
    cols = [jnp.where(rt[:, e:e + 1] - first[e] == slot, rt[:, N_EXPERTS + e:N_EXPERTS + e + 1], 0.0).astype(BF16)
            for e in range(N_EXPERTS)]
    return jnp.concatenate(cols, axis=1)


def _scatter_kernel(*refs, cap, final, windowed):
    if windowed:
        win_ref, ok_ref, x_ref, rt_ref, y_ref, g2_ref, fg_ref, o_ref = refs
    else:
        x_ref, rt_ref, y_ref, g2_ref, fg_ref, o_ref = refs
    i, j = pl.program_id(0), pl.program_id(1)
    tn = x_ref.shape[1]

    def finish(rows, y):
        x = x_ref[0, rows] + g2_ref[0] * y
        if final:
            ms = jnp.mean(x * x, axis=-1, keepdims=True)
            x = x * lax.rsqrt(ms + EPS) * fg_ref[...]
        o_ref[0, rows] = x

    def full():
        rows = slice(0, tn)
        finish(rows, _dot(_combine_matrix(rt_ref[0], [0.0] * N_EXPERTS, cap), y_ref[0]))

    if not windowed:
        full()
        return
    subs = tn // SCATTER_TILE
    tiles = [j * subs + u for u in range(subs)]

    def windows():
        ys = []
        for u, tile in enumerate(tiles):
            rt = rt_ref[0, u * SCATTER_TILE:(u + 1) * SCATTER_TILE]
            first = [win_ref[i, tile, e] for e in range(N_EXPERTS)]
            picked = [y_ref[0, pl.ds(pl.multiple_of(e * cap + first[e], 16), SCATTER_WINDOW), :]
                      for e in range(N_EXPERTS)]
            a = _combine_matrix(rt, [f.astype(F32) for f in first], SCATTER_WINDOW)
            ys.append(_dot(a, jnp.concatenate(picked, axis=0)))
        for u, y in enumerate(ys):
            finish(slice(u * SCATTER_TILE, (u + 1) * SCATTER_TILE), y)

    all_fit = functools.reduce(jnp.minimum, [ok_ref[i, tile] for tile in tiles])
    lax.cond(all_fit > 0, windows, full)


def _scatter(x, rt, y, g2, fg, *, cap, final, tn, windows=None):
    b, t, d = x.shape
    smem = pl.BlockSpec(memory_space=pltpu.SMEM)
    windowed = windows is not None
    return pl.pallas_call(
        functools.partial(_scatter_kernel, cap=cap, final=final, windowed=windowed),
        grid=(b, t // tn),
        in_specs=([smem, smem] if windowed else []) + [
            pl.BlockSpec((1, tn, d), lambda i, j: (i, j, 0)),
            pl.BlockSpec((1, tn, ROUTE_ROWS), lambda i, j: (i, j, 0)),
            pl.BlockSpec((1, N_EXPERTS * cap, d), lambda i, j: (i, 0, 0)),
            pl.BlockSpec((1, 1, d), lambda i, j: (i, 0, 0)),
            pl.BlockSpec((1, d), lambda i, j: (0, 0))],
        out_specs=pl.BlockSpec((1, tn, d), lambda i, j: (i, j, 0)),
        out_shape=jax.ShapeDtypeStruct((b, t, d), F32),
        compiler_params=_cparams("arbitrary", "arbitrary"),
        name="moe_scatter",
    )(*(windows if windowed else ()), x, rt, y, g2, fg)


def _capacity(t):
    return CAPACITY_FACTOR * t // N_EXPERTS


def _dft_tables(n):
    k = np.arange(n, dtype=np.int64)
    ang = 2.0 * np.pi * ((k[:, None] * k[None, :]) % n).astype(np.float64) / n
    return np.cos(ang), np.sin(ang)


def _channel_dft():
    c, s = _dft_tables(FOURIER_GROUP_CH)
    eye = np.eye(FOURIER_GROUPS)
    return np.concatenate([np.kron(eye, c), np.kron(eye, s)], axis=1)


def _rope_tables(t):
    rows = t // GRID_W
    r, col = jnp.meshgrid(jnp.arange(rows), jnp.arange(GRID_W), indexing="ij")
    half = HEAD_DIM // 2
    inv = ROPE_THETA ** (-jnp.arange(0, half, 2, dtype=F32) / half)
    ang = jnp.concatenate([r.reshape(-1, 1).astype(F32) * inv, col.reshape(-1, 1).astype(F32) * inv], axis=-1)
    cos = jnp.repeat(jnp.cos(ang), 2, axis=1)
    sin = jnp.repeat(jnp.sin(ang), 2, axis=1) * jnp.tile(jnp.asarray([-1.0, 1.0], F32), half)
    return jnp.tile(cos, (1, HEAD_PAD)), jnp.tile(sin, (1, HEAD_PAD))


def kernel(x, c, ctx, c_ctx, w_ada, b_ada, norm1_g, w_in, q_norm_g, k_norm_g, sink, w_br_fourier, w_br_global,
           w_br_window, w_out, norm2_g, w_router, w_gate_e, w_up_e, w_down_e, final_g):
    b, t, d = x.shape
    l_ctx = ctx.shape[1]
    depth = w_ada.shape[0]
    cap_t, cap_c = _capacity(t), _capacity(l_ctx)

    cos_t, sin_t = _rope_tables(t)
    cos_c, sin_c = jnp.ones((l_ctx, LANES), F32), jnp.zeros((l_ctx, LANES), F32)
    head_avg = jnp.asarray(np.kron(np.eye(MXU_DIM // HEAD_DIM), np.full((HEAD_DIM, HEAD_DIM), 1.0 / HEAD_DIM)), BF16)
    cdft = jnp.asarray(_channel_dft(), F32)
    cn_t, sn_t = (jnp.asarray(a, F32) for a in _dft_tables(t))
    cn_c, sn_c = (jnp.asarray(a, F32) for a in _dft_tables(l_ctx))
    tri = jnp.asarray(np.triu(np.ones((LANES, LANES))), BF16)

    cvec = jnp.concatenate([c, c_ctx[None], jnp.zeros((MOD_ROWS - b - 1, d), F32)], axis=0)
    mods = _ada(cvec, w_ada, b_ada)

    xc = ctx
    for l in range(depth):
        need_ctx = l < depth - 1
        final = l == depth - 1
        lat = [mods[l, :b, i * d:(i + 1) * d].reshape(b, 1, d) for i in range(6)]
        cmod = [jnp.broadcast_to(mods[l, b, i * d:(i + 1) * d].reshape(1, 1, d), (b, 1, d)) for i in range(6)]
        w_mix = w_in[l, :, :MIX_W].astype(BF16)
        w_gate = w_in[l, :, MIX_W:].astype(BF16)
        n1, n2 = norm1_g[l].reshape(1, d), norm2_g[l].reshape(1, d)
        qn = jnp.tile(q_norm_g[l], G_HEADS).reshape(1, GQ_W)
        kn = jnp.tile(k_norm_g[l], G_KV).reshape(1, GKV_W)
        wbf, wbg, wbw = (w[l].astype(BF16) for w in (w_br_fourier, w_br_global, w_br_window))
        wout = w_out[l].astype(BF16)
        wr = jnp.pad(w_router[l], ((0, 0), (0, LANES - N_EXPERTS))).astype(BF16)
        fg = final_g.reshape(1, d)
        merge_w = (n1, n2, w_gate, wbf, wbg, wbw, wout, wr)

        cfcs, cqg, ckg, cvg, cqw, ckw, cvw = _inproj(xc, cmod[0], cmod[1], n1, w_mix, cos_c, sin_c, head_avg, cdft,
                                                     qn, kn, rope=False, tm=l_ctx, chunks=2)
        xgs, caps = [], []
        if need_ctx:
            cf_mix = _fourier(cfcs, cn_c, sn_c, tr=l_ctx)
            oc_g, oc_w = _ctx_attn(sink[l], cqg, ckg, cvg, cqw, ckw, cvw)
            xc1, hc2, caff = _merge(xc, cf_mix, oc_g, oc_w, *cmod[:5], *merge_w, tm=l_ctx, chunks=1)
            cpos, crt = _route(caff, tri, cap=cap_c, windows=False)
            xgs.append(_gather(cpos, hc2, cap=cap_c, group=4))
            caps.append(cap_c)

        fcs, qg, kg, vg, qw, kw, vw = _inproj(x, lat[0], lat[1], n1, w_mix, cos_t, sin_t, head_avg, cdft, qn, kn,
                                              rope=True, tm=512, chunks=4)
        f_mix = _fourier(fcs, cn_t, sn_t, tr=512)
        o_g = _global_attn(qg, kg, vg, ckg, cvg, tq=256)
        o_w = _window_attn(sink[l], qw, kw, vw, ckw, cvw)
        x1, h2, aff = _merge(x, f_mix, o_g, o_w, *lat[:5], *merge_w, tm=512, chunks=2)
        pos, rt, win, win_ok = _route(aff, tri, cap=cap_t, windows=True)
        xgs.insert(0, _gather_win(win, win_ok, pos, h2, cap=cap_t, tiles=2))
        caps.insert(0, cap_t)

        ys = _experts(xgs, caps, l, w_gate_e, w_up_e, w_down_e, nb=4)
        x = _scatter(x1, rt, ys[0], lat[5], fg, cap=cap_t, final=final, tn=512, windows=(win, win_ok))
        if need_ctx:
            xc = _scatter(xc1, crt, ys[1], cmod[5], fg, cap=cap_c, final=False, tn=l_ctx)
    return x
```

```python
import functools

import numpy as np
import jax
import jax.numpy as jnp
from jax import lax
from jax.experimental import pallas as pl
from jax.experimental.pallas import tpu as pltpu

F32 = jnp.float32
BF16 = jnp.bfloat16

D_MODEL = 1024
HEAD_DIM = 64
GRID_W = 64
FOURIER_GROUPS = 4
FOURIER_GROUP_CH = 64
FOURIER_W = FOURIER_GROUPS * FOURIER_GROUP_CH
G_HEADS, G_KV = 8, 2
G_GROUP = G_HEADS // G_KV
W_HEADS, W_KV = 4, 2
W_GROUP = W_HEADS // W_KV
WINDOW = 128
Q_BLOCK = 128
N_BRANCH = 3
GQ_W = G_HEADS * HEAD_DIM
GKV_W = G_KV * HEAD_DIM
WQ_W = W_HEADS * HEAD_DIM
WKV_W = W_KV * HEAD_DIM
OFF_F = 0
OFF_GQ = OFF_F + FOURIER_W
OFF_GK = OFF_GQ + GQ_W
OFF_GV = OFF_GK + GKV_W
OFF_WQ = OFF_GV + GKV_W
OFF_WK = OFF_WQ + WQ_W
OFF_WV = OFF_WK + WKV_W
MIX_W = OFF_WV + WKV_W
GATE_W = N_BRANCH * D_MODEL
ROPE_THETA = 10000.0
N_EXPERTS = 16
CAPACITY_FACTOR = 2
EPS = 1e-6
NEG_INF = -1e30
LANES = 128
HEAD_PAD = LANES // HEAD_DIM
MXU_DIM = 256
MOD_ROWS = 16
SCATTER_TILE = 256
SCATTER_WINDOW = 64
ROUTE_ROWS = 128
WINDOW_TILE = 4
GLOBAL_CHAINS_PER_KV = 1
VMEM_LIMIT = 56 * 1024 * 1024


def _cparams(*sem):
    return pltpu.CompilerParams(dimension_semantics=sem, vmem_limit_bytes=VMEM_LIMIT)


def _norm_mod(x, g, sh, sc):
    ms = jnp.mean(x * x, axis=-1, keepdims=True)
    return (x * lax.rsqrt(ms + EPS) * g) * (1.0 + sc) + sh


def _dot(a, b):
    return jnp.dot(a, b, preferred_element_type=F32)


def _dot_nt(a, b):
    return lax.dot_general(a, b, (((1,), (1,)), ((), ())), preferred_element_type=F32)


def _ada_kernel(c_ref, w_ref, b_ref, o_ref):
    c = c_ref[...]
    s = (c * jax.nn.sigmoid(c)).astype(BF16)
    o_ref[0] = _dot(s, w_ref[0].astype(BF16)) + b_ref[0]


def _ada(cvec, w_ada, b_ada):
    depth, d, n = w_ada.shape
    tn = 1536
    return pl.pallas_call(
        _ada_kernel,
        grid=(depth, n // tn),
        in_specs=[pl.BlockSpec((MOD_ROWS, d), lambda l, j: (0, 0)),
                  pl.BlockSpec((1, d, tn), lambda l, j: (l, 0, j)),
                  pl.BlockSpec((1, 1, tn), lambda l, j: (l, 0, j))],
        out_specs=pl.BlockSpec((1, MOD_ROWS, tn), lambda l, j: (l, 0, j)),
        out_shape=jax.ShapeDtypeStruct((depth, MOD_ROWS, n), F32),
        compiler_params=_cparams("arbitrary", "arbitrary"),
        name="ada",
    )(cvec, w_ada, b_ada.reshape(depth, 1, n))


def _rope(x, cos, sin_signed, even):
    outs = []
    for j in range(x.shape[1] // LANES):
        xb = x[:, j * LANES:(j + 1) * LANES]
        swap = jnp.where(even, pltpu.roll(xb, LANES - 1, 1), pltpu.roll(xb, 1, 1))
        outs.append(xb * cos + swap * sin_signed)
    return outs[0] if len(outs) == 1 else jnp.concatenate(outs, axis=1)


def _head_mean_square(z, bd):
    zz = (z * z).astype(BF16)
    blk = bd.shape[0]
    if z.shape[1] < blk:
        return _dot(zz, bd[:z.shape[1], :z.shape[1]])
    outs = [_dot(zz[:, j * blk:(j + 1) * blk], bd) for j in range(z.shape[1] // blk)]
    return outs[0] if len(outs) == 1 else jnp.concatenate(outs, axis=1)


def _pad_heads_f32(x, fill):
    blk = jnp.full((x.shape[0], LANES - HEAD_DIM), fill, F32)
    parts = []
    for h in range(x.shape[1] // HEAD_DIM):
        parts += [x[:, h * HEAD_DIM:(h + 1) * HEAD_DIM], blk]
    return jnp.concatenate(parts, axis=1)


def _pad_heads(x, fill):
    return _pad_heads_f32(x, fill).astype(BF16)


def _inproj_kernel(x_ref, sh_ref, sc_ref, g_ref, w_ref, cos_ref, sin_ref, bd_ref, cdft_ref, qn_ref, kn_ref,
                   fcs_ref, qg_ref, kg_ref, vg_ref, qw_ref, kw_ref, vw_ref, *, rope, chunks):
    tm = x_ref.shape[1]
    rows = [slice(c * (tm // chunks), (c + 1) * (tm // chunks)) for c in range(chunks)]
    zs = [_dot(_norm_mod(x_ref[0, r], g_ref[...], sh_ref[0], sc_ref[0]).astype(BF16), w_ref[...]) for r in rows]
    bd = bd_ref[...]
    scale = HEAD_DIM ** -0.5
    for r, z in zip(rows, zs):
        fcs_ref[0, r] = _dot(z[:, OFF_F:OFF_F + FOURIER_W].astype(BF16), cdft_ref[...].astype(BF16)).astype(BF16)
        q = z[:, OFF_GQ:OFF_GQ + GQ_W]
        k = z[:, OFF_GK:OFF_GK + GKV_W]
        q = q * lax.rsqrt(_head_mean_square(q, bd) + EPS) * qn_ref[...]
        k = k * lax.rsqrt(_head_mean_square(k, bd) + EPS) * kn_ref[...]
        qw = z[:, OFF_WQ:OFF_WQ + WQ_W]
        kw = z[:, OFF_WK:OFF_WK + WKV_W]
        if rope:
            cos, sin = cos_ref[r], sin_ref[r]
            even = (lax.broadcasted_iota(jnp.int32, cos.shape, 1) % 2) == 0
            q, k = _rope(q, cos, sin, even), _rope(k, cos, sin, even)
            qw, kw = _rope(qw, cos, sin, even), _rope(kw, cos, sin, even)
        qg_ref[0, r] = _pad_heads(q * scale, 0.0)
        kg_ref[0, r] = _pad_heads(k, 0.0)
        vg_ref[0, r] = _pad_heads(z[:, OFF_GV:OFF_GV + GKV_W], 1.0)
        qw_ref[0, r] = _pad_heads(qw * scale, 0.0)
        kw_ref[0, r] = _pad_heads(kw, 0.0)
        vw_ref[0, r] = _pad_heads(z[:, OFF_WV:OFF_WV + WKV_W], 1.0)


def _inproj(x, sh, sc, g, w_mix, cos_t, sin_t, bd, cdft, qn, kn, *, rope, tm, chunks):
    b, t, d = x.shape
    tok = lambda w: pl.BlockSpec((1, tm, w), lambda i, j: (i, j, 0))
    row = pl.BlockSpec((1, 1, d), lambda i, j: (i, 0, 0))
    const = lambda shape: pl.BlockSpec(shape, lambda i, j: (0,) * len(shape))
    widths = (2 * FOURIER_W,) + tuple(HEAD_PAD * w for w in (GQ_W, GKV_W, GKV_W, WQ_W, WKV_W, WKV_W))
    out_specs = [tok(w) for w in widths]
    out_shape = [jax.ShapeDtypeStruct((b, t, w), BF16) for w in widths]
    return pl.pallas_call(
        functools.partial(_inproj_kernel, rope=rope, chunks=chunks),
        grid=(b, t // tm),
        in_specs=[tok(d), row, row, const((1, d)), const((d, MIX_W)),
                  pl.BlockSpec((tm, LANES), lambda i, j: (j, 0)),
                  pl.BlockSpec((tm, LANES), lambda i, j: (j, 0)),
                  const((MXU_DIM, MXU_DIM)), const((FOURIER_W, 2 * FOURIER_W)),
                  const((1, GQ_W)), const((1, GKV_W))],
        out_specs=out_specs,
        out_shape=out_shape,
        compiler_params=_cparams("arbitrary", "arbitrary"),
        name="inproj_rope" if rope else "inproj_ctx",
    )(x, sh, sc, g, w_mix, cos_t, sin_t, bd, cdft, qn, kn)


def _fourier_kernel(cn_ref, sn_ref, fcs_ref, o_ref, cn_s, sn_s, *, scale):
    @pl.when(pl.program_id(1) == 0)
    def _():
        cn_s[...] = cn_ref[...].astype(BF16)
        sn_s[...] = sn_ref[...].astype(BF16)

    fcs = fcs_ref[0]
    re = _dot(cn_s[...], fcs[:, :FOURIER_W]) - _dot(sn_s[...], fcs[:, FOURIER_W:])
    o_ref[0] = (re * scale).astype(BF16)


def _fourier(fcs, cn, sn, *, tr):
    b, n, _ = fcs.shape
    scale = float((n * FOURIER_GROUP_CH) ** -0.5)
    return pl.pallas_call(
        functools.partial(_fourier_kernel, scale=scale),
        grid=(n // tr, b),
        in_specs=[pl.BlockSpec((tr, n), lambda r, i: (r, 0)),
                  pl.BlockSpec((tr, n), lambda r, i: (r, 0)),
                  pl.BlockSpec((1, n, 2 * FOURIER_W), lambda r, i: (i, 0, 0))],
        out_specs=pl.BlockSpec((1, tr, FOURIER_W), lambda r, i: (i, r, 0)),
        out_shape=jax.ShapeDtypeStruct((b, n, FOURIER_W), BF16),
        scratch_shapes=[pltpu.VMEM((tr, n), BF16), pltpu.VMEM((tr, n), BF16)],
        compiler_params=_cparams("arbitrary", "arbitrary"),
        name="fourier",
    )(cn, sn, fcs)


def _head(x, h):
    return x[:, h * LANES:(h + 1) * LANES]


def _stack_heads(q, first, count):
    return jnp.concatenate([_head(q, first + g) for g in range(count)], axis=0)


def _softmax_pv(parts, extra=None):
    m = functools.reduce(jnp.maximum, [jnp.max(s, axis=-1, keepdims=True) for s, _ in parts])
    if extra is not None:
        m = jnp.maximum(m, extra)
    acc = 0.0
    for s, v in parts:
        acc = acc + _dot(jnp.exp((s - m).astype(BF16)), v)
    den = acc[:, HEAD_DIM:HEAD_DIM + 1]
    if extra is not None:
        den = den + jnp.exp(extra - m)
    return acc[:, :HEAD_DIM] / den


def _unstack_heads(o_list, tq, count):
    cols = []
    for o in o_list:
        cols += [o[g * tq:(g + 1) * tq] for g in range(count)]
    return jnp.concatenate(cols, axis=1)


def _global_attn_kernel(q_ref, kl_ref, vl_ref, kc_ref, vc_ref, o_ref):
    q = q_ref[0]
    tq = q.shape[0]
    per_chain = G_GROUP // GLOBAL_CHAINS_PER_KV
    parts = []
    for kv in range(G_KV):
        for c in range(GLOBAL_CHAINS_PER_KV):
            qs = _stack_heads(q, kv * G_GROUP + c * per_chain, per_chain)
            parts.append([(_dot_nt(qs, _head(kc_ref[0], kv)), _head(vc_ref[0], kv)),
                          (_dot_nt(qs, _head(kl_ref[0], kv)), _head(vl_ref[0], kv))])
    outs = [_softmax_pv(p) for p in parts]
    o_ref[0] = _unstack_heads(outs, tq, per_chain).astype(BF16)


def _global_attn(q, kl, vl, kc, vc, *, tq):
    b, t, _ = q.shape
    l = kc.shape[1]
    full = lambda n: pl.BlockSpec((1, n, HEAD_PAD * GKV_W), lambda i, j: (i, 0, 0))
    return pl.pallas_call(
        _global_attn_kernel,
        grid=(b, t // tq),
        in_specs=[pl.BlockSpec((1, tq, HEAD_PAD * GQ_W), lambda i, j: (i, j, 0)), full(t), full(t), full(l), full(l)],
        out_specs=pl.BlockSpec((1, tq, GQ_W), lambda i, j: (i, j, 0)),
        out_shape=jax.ShapeDtypeStruct((b, t, GQ_W), BF16),
        compiler_params=_cparams("arbitrary", "arbitrary"),
        name="global_attn",
    )(q, kl, vl, kc, vc)


def _sink_column(sink_ref, kv, rows_per_head):
    r = lax.broadcasted_iota(jnp.int32, (W_GROUP * rows_per_head, 1), 0)
    col = jnp.full(r.shape, sink_ref[kv * W_GROUP], F32)
    for g in range(1, W_GROUP):
        col = jnp.where(r >= g * rows_per_head, sink_ref[kv * W_GROUP + g], col)
    return col


def _window_attn_kernel(sink_ref, q_ref, kp_ref, k0_ref, kn_ref, vp_ref, v0_ref, vn_ref, kc_ref, vc_ref, o_ref,
                        *, n_tiles):
    j = pl.program_id(1)
    q = q_ref[0]
    kb = jnp.concatenate([kp_ref[0], k0_ref[0], kn_ref[0]], axis=0)
    vb = jnp.concatenate([vp_ref[0], v0_ref[0], vn_ref[0]], axis=0)
    rows = W_GROUP * Q_BLOCK
    qpos = lax.broadcasted_iota(jnp.int32, (rows, 3 * Q_BLOCK), 0) % Q_BLOCK
    kpos = lax.broadcasted_iota(jnp.int32, (rows, 3 * Q_BLOCK), 1)
    dist = jnp.abs(qpos - kpos + Q_BLOCK)
    in_window = dist <= WINDOW
    outside = jnp.full(dist.shape, WINDOW + 1, jnp.int32)
    first_ok = jnp.where((kpos < Q_BLOCK) & (j == 0), outside, dist) <= WINDOW
    last_ok = jnp.where((kpos >= 2 * Q_BLOCK) & (j == n_tiles - 1), outside, dist) <= WINDOW
    parts = []
    for i in range(WINDOW_TILE):
        valid = first_ok if i == 0 else (last_ok if i == WINDOW_TILE - 1 else in_window)
        q_i = q[i * Q_BLOCK:(i + 1) * Q_BLOCK]
        band = slice(i * Q_BLOCK, (i + 3) * Q_BLOCK)
        for kv in range(W_KV):
            qs = _stack_heads(q_i, kv * W_GROUP, W_GROUP)
            s_b = jnp.where(valid, _dot_nt(qs, _head(kb, kv)[band]), NEG_INF)
            parts.append([(_dot_nt(qs, _head(kc_ref[0], kv)), _head(vc_ref[0], kv)), (s_b, _head(vb, kv)[band])])
    sinks = [_sink_column(sink_ref, kv, Q_BLOCK) for kv in range(W_KV)]
    outs = [_softmax_pv(p, sinks[n % W_KV]) for n, p in enumerate(parts)]
    blocks = [_unstack_heads(outs[i * W_KV:(i + 1) * W_KV], Q_BLOCK, W_GROUP) for i in range(WINDOW_TILE)]
    o_ref[0] = jnp.concatenate(blocks, axis=0).astype(BF16)


def _window_attn(sink, q, k, v, kc, vc):
    b, t, _ = q.shape
    l = kc.shape[1]
    nb = t // Q_BLOCK
    tile = WINDOW_TILE * Q_BLOCK
    kvw = HEAD_PAD * WKV_W
    edge = lambda f: pl.BlockSpec((1, Q_BLOCK, kvw), lambda i, j: (i, f(j), 0))
    prev = lambda j: jnp.maximum(j * WINDOW_TILE - 1, 0)
    nxt = lambda j: jnp.minimum((j + 1) * WINDOW_TILE, nb - 1)
    mid = pl.BlockSpec((1, tile, kvw), lambda i, j: (i, j, 0))
    full = pl.BlockSpec((1, l, kvw), lambda i, j: (i, 0, 0))
    return pl.pallas_call(
        functools.partial(_window_attn_kernel, n_tiles=t // tile),
        grid=(b, t // tile),
        in_specs=[pl.BlockSpec(memory_space=pltpu.SMEM),
                  pl.BlockSpec((1, tile, HEAD_PAD * WQ_W), lambda i, j: (i, j, 0)),
                  edge(prev), mid, edge(nxt), edge(prev), mid, edge(nxt), full, full],
        out_specs=pl.BlockSpec((1, tile, WQ_W), lambda i, j: (i, j, 0)),
        out_shape=jax.ShapeDtypeStruct((b, t, WQ_W), BF16),
        compiler_params=_cparams("arbitrary", "arbitrary"),
        name="window_attn",
    )(sink, q, k, k, k, v, v, v, kc, vc)


def _ctx_attn_kernel(sink_ref, qg_ref, kg_ref, vg_ref, qw_ref, kw_ref, vw_ref, og_ref, ow_ref):
    l = qg_ref.shape[1]
    qg, qw = qg_ref[0], qw_ref[0]
    g_parts = [[(_dot_nt(_stack_heads(qg, kv * G_GROUP, G_GROUP), _head(kg_ref[0], kv)), _head(vg_ref[0], kv))]
               for kv in range(G_KV)]
    w_parts = [[(_dot_nt(_stack_heads(qw, kv * W_GROUP, W_GROUP), _head(kw_ref[0], kv)), _head(vw_ref[0], kv))]
               for kv in range(W_KV)]
    og_ref[0] = _unstack_heads([_softmax_pv(p) for p in g_parts], l, G_GROUP).astype(BF16)
    outs = [_softmax_pv(p, _sink_column(sink_ref, kv, l)) for kv, p in enumerate(w_parts)]
    ow_ref[0] = _unstack_heads(outs, l, W_GROUP).astype(BF16)


def _ctx_attn(sink, qg, kg, vg, qw, kw, vw):
    b, l, _ = qg.shape
    spec = lambda w: pl.BlockSpec((1, l, w), lambda i: (i, 0, 0))
    padded = lambda w: spec(HEAD_PAD * w)
    return pl.pallas_call(
        _ctx_attn_kernel,
        grid=(b,),
        in_specs=[pl.BlockSpec(memory_space=pltpu.SMEM),
                  padded(GQ_W), padded(GKV_W), padded(GKV_W), padded(WQ_W), padded(WKV_W), padded(WKV_W)],
        out_specs=[spec(GQ_W), spec(WQ_W)],
        out_shape=[jax.ShapeDtypeStruct((b, l, GQ_W), BF16), jax.ShapeDtypeStruct((b, l, WQ_W), BF16)],
        compiler_params=_cparams("arbitrary"),
        name="ctx_attn",
    )(sink, qg, kg, vg, qw, kw, vw)


def _merge_kernel(x_ref, f_ref, og_ref, ow_ref, sh1_ref, sc1_ref, g1_ref, sh2_ref, sc2_ref,
                  n1_ref, n2_ref, wgate_ref, wbf_ref, wbg_ref, wbw_ref, wout_ref, wr_ref,
                  x1_ref, h2_ref, aff_ref, *, chunks):
    tm = x_ref.shape[1]
    rows = [slice(c * (tm // chunks), (c + 1) * (tm // chunks)) for c in range(chunks)]
    d = D_MODEL
    heads = []
    for r in rows:
        h = _norm_mod(x_ref[0, r], n1_ref[...], sh1_ref[0], sc1_ref[0]).astype(BF16)
        heads.append((_dot(h, wgate_ref[...]), _dot(f_ref[0, r], wbf_ref[...]),
                      _dot(og_ref[0, r], wbg_ref[...]), _dot(ow_ref[0, r], wbw_ref[...])))
    for r, (gate_logits, bf, bg, bw) in zip(rows, heads):
        gate = jax.nn.sigmoid(gate_logits)
        m = gate[:, 0:d] * bf + gate[:, d:2 * d] * bg + gate[:, 2 * d:3 * d] * bw
        x1 = x_ref[0, r] + g1_ref[0] * _dot(m.astype(BF16), wout_ref[...])
        x1_ref[0, r] = x1
        h2 = _norm_mod(x1, n2_ref[...], sh2_ref[0], sc2_ref[0]).astype(BF16)
        h2_ref[0, r] = h2
        logits = _dot(h2, wr_ref[...]).T[:N_EXPERTS]
        e = jnp.exp(logits - jnp.max(logits, axis=0, keepdims=True))
        aff_ref[0, :, r] = e / jnp.sum(e, axis=0, keepdims=True)


def _merge(x, f, og, ow, sh1, sc1, g1, sh2, sc2, n1, n2, wgate, wbf, wbg, wbw, wout, wr, *, tm, chunks):
    b, t, d = x.shape
    tok = lambda w: pl.BlockSpec((1, tm, w), lambda i, j: (i, j, 0))
    row = pl.BlockSpec((1, 1, d), lambda i, j: (i, 0, 0))
    const = lambda shape: pl.BlockSpec(shape, lambda i, j: (0,) * len(shape))
    return pl.pallas_call(
        functools.partial(_merge_kernel, chunks=chunks),
        grid=(b, t // tm),
        in_specs=[tok(d), tok(FOURIER_W), tok(GQ_W), tok(WQ_W), row, row, row, row, row,
                  const((1, d)), const((1, d)), const((d, GATE_W)), const((FOURIER_W, d)),
                  const((GQ_W, d)), const((WQ_W, d)), const((d, d)), const((d, LANES))],
        out_specs=[tok(d), tok(d), pl.BlockSpec((1, N_EXPERTS, tm), lambda i, j: (i, 0, j))],
        out_shape=[jax.ShapeDtypeStruct((b, t, d), F32), jax.ShapeDtypeStruct((b, t, d), BF16),
                   jax.ShapeDtypeStruct((b, N_EXPERTS, t), F32)],
        compiler_params=_cparams("arbitrary", "arbitrary"),
        name="merge",
    )(x, f, og, ow, sh1, sc1, g1, sh2, sc2, n1, n2, wgate, wbf, wbg, wbw, wout, wr)


def _cumsum_lanes(m, tri):
    e, n = m.shape
    nch = n // LANES
    stacked = jnp.concatenate([m[:, j * LANES:(j + 1) * LANES] for j in range(nch)], axis=0).astype(BF16)
    w = _dot(stacked, tri)
    outs, off = [], jnp.zeros((e, 1), F32)
    for j in range(nch):
        wj = w[j * e:(j + 1) * e]
        outs.append(wj + off)
        off = off + wj[:, LANES - 1:LANES]
    return jnp.concatenate(outs, axis=1)


def _route_kernel(aff_ref, tri_ref, pos_ref, rt_ref, *win_refs, cap, n_exp):
    aff = aff_ref[...]
    e, n = aff.shape
    thr_bits = jnp.zeros((e, 1), jnp.int32)
    for bit in range(30, -1, -1):
        cand = thr_bits | (1 << bit)
        cnt = jnp.sum(jnp.where(aff >= pltpu.bitcast(cand, F32), 1.0, 0.0), axis=1, keepdims=True)
        thr_bits = jnp.where(cnt >= cap, cand, thr_bits)
    ge = jnp.where(aff >= pltpu.bitcast(thr_bits, F32), 1.0, 0.0)
    gt = jnp.where(aff >= pltpu.bitcast(thr_bits + 1, F32), 1.0, 0.0)
    eq = ge - gt
    room = cap - jnp.sum(gt, axis=1, keepdims=True)
    tri = tri_ref[...]
    sel = gt + jnp.where(_cumsum_lanes(eq, tri) <= room, eq, 0.0)
    cums = _cumsum_lanes(sel, tri)
    pos = jnp.where(sel > 0.0, cums - 1.0, -1.0)
    pos_ref[...] = pos.astype(jnp.int32)
    weight = sel * aff
    pad = jnp.zeros((ROUTE_ROWS - 2 * n_exp, n), F32)
    for i in range(e // n_exp):
        rows = slice(i * n_exp, (i + 1) * n_exp)
        rt_ref[i] = jnp.concatenate([pos[rows], weight[rows], pad], axis=0).T
    if win_refs:
        win_ref, ok_ref = win_refs
        n_tiles = n // SCATTER_TILE
        ends = jnp.concatenate([cums[:, (k + 1) * SCATTER_TILE - 1:(k + 1) * SCATTER_TILE] for k in range(n_tiles)],
                               axis=1)
        starts = jnp.concatenate([jnp.zeros((e, 1), F32), ends[:, :n_tiles - 1]], axis=1)
        align = 16.0
        first = jnp.minimum(jnp.floor(starts / align) * align, float(cap - SCATTER_WINDOW))
        fits = jnp.where(ends - first <= SCATTER_WINDOW, 1.0, 0.0)
        win_ref[...] = first.astype(jnp.int32)
        ok = [jnp.min(fits[i * n_exp:(i + 1) * n_exp], axis=0, keepdims=True) for i in range(e // n_exp)]
        ok_ref[...] = jnp.concatenate(ok, axis=0).astype(jnp.int32)


def _route(aff, tri, *, cap, windows):
    b, e, n = aff.shape
    whole = lambda shape: pl.BlockSpec(shape, lambda i: (0,) * len(shape))
    out_shape = [jax.ShapeDtypeStruct((b * e, n), jnp.int32), jax.ShapeDtypeStruct((b, n, ROUTE_ROWS), F32)]
    if windows:
        n_tiles = n // SCATTER_TILE
        out_shape += [jax.ShapeDtypeStruct((b * e, n_tiles), jnp.int32), jax.ShapeDtypeStruct((b, n_tiles), jnp.int32)]
    outs = pl.pallas_call(
        functools.partial(_route_kernel, cap=cap, n_exp=e),
        grid=(1,),
        in_specs=[whole((b * e, n)), whole((LANES, LANES))],
        out_specs=[whole(s.shape) for s in out_shape],
        out_shape=out_shape,
        compiler_params=_cparams("arbitrary"),
        name="route",
    )(aff.reshape(b * e, n), tri)
    pos, rt = outs[0].reshape(b, e, n), outs[1]
    if not windows:
        return pos, rt
    win = outs[2].reshape(b, e, n_tiles).transpose(0, 2, 1)
    return pos, rt, win, outs[3]


def _gather_kernel(pos_ref, h_ref, o_ref, *, cap, group):
    j = pl.program_id(1)
    n = h_ref.shape[1]
    slot = lax.broadcasted_iota(jnp.int32, (cap, n), 0)
    sel = [jnp.where(pos_ref[0, pl.ds(j * group + g, 1), :] == slot, 1.0, 0.0).astype(BF16) for g in range(group)]
    o_ref[0] = _dot(jnp.concatenate(sel, axis=0), h_ref[0]).astype(BF16)


def _gather(pos, h, *, cap, group):
    b, e, n = pos.shape
    d = h.shape[2]
    return pl.pallas_call(
        functools.partial(_gather_kernel, cap=cap, group=group),
        grid=(b, e // group),
        in_specs=[pl.BlockSpec((1, e, n), lambda i, j: (i, 0, 0)), pl.BlockSpec((1, n, d), lambda i, j: (i, 0, 0))],
        out_specs=pl.BlockSpec((1, group * cap, d), lambda i, j: (i, j, 0)),
        out_shape=jax.ShapeDtypeStruct((b, e * cap, d), BF16),
        compiler_params=_cparams("arbitrary", "arbitrary"),
        name="moe_gather",
    )(pos, h)


def _gather_win_kernel(win_ref, ok_ref, pos_ref, h_ref, o_ref, *, tiles):
    i, j = pl.program_id(0), pl.program_id(1)
    cap = o_ref.shape[2]

    @pl.when(j == 0)
    def _():
        o_ref[...] = jnp.zeros_like(o_ref)

    kts = [j * tiles + u for u in range(tiles)]
    cols = [slice(u * SCATTER_TILE, (u + 1) * SCATTER_TILE) for u in range(tiles)]

    def windows():
        slot = lax.broadcasted_iota(jnp.int32, (SCATTER_WINDOW, SCATTER_TILE), 0)
        firsts, zs = [], []
        for kt, c in zip(kts, cols):
            pos = pos_ref[0, :, c]
            first = [win_ref[i, kt, e] for e in range(N_EXPERTS)]
            pick = [jnp.where(pos[e:e + 1] - first[e] == slot, 1.0, 0.0).astype(BF16) for e in range(N_EXPERTS)]
            zs.append(_dot(jnp.concatenate(pick, axis=0), h_ref[0, c]))
            firsts.append(first)
        for first, z in zip(firsts, zs):
            for e in range(N_EXPERTS):
                rows = pl.ds(pl.multiple_of(first[e], 16), SCATTER_WINDOW)
                o_ref[0, e, rows, :] += z[e * SCATTER_WINDOW:(e + 1) * SCATTER_WINDOW].astype(BF16)

    def full():
        slot = lax.broadcasted_iota(jnp.int32, (cap, SCATTER_TILE), 0)
        for c in cols:
            for e in range(N_EXPERTS):
                pick = jnp.where(pos_ref[0, e:e + 1, c] == slot, 1.0, 0.0).astype(BF16)
                o_ref[0, e] += _dot(pick, h_ref[0, c]).astype(BF16)

    all_fit = functools.reduce(jnp.minimum, [ok_ref[i, kt] for kt in kts])
    lax.cond(all_fit > 0, windows, full)


def _gather_win(win, ok, pos, h, *, cap, tiles):
    b, e, n = pos.shape
    d = h.shape[2]
    smem = pl.BlockSpec(memory_space=pltpu.SMEM)
    step = tiles * SCATTER_TILE
    out = pl.pallas_call(
        functools.partial(_gather_win_kernel, tiles=tiles),
        grid=(b, n // step),
        in_specs=[smem, smem,
                  pl.BlockSpec((1, e, step), lambda i, j: (i, 0, j)),
                  pl.BlockSpec((1, step, d), lambda i, j: (i, j, 0))],
        out_specs=pl.BlockSpec((1, e, cap, d), lambda i, j: (i, 0, 0, 0)),
        out_shape=jax.ShapeDtypeStruct((b, e, cap, d), BF16),
        compiler_params=_cparams("arbitrary", "arbitrary"),
        name="moe_gather_win",
    )(win, ok, pos, h)
    return out.reshape(b, e * cap, d)


def _expert_kernel(*refs, n_sets):
    x_refs, (wg_ref, wu_ref, wd_ref) = refs[:n_sets], refs[n_sets:n_sets + 3]
    o_refs = refs[n_sets + 3:2 * n_sets + 3]
    d = x_refs[0].shape[2]
    rows = [r.shape[0] * r.shape[1] for r in x_refs]
    xs = [r[...].reshape(n, d) for r, n in zip(x_refs, rows)]
    x = xs[0] if n_sets == 1 else jnp.concatenate(xs, axis=0)
    a = _dot(x, wg_ref[0, 0].astype(BF16))
    u = _dot(x, wu_ref[0, 0].astype(BF16))
    y = _dot((a * jax.nn.sigmoid(a) * u).astype(BF16), wd_ref[0, 0].astype(BF16)).astype(BF16)
    start = 0
    for o_ref, n in zip(o_refs, rows):
        o_ref[...] = y[start:start + n].reshape(o_ref.shape)
        start += n


def _experts(xgs, caps, layer, wg, wu, wd, *, nb):
    b, _, d = xgs[0].shape
    _, e, _, f = wg.shape
    wspec = lambda r, c: pl.BlockSpec((1, 1, r, c), lambda i, j: (layer, i, 0, 0))
    xspecs = [pl.BlockSpec((nb, cap, d), lambda i, j: (j, i, 0)) for cap in caps]
    return pl.pallas_call(
        functools.partial(_expert_kernel, n_sets=len(xgs)),
        grid=(e, b // nb),
        in_specs=xspecs + [wspec(d, f), wspec(d, f), wspec(f, d)],
        out_specs=xspecs,
        out_shape=[jax.ShapeDtypeStruct(xg.shape, BF16) for xg in xgs],
        compiler_params=_cparams("arbitrary", "arbitrary"),
        name="moe_experts",
    )(*xgs, wg, wu, wd)


def _combine_matrix(rt, first, width):
    slot = lax.broadcasted_iota(jnp.int32, (rt.shape[0], width), 1).astype(F32)
    cols = [jnp.where(rt[:, e:e + 1] - first[e] == slot, rt[:, N_EXPERTS + e:N_EXPERTS + e + 1], 0.0).astype(BF16)
            for e in range(N_EXPERTS)]
    return jnp.concatenate(cols, axis=1)


def _scatter_kernel(*refs, cap, final, windowed):
    if windowed:
        win_ref, ok_ref, x_ref, rt_ref, y_ref, g2_ref, fg_ref, o_ref = refs
    else:
        x_ref, rt_ref, y_ref, g2_ref, fg_ref, o_ref = refs
    i, j = pl.program_id(0), pl.program_id(1)
    tn = x_ref.shape[1]

    def finish(rows, y):
        x = x_ref[0, rows] + g2_ref[0] * y
        if final:
            ms = jnp.mean(x * x, axis=-1, keepdims=True)
            x = x * lax.rsqrt(ms + EPS) * fg_ref[...]
        o_ref[0, rows] = x

    def full():
        rows = slice(0, tn)
        finish(rows, _dot(_combine_matrix(rt_ref[0], [0.0] * N_EXPERTS, cap), y_ref[0]))

    if not windowed:
        full()
        return
    subs = tn // SCATTER_TILE
    tiles = [j * subs + u for u in range(subs)]

    def windows():
        ys = []
        for u, tile in enumerate(tiles):
            rt = rt_ref[0, u * SCATTER_TILE:(u + 1) * SCATTER_TILE]
            first = [win_ref[i, tile, e] for e in range(N_EXPERTS)]
            picked = [y_ref[0, pl.ds(pl.multiple_of(e * cap + first[e], 16), SCATTER_WINDOW), :]
                      for e in range(N_EXPERTS)]
            a = _combine_matrix(rt, [f.astype(F32) for f in first], SCATTER_WINDOW)
            ys.append(_dot(a, jnp.concatenate(picked, axis=0)))
        for u, y in enumerate(ys):
            finish(slice(u * SCATTER_TILE, (u + 1) * SCATTER_TILE), y)

    all_fit = functools.reduce(jnp.minimum, [ok_ref[i, tile] for tile in tiles])
    lax.cond(all_fit > 0, windows, full)


def _scatter(x, rt, y, g2, fg, *, cap, final, tn, windows=None):
    b, t, d = x.shape
    smem = pl.BlockSpec(memory_space=pltpu.SMEM)
    windowed = windows is not None
    return pl.pallas_call(
        functools.partial(_scatter_kernel, cap=cap, final=final, windowed=windowed),
        grid=(b, t // tn),
        in_specs=([smem, smem] if windowed else []) + [
            pl.BlockSpec((1, tn, d), lambda i, j: (i, j, 0)),
            pl.BlockSpec((1, tn, ROUTE_ROWS), lambda i, j: (i, j, 0)),
            pl.BlockSpec((1, N_EXPERTS * cap, d), lambda i, j: (i, 0, 0)),
            pl.BlockSpec((1, 1, d), lambda i, j: (i, 0, 0)),
            pl.BlockSpec((1, d), lambda i, j: (0, 0))],
        out_specs=pl.BlockSpec((1, tn, d), lambda i, j: (i, j, 0)),
        out_shape=jax.ShapeDtypeStruct((b, t, d), F32),
        compiler_params=_cparams("arbitrary", "arbitrary"),
        name="moe_scatter",
    )(*(windows if windowed else ()), x, rt, y, g2, fg)


def _capacity(t):
    return CAPACITY_FACTOR * t // N_EXPERTS


def _dft_tables(n):
    k = np.arange(n, dtype=np.int64)
    ang = 2.0 * np.pi * ((k[:, None] * k[None, :]) % n).astype(np.float64) / n
    return np.cos(ang), np.sin(ang)


def _channel_dft():
    c, s = _dft_tables(FOURIER_GROUP_CH)
    eye = np.eye(FOURIER_GROUPS)
    return np.concatenate([np.kron(eye, c), np.kron(eye, s)], axis=1)


def _rope_tables(t):
    rows = t // GRID_W
    r, col = jnp.meshgrid(jnp.arange(rows), jnp.arange(GRID_W), indexing="ij")
    half = HEAD_DIM // 2
    inv = ROPE_THETA ** (-jnp.arange(0, half, 2, dtype=F32) / half)
    ang = jnp.concatenate([r.reshape(-1, 1).astype(F32) * inv, col.reshape(-1, 1).astype(F32) * inv], axis=-1)
    cos = jnp.repeat(jnp.cos(ang), 2, axis=1)
    sin = jnp.repeat(jnp.sin(ang), 2, axis=1) * jnp.tile(jnp.asarray([-1.0, 1.0], F32), half)
    return jnp.tile(cos, (1, HEAD_PAD)), jnp.tile(sin, (1, HEAD_PAD))


def kernel(x, c, ctx, c_ctx, w_ada, b_ada, norm1_g, w_in, q_norm_g, k_norm_g, sink, w_br_fourier, w_br_global,
           w_br_window, w_out, norm2_g, w_router, w_gate_e, w_up_e, w_down_e, final_g):
    b, t, d = x.shape
    l_ctx = ctx.shape[1]
    depth = w_ada.shape[0]
    cap_t, cap_c = _capacity(t), _capacity(l_ctx)

    cos_t, sin_t = _rope_tables(t)
    cos_c, sin_c = jnp.ones((l_ctx, LANES), F32), jnp.zeros((l_ctx, LANES), F32)
    head_avg = jnp.asarray(np.kron(np.eye(MXU_DIM // HEAD_DIM), np.full((HEAD_DIM, HEAD_DIM), 1.0 / HEAD_DIM)), BF16)
    cdft = jnp.asarray(_channel_dft(), F32)
    cn_t, sn_t = (jnp.asarray(a, F32) for a in _dft_tables(t))
    cn_c, sn_c = (jnp.asarray(a, F32) for a in _dft_tables(l_ctx))
    tri = jnp.asarray(np.triu(np.ones((LANES, LANES))), BF16)

    cvec = jnp.concatenate([c, c_ctx[None], jnp.zeros((MOD_ROWS - b - 1, d), F32)], axis=0)
    mods = _ada(cvec, w_ada, b_ada)

    xc = ctx
    for l in range(depth):
        need_ctx = l < depth - 1
        final = l == depth - 1
        lat = [mods[l, :b, i * d:(i + 1) * d].reshape(b, 1, d) for i in range(6)]
        cmod = [jnp.broadcast_to(mods[l, b, i * d:(i + 1) * d].reshape(1, 1, d), (b, 1, d)) for i in range(6)]
        w_mix = w_in[l, :, :MIX_W].astype(BF16)
        w_gate = w_in[l, :, MIX_W:].astype(BF16)
        n1, n2 = norm1_g[l].reshape(1, d), norm2_g[l].reshape(1, d)
        qn = jnp.tile(q_norm_g[l], G_HEADS).reshape(1, GQ_W)
        kn = jnp.tile(k_norm_g[l], G_KV).reshape(1, GKV_W)
        wbf, wbg, wbw = (w[l].astype(BF16) for w in (w_br_fourier, w_br_global, w_br_window))
        wout = w_out[l].astype(BF16)
        wr = jnp.pad(w_router[l], ((0, 0), (0, LANES - N_EXPERTS))).astype(BF16)
        fg = final_g.reshape(1, d)
        merge_w = (n1, n2, w_gate, wbf, wbg, wbw, wout, wr)

        cfcs, cqg, ckg, cvg, cqw, ckw, cvw = _inproj(xc, cmod[0], cmod[1], n1, w_mix, cos_c, sin_c, head_avg, cdft,
                                                     qn, kn, rope=False, tm=l_ctx, chunks=2)
        xgs, caps = [], []
        if need_ctx:
            cf_mix = _fourier(cfcs, cn_c, sn_c, tr=l_ctx)
            oc_g, oc_w = _ctx_attn(sink[l], cqg, ckg, cvg, cqw, ckw, cvw)
            xc1, hc2, caff = _merge(xc, cf_mix, oc_g, oc_w, *cmod[:5], *merge_w, tm=l_ctx, chunks=1)
            cpos, crt = _route(caff, tri, cap=cap_c, windows=False)
            xgs.append(_gather(cpos, hc2, cap=cap_c, group=4))
            caps.append(cap_c)

        fcs, qg, kg, vg, qw, kw, vw = _inproj(x, lat[0], lat[1], n1, w_mix, cos_t, sin_t, head_avg, cdft, qn, kn,
                                              rope=True, tm=1024, chunks=8)
        f_mix = _fourier(fcs, cn_t, sn_t, tr=512)
        o_g = _global_attn(qg, kg, vg, ckg, cvg, tq=256)
        o_w = _window_attn(sink[l], qw, kw, vw, ckw, cvw)
        x1, h2, aff = _merge(x, f_mix, o_g, o_w, *lat[:5], *merge_w, tm=1024, chunks=2)
        pos, rt, win, win_ok = _route(aff, tri, cap=cap_t, windows=True)
        xgs.insert(0, _gather_win(win, win_ok, pos, h2, cap=cap_t, tiles=2))
        caps.insert(0, cap_t)

        ys = _experts(xgs, caps, l, w_gate_e, w_up_e, w_down_e, nb=4)
        x = _scatter(x1, rt, ys[0], lat[5], fg, cap=cap_t, final=final, tn=1024, windows=(win, win_ok))
        if need_ctx:
            xc = _scatter(xc1, crt, ys[1], cmod[5], fg, cap=cap_c, final=False, tn=l_ctx)
    return x
```

```python
import functools

import numpy as np
import jax
import jax.numpy as jnp
from jax import lax
from jax.experimental import pallas as pl
from jax.experimental.pallas import tpu as pltpu

F32 = jnp.float32
BF16 = jnp.bfloat16

D_MODEL = 1024
HEAD_DIM = 64
GRID_W = 64
FOURIER_GROUPS = 4
FOURIER_GROUP_CH = 64
FOURIER_W = FOURIER_GROUPS * FOURIER_GROUP_CH
G_HEADS, G_KV = 8, 2
G_GROUP = G_HEADS // G_KV
W_HEADS, W_KV = 4, 2
W_GROUP = W_HEADS // W_KV
WINDOW = 128
Q_BLOCK = 128
N_BRANCH = 3
GQ_W = G_HEADS * HEAD_DIM
GKV_W = G_KV * HEAD_DIM
WQ_W = W_HEADS * HEAD_DIM
WKV_W = W_KV * HEAD_DIM
OFF_F = 0
OFF_GQ = OFF_F + FOURIER_W
OFF_GK = OFF_GQ + GQ_W
OFF_GV = OFF_GK + GKV_W
OFF_WQ = OFF_GV + GKV_W
OFF_WK = OFF_WQ + WQ_W
OFF_WV = OFF_WK + WKV_W
MIX_W = OFF_WV + WKV_W
GATE_W = N_BRANCH * D_MODEL
ROPE_THETA = 10000.0
N_EXPERTS = 16
CAPACITY_FACTOR = 2
EPS = 1e-6
NEG_INF = -1e30
LANES = 128
HEAD_PAD = LANES // HEAD_DIM
MXU_DIM = 256
MOD_ROWS = 16
SCATTER_TILE = 256
SCATTER_WINDOW = 64
ROUTE_ROWS = 128
WINDOW_TILE = 4
WINDOW_LOOKAHEAD = 2
GLOBAL_CHAINS_PER_KV = 1
VMEM_LIMIT = 56 * 1024 * 1024


def _cparams(*sem):
    return pltpu.CompilerParams(dimension_semantics=sem, vmem_limit_bytes=VMEM_LIMIT)


def _norm_mod(x, g, sh, sc):
    ms = jnp.mean(x * x, axis=-1, keepdims=True)
    return (x * lax.rsqrt(ms + EPS) * g) * (1.0 + sc) + sh


def _dot(a, b):
    return jnp.dot(a, b, preferred_element_type=F32)


def _dot_nt(a, b):
    return lax.dot_general(a, b, (((1,), (1,)), ((), ())), preferred_element_type=F32)


def _ada_kernel(c_ref, w_ref, b_ref, o_ref):
    c = c_ref[...]
    s = (c * jax.nn.sigmoid(c)).astype(BF16)
    o_ref[0] = _dot(s, w_ref[0].astype(BF16)) + b_ref[0]


def _ada(cvec, w_ada, b_ada):
    depth, d, n = w_ada.shape
    tn = 1536
    return pl.pallas_call(
        _ada_kernel,
        grid=(depth, n // tn),
        in_specs=[pl.BlockSpec((MOD_ROWS, d), lambda l, j: (0, 0)),
                  pl.BlockSpec((1, d, tn), lambda l, j: (l, 0, j)),
                  pl.BlockSpec((1, 1, tn), lambda l, j: (l, 0, j))],
        out_specs=pl.BlockSpec((1, MOD_ROWS, tn), lambda l, j: (l, 0, j)),
        out_shape=jax.ShapeDtypeStruct((depth, MOD_ROWS, n), F32),
        compiler_params=_cparams("arbitrary", "arbitrary"),
        name="ada",
    )(cvec, w_ada, b_ada.reshape(depth, 1, n))


def _rope(x, cos, sin_signed, even):
    outs = []
    for j in range(x.shape[1] // LANES):
        xb = x[:, j * LANES:(j + 1) * LANES]
        swap = jnp.where(even, pltpu.roll(xb, LANES - 1, 1), pltpu.roll(xb, 1, 1))
        outs.append(xb * cos + swap * sin_signed)
    return outs[0] if len(outs) == 1 else jnp.concatenate(outs, axis=1)


def _head_mean_square(z, bd):
    zz = (z * z).astype(BF16)
    blk = bd.shape[0]
    if z.shape[1] < blk:
        return _dot(zz, bd[:z.shape[1], :z.shape[1]])
    outs = [_dot(zz[:, j * blk:(j + 1) * blk], bd) for j in range(z.shape[1] // blk)]
    return outs[0] if len(outs) == 1 else jnp.concatenate(outs, axis=1)


def _pad_heads_f32(x, fill):
    blk = jnp.full((x.shape[0], LANES - HEAD_DIM), fill, F32)
    parts = []
    for h in range(x.shape[1] // HEAD_DIM):
        parts += [x[:, h * HEAD_DIM:(h + 1) * HEAD_DIM], blk]
    return jnp.concatenate(parts, axis=1)


def _pad_heads(x, fill):
    return _pad_heads_f32(x, fill).astype(BF16)


def _inproj_kernel(x_ref, sh_ref, sc_ref, g_ref, w_ref, cos_ref, sin_ref, bd_ref, cdft_ref, qn_ref, kn_ref,
                   fcs_ref, qg_ref, kg_ref, vg_ref, qw_ref, kw_ref, vw_ref, *, rope, chunks):
    tm = x_ref.shape[1]
    rows = [slice(c * (tm // chunks), (c + 1) * (tm // chunks)) for c in range(chunks)]
    bd = bd_ref[...]
    scale = HEAD_DIM ** -0.5

    def project(r):
        return _dot(_norm_mod(x_ref[0, r], g_ref[...], sh_ref[0], sc_ref[0]).astype(BF16), w_ref[...])

    zs = {0: project(rows[0])}
    for c, r in enumerate(rows):
        if c + 1 < chunks:
            zs[c + 1] = project(rows[c + 1])
        z = zs.pop(c)
        fcs_ref[0, r] = _dot(z[:, OFF_F:OFF_F + FOURIER_W].astype(BF16), cdft_ref[...].astype(BF16)).astype(BF16)
        q = z[:, OFF_GQ:OFF_GQ + GQ_W]
        k = z[:, OFF_GK:OFF_GK + GKV_W]
        q = q * lax.rsqrt(_head_mean_square(q, bd) + EPS) * qn_ref[...]
        k = k * lax.rsqrt(_head_mean_square(k, bd) + EPS) * kn_ref[...]
        qw = z[:, OFF_WQ:OFF_WQ + WQ_W]
        kw = z[:, OFF_WK:OFF_WK + WKV_W]
        if rope:
            cos, sin = cos_ref[r], sin_ref[r]
            even = (lax.broadcasted_iota(jnp.int32, cos.shape, 1) % 2) == 0
            q, k = _rope(q, cos, sin, even), _rope(k, cos, sin, even)
            qw, kw = _rope(qw, cos, sin, even), _rope(kw, cos, sin, even)
        qg_ref[0, r] = _pad_heads(q * scale, 0.0)
        kg_ref[0, r] = _pad_heads(k, 0.0)
        vg_ref[0, r] = _pad_heads(z[:, OFF_GV:OFF_GV + GKV_W], 1.0)
        qw_ref[0, r] = _pad_heads(qw * scale, 0.0)
        kw_ref[0, r] = _pad_heads(kw, 0.0)
        vw_ref[0, r] = _pad_heads(z[:, OFF_WV:OFF_WV + WKV_W], 1.0)


def _inproj(x, sh, sc, g, w_mix, cos_t, sin_t, bd, cdft, qn, kn, *, rope, tm, chunks):
    b, t, d = x.shape
    tok = lambda w: pl.BlockSpec((1, tm, w), lambda i, j: (i, j, 0))
    row = pl.BlockSpec((1, 1, d), lambda i, j: (i, 0, 0))
    const = lambda shape: pl.BlockSpec(shape, lambda i, j: (0,) * len(shape))
    widths = (2 * FOURIER_W,) + tuple(HEAD_PAD * w for w in (GQ_W, GKV_W, GKV_W, WQ_W, WKV_W, WKV_W))
    out_specs = [tok(w) for w in widths]
    out_shape = [jax.ShapeDtypeStruct((b, t, w), BF16) for w in widths]
    return pl.pallas_call(
        functools.partial(_inproj_kernel, rope=rope, chunks=chunks),
        grid=(b, t // tm),
        in_specs=[tok(d), row, row, const((1, d)), const((d, MIX_W)),
                  pl.BlockSpec((tm, LANES), lambda i, j: (j, 0)),
                  pl.BlockSpec((tm, LANES), lambda i, j: (j, 0)),
                  const((MXU_DIM, MXU_DIM)), const((FOURIER_W, 2 * FOURIER_W)),
                  const((1, GQ_W)), const((1, GKV_W))],
        out_specs=out_specs,
        out_shape=out_shape,
        compiler_params=_cparams("arbitrary", "arbitrary"),
        name="inproj_rope" if rope else "inproj_ctx",
    )(x, sh, sc, g, w_mix, cos_t, sin_t, bd, cdft, qn, kn)


def _fourier_kernel(cn_ref, sn_ref, fcs_ref, o_ref, cn_s, sn_s, *, scale):
    @pl.when(pl.program_id(1) == 0)
    def _():
        cn_s[...] = cn_ref[...].astype(BF16)
        sn_s[...] = sn_ref[...].astype(BF16)

    fcs = fcs_ref[0]
    re = _dot(cn_s[...], fcs[:, :FOURIER_W]) - _dot(sn_s[...], fcs[:, FOURIER_W:])
    o_ref[0] = (re * scale).astype(BF16)


def _fourier(fcs, cn, sn, *, tr):
    b, n, _ = fcs.shape
    scale = float((n * FOURIER_GROUP_CH) ** -0.5)
    return pl.pallas_call(
        functools.partial(_fourier_kernel, scale=scale),
        grid=(n // tr, b),
        in_specs=[pl.BlockSpec((tr, n), lambda r, i: (r, 0)),
                  pl.BlockSpec((tr, n), lambda r, i: (r, 0)),
                  pl.BlockSpec((1, n, 2 * FOURIER_W), lambda r, i: (i, 0, 0))],
        out_specs=pl.BlockSpec((1, tr, FOURIER_W), lambda r, i: (i, r, 0)),
        out_shape=jax.ShapeDtypeStruct((b, n, FOURIER_W), BF16),
        scratch_shapes=[pltpu.VMEM((tr, n), BF16), pltpu.VMEM((tr, n), BF16)],
        compiler_params=_cparams("arbitrary", "arbitrary"),
        name="fourier",
    )(cn, sn, fcs)


def _head(x, h):
    return x[:, h * LANES:(h + 1) * LANES]


def _stack_heads(q, first, count):
    return jnp.concatenate([_head(q, first + g) for g in range(count)], axis=0)


def _softmax_pv(parts, extra=None):
    m = functools.reduce(jnp.maximum, [jnp.max(s, axis=-1, keepdims=True) for s, _ in parts])
    if extra is not None:
        m = jnp.maximum(m, extra)
    acc = 0.0
    for s, v in parts:
        acc = acc + _dot(jnp.exp((s - m).astype(BF16)), v)
    den = acc[:, HEAD_DIM:HEAD_DIM + 1]
    if extra is not None:
        den = den + jnp.exp(extra - m)
    return acc[:, :HEAD_DIM] / den


def _unstack_heads(o_list, tq, count):
    cols = []
    for o in o_list:
        cols += [o[g * tq:(g + 1) * tq] for g in range(count)]
    return jnp.concatenate(cols, axis=1)


def _global_attn_kernel(q_ref, kl_ref, vl_ref, kc_ref, vc_ref, o_ref):
    q = q_ref[0]
    tq = q.shape[0]
    per_chain = G_GROUP // GLOBAL_CHAINS_PER_KV
    parts = []
    for kv in range(G_KV):
        for c in range(GLOBAL_CHAINS_PER_KV):
            qs = _stack_heads(q, kv * G_GROUP + c * per_chain, per_chain)
            parts.append([(_dot_nt(qs, _head(kc_ref[0], kv)), _head(vc_ref[0], kv)),
                          (_dot_nt(qs, _head(kl_ref[0], kv)), _head(vl_ref[0], kv))])
    outs = [_softmax_pv(p) for p in parts]
    o_ref[0] = _unstack_heads(outs, tq, per_chain).astype(BF16)


def _global_attn(q, kl, vl, kc, vc, *, tq):
    b, t, _ = q.shape
    l = kc.shape[1]
    full = lambda n: pl.BlockSpec((1, n, HEAD_PAD * GKV_W), lambda i, j: (i, 0, 0))
    return pl.pallas_call(
        _global_attn_kernel,
        grid=(b, t // tq),
        in_specs=[pl.BlockSpec((1, tq, HEAD_PAD * GQ_W), lambda i, j: (i, j, 0)), full(t), full(t), full(l), full(l)],
        out_specs=pl.BlockSpec((1, tq, GQ_W), lambda i, j: (i, j, 0)),
        out_shape=jax.ShapeDtypeStruct((b, t, GQ_W), BF16),
        compiler_params=_cparams("arbitrary", "arbitrary"),
        name="global_attn",
    )(q, kl, vl, kc, vc)


def _sink_column(sink_ref, kv, rows_per_head):
    r = lax.broadcasted_iota(jnp.int32, (W_GROUP * rows_per_head, 1), 0)
    col = jnp.full(r.shape, sink_ref[kv * W_GROUP], F32)
    for g in range(1, W_GROUP):
        col = jnp.where(r >= g * rows_per_head, sink_ref[kv * W_GROUP + g], col)
    return col


def _window_attn_kernel(sink_ref, q_ref, kp_ref, k0_ref, kn_ref, vp_ref, v0_ref, vn_ref, kc_ref, vc_ref, o_ref,
                        *, n_tiles):
    j = pl.program_id(1)
    q = q_ref[0]
    kb = jnp.concatenate([kp_ref[0], k0_ref[0], kn_ref[0]], axis=0)
    vb = jnp.concatenate([vp_ref[0], v0_ref[0], vn_ref[0]], axis=0)
    rows = W_GROUP * Q_BLOCK
    qpos = lax.broadcasted_iota(jnp.int32, (rows, 3 * Q_BLOCK), 0) % Q_BLOCK
    kpos = lax.broadcasted_iota(jnp.int32, (rows, 3 * Q_BLOCK), 1)
    dist = jnp.abs(qpos - kpos + Q_BLOCK)
    in_window = dist <= WINDOW
    outside = jnp.full(dist.shape, WINDOW + 1, jnp.int32)
    first_ok = jnp.where((kpos < Q_BLOCK) & (j == 0), outside, dist) <= WINDOW
    last_ok = jnp.where((kpos >= 2 * Q_BLOCK) & (j == n_tiles - 1), outside, dist) <= WINDOW
    def scores(n):
        i, kv = divmod(n, W_KV)
        valid = first_ok if i == 0 else (last_ok if i == WINDOW_TILE - 1 else in_window)
        band = slice(i * Q_BLOCK, (i + 3) * Q_BLOCK)
        qs = _stack_heads(q[i * Q_BLOCK:(i + 1) * Q_BLOCK], kv * W_GROUP, W_GROUP)
        s_b = jnp.where(valid, _dot_nt(qs, _head(kb, kv)[band]), NEG_INF)
        return [(_dot_nt(qs, _head(kc_ref[0], kv)), _head(vc_ref[0], kv)), (s_b, _head(vb, kv)[band])]

    sinks = [_sink_column(sink_ref, kv, Q_BLOCK) for kv in range(W_KV)]
    n_chains = WINDOW_TILE * W_KV
    pending = {n: scores(n) for n in range(min(WINDOW_LOOKAHEAD, n_chains))}
    outs = []
    for n in range(n_chains):
        if n + WINDOW_LOOKAHEAD < n_chains:
            pending[n + WINDOW_LOOKAHEAD] = scores(n + WINDOW_LOOKAHEAD)
        outs.append(_softmax_pv(pending.pop(n), sinks[n % W_KV]))
    blocks = [_unstack_heads(outs[i * W_KV:(i + 1) * W_KV], Q_BLOCK, W_GROUP) for i in range(WINDOW_TILE)]
    o_ref[0] = jnp.concatenate(blocks, axis=0).astype(BF16)


def _window_attn(sink, q, k, v, kc, vc):
    b, t, _ = q.shape
    l = kc.shape[1]
    nb = t // Q_BLOCK
    tile = WINDOW_TILE * Q_BLOCK
    kvw = HEAD_PAD * WKV_W
    edge = lambda f: pl.BlockSpec((1, Q_BLOCK, kvw), lambda i, j: (i, f(j), 0))
    prev = lambda j: jnp.maximum(j * WINDOW_TILE - 1, 0)
    nxt = lambda j: jnp.minimum((j + 1) * WINDOW_TILE, nb - 1)
    mid = pl.BlockSpec((1, tile, kvw), lambda i, j: (i, j, 0))
    full = pl.BlockSpec((1, l, kvw), lambda i, j: (i, 0, 0))
    return pl.pallas_call(
        functools.partial(_window_attn_kernel, n_tiles=t // tile),
        grid=(b, t // tile),
        in_specs=[pl.BlockSpec(memory_space=pltpu.SMEM),
                  pl.BlockSpec((1, tile, HEAD_PAD * WQ_W), lambda i, j: (i, j, 0)),
                  edge(prev), mid, edge(nxt), edge(prev), mid, edge(nxt), full, full],
        out_specs=pl.BlockSpec((1, tile, WQ_W), lambda i, j: (i, j, 0)),
        out_shape=jax.ShapeDtypeStruct((b, t, WQ_W), BF16),
        compiler_params=_cparams("arbitrary", "arbitrary"),
        name="window_attn",
    )(sink, q, k, k, k, v, v, v, kc, vc)


def _ctx_attn_kernel(sink_ref, qg_ref, kg_ref, vg_ref, qw_ref, kw_ref, vw_ref, og_ref, ow_ref):
    l = qg_ref.shape[1]
    qg, qw = qg_ref[0], qw_ref[0]
    g_parts = [[(_dot_nt(_stack_heads(qg, kv * G_GROUP, G_GROUP), _head(kg_ref[0], kv)), _head(vg_ref[0], kv))]
               for kv in range(G_KV)]
    w_parts = [[(_dot_nt(_stack_heads(qw, kv * W_GROUP, W_GROUP), _head(kw_ref[0], kv)), _head(vw_ref[0], kv))]
               for kv in range(W_KV)]
    og_ref[0] = _unstack_heads([_softmax_pv(p) for p in g_parts], l, G_GROUP).astype(BF16)
    outs = [_softmax_pv(p, _sink_column(sink_ref, kv, l)) for kv, p in enumerate(w_parts)]
    ow_ref[0] = _unstack_heads(outs, l, W_GROUP).astype(BF16)


def _ctx_attn(sink, qg, kg, vg, qw, kw, vw):
    b, l, _ = qg.shape
    spec = lambda w: pl.BlockSpec((1, l, w), lambda i: (i, 0, 0))
    padded = lambda w: spec(HEAD_PAD * w)
    return pl.pallas_call(
        _ctx_attn_kernel,
        grid=(b,),
        in_specs=[pl.BlockSpec(memory_space=pltpu.SMEM),
                  padded(GQ_W), padded(GKV_W), padded(GKV_W), padded(WQ_W), padded(WKV_W), padded(WKV_W)],
        out_specs=[spec(GQ_W), spec(WQ_W)],
        out_shape=[jax.ShapeDtypeStruct((b, l, GQ_W), BF16), jax.ShapeDtypeStruct((b, l, WQ_W), BF16)],
        compiler_params=_cparams("arbitrary"),
        name="ctx_attn",
    )(sink, qg, kg, vg, qw, kw, vw)


def _merge_kernel(x_ref, f_ref, og_ref, ow_ref, sh1_ref, sc1_ref, g1_ref, sh2_ref, sc2_ref,
                  n1_ref, n2_ref, wgate_ref, wbf_ref, wbg_ref, wbw_ref, wout_ref, wr_ref,
                  x1_ref, h2_ref, aff_ref, *, chunks):
    tm = x_ref.shape[1]
    rows = [slice(c * (tm // chunks), (c + 1) * (tm // chunks)) for c in range(chunks)]
    d = D_MODEL

    def project(r):
        h = _norm_mod(x_ref[0, r], n1_ref[...], sh1_ref[0], sc1_ref[0]).astype(BF16)
        return (_dot(h, wgate_ref[...]), _dot(f_ref[0, r], wbf_ref[...]),
                _dot(og_ref[0, r], wbg_ref[...]), _dot(ow_ref[0, r], wbw_ref[...]))

    heads = {0: project(rows[0])}
    for c, r in enumerate(rows):
        if c + 1 < chunks:
            heads[c + 1] = project(rows[c + 1])
        gate_logits, bf, bg, bw = heads.pop(c)
        gate = jax.nn.sigmoid(gate_logits)
        m = gate[:, 0:d] * bf + gate[:, d:2 * d] * bg + gate[:, 2 * d:3 * d] * bw
        x1 = x_ref[0, r] + g1_ref[0] * _dot(m.astype(BF16), wout_ref[...])
        x1_ref[0, r] = x1
        h2 = _norm_mod(x1, n2_ref[...], sh2_ref[0], sc2_ref[0]).astype(BF16)
        h2_ref[0, r] = h2
        logits = _dot(h2, wr_ref[...]).T[:N_EXPERTS]
        e = jnp.exp(logits - jnp.max(logits, axis=0, keepdims=True))
        aff_ref[0, :, r] = e / jnp.sum(e, axis=0, keepdims=True)


def _merge(x, f, og, ow, sh1, sc1, g1, sh2, sc2, n1, n2, wgate, wbf, wbg, wbw, wout, wr, *, tm, chunks):
    b, t, d = x.shape
    tok = lambda w: pl.BlockSpec((1, tm, w), lambda i, j: (i, j, 0))
    row = pl.BlockSpec((1, 1, d), lambda i, j: (i, 0, 0))
    const = lambda shape: pl.BlockSpec(shape, lambda i, j: (0,) * len(shape))
    return pl.pallas_call(
        functools.partial(_merge_kernel, chunks=chunks),
        grid=(b, t // tm),
        in_specs=[tok(d), tok(FOURIER_W), tok(GQ_W), tok(WQ_W), row, row, row, row, row,
                  const((1, d)), const((1, d)), const((d, GATE_W)), const((FOURIER_W, d)),
                  const((GQ_W, d)), const((WQ_W, d)), const((d, d)), const((d, LANES))],
        out_specs=[tok(d), tok(d), pl.BlockSpec((1, N_EXPERTS, tm), lambda i, j: (i, 0, j))],
        out_shape=[jax.ShapeDtypeStruct((b, t, d), F32), jax.ShapeDtypeStruct((b, t, d), BF16),
                   jax.ShapeDtypeStruct((b, N_EXPERTS, t), F32)],
        compiler_params=_cparams("arbitrary", "arbitrary"),
        name="merge",
    )(x, f, og, ow, sh1, sc1, g1, sh2, sc2, n1, n2, wgate, wbf, wbg, wbw, wout, wr)


def _cumsum_lanes(m, tri):
    e, n = m.shape
    nch = n // LANES
    stacked = jnp.concatenate([m[:, j * LANES:(j + 1) * LANES] for j in range(nch)], axis=0).astype(BF16)
    w = _dot(stacked, tri)
    outs, off = [], jnp.zeros((e, 1), F32)
    for j in range(nch):
        wj = w[j * e:(j + 1) * e]
        outs.append(wj + off)
        off = off + wj[:, LANES - 1:LANES]
    return jnp.concatenate(outs, axis=1)


def _route_kernel(aff_ref, tri_ref, pos_ref, rt_ref, *win_refs, cap, n_exp):
    aff = aff_ref[...]
    e, n = aff.shape
    thr_bits = jnp.zeros((e, 1), jnp.int32)
    for bit in range(30, -1, -1):
        cand = thr_bits | (1 << bit)
        cnt = jnp.sum(jnp.where(aff >= pltpu.bitcast(cand, F32), 1.0, 0.0), axis=1, keepdims=True)
        thr_bits = jnp.where(cnt >= cap, cand, thr_bits)
    ge = jnp.where(aff >= pltpu.bitcast(thr_bits, F32), 1.0, 0.0)
    gt = jnp.where(aff >= pltpu.bitcast(thr_bits + 1, F32), 1.0, 0.0)
    eq = ge - gt
    room = cap - jnp.sum(gt, axis=1, keepdims=True)
    tri = tri_ref[...]
    sel = gt + jnp.where(_cumsum_lanes(eq, tri) <= room, eq, 0.0)
    cums = _cumsum_lanes(sel, tri)
    pos = jnp.where(sel > 0.0, cums - 1.0, -1.0)
    pos_ref[...] = pos.astype(jnp.int32)
    weight = sel * aff
    pad = jnp.zeros((ROUTE_ROWS - 2 * n_exp, n), F32)
    for i in range(e // n_exp):
        rows = slice(i * n_exp, (i + 1) * n_exp)
        rt_ref[i] = jnp.concatenate([pos[rows], weight[rows], pad], axis=0).T
    if win_refs:
        win_ref, ok_ref = win_refs
        n_tiles = n // SCATTER_TILE
        ends = jnp.concatenate([cums[:, (k + 1) * SCATTER_TILE - 1:(k + 1) * SCATTER_TILE] for k in range(n_tiles)],
                               axis=1)
        starts = jnp.concatenate([jnp.zeros((e, 1), F32), ends[:, :n_tiles - 1]], axis=1)
        align = 16.0
        first = jnp.minimum(jnp.floor(starts / align) * align, float(cap - SCATTER_WINDOW))
        fits = jnp.where(ends - first <= SCATTER_WINDOW, 1.0, 0.0)
        win_ref[...] = first.astype(jnp.int32)
        ok = [jnp.min(fits[i * n_exp:(i + 1) * n_exp], axis=0, keepdims=True) for i in range(e // n_exp)]
        ok_ref[...] = jnp.concatenate(ok, axis=0).astype(jnp.int32)


def _route(aff, tri, *, cap, windows):
    b, e, n = aff.shape
    whole = lambda shape: pl.BlockSpec(shape, lambda i: (0,) * len(shape))
    out_shape = [jax.ShapeDtypeStruct((b * e, n), jnp.int32), jax.ShapeDtypeStruct((b, n, ROUTE_ROWS), F32)]
    if windows:
        n_tiles = n // SCATTER_TILE
        out_shape += [jax.ShapeDtypeStruct((b * e, n_tiles), jnp.int32), jax.ShapeDtypeStruct((b, n_tiles), jnp.int32)]
    outs = pl.pallas_call(
        functools.partial(_route_kernel, cap=cap, n_exp=e),
        grid=(1,),
        in_specs=[whole((b * e, n)), whole((LANES, LANES))],
        out_specs=[whole(s.shape) for s in out_shape],
        out_shape=out_shape,
        compiler_params=_cparams("arbitrary"),
        name="route",
    )(aff.reshape(b * e, n), tri)
    pos, rt = outs[0].reshape(b, e, n), outs[1]
    if not windows:
        return pos, rt
    win = outs[2].reshape(b, e, n_tiles).transpose(0, 2, 1)
    return pos, rt, win, outs[3]


def _gather_kernel(pos_ref, h_ref, o_ref, *, cap, group):
    j = pl.program_id(1)
    n = h_ref.shape[1]
    slot = lax.broadcasted_iota(jnp.int32, (cap, n), 0)
    sel = [jnp.where(pos_ref[0, pl.ds(j * group + g, 1), :] == slot, 1.0, 0.0).astype(BF16) for g in range(group)]
    o_ref[0] = _dot(jnp.concatenate(sel, axis=0), h_ref[0]).astype(BF16)


def _gather(pos, h, *, cap, group):
    b, e, n = pos.shape
    d = h.shape[2]
    return pl.pallas_call(
        functools.partial(_gather_kernel, cap=cap, group=group),
        grid=(b, e // group),
        in_specs=[pl.BlockSpec((1, e, n), lambda i, j: (i, 0, 0)), pl.BlockSpec((1, n, d), lambda i, j: (i, 0, 0))],
        out_specs=pl.BlockSpec((1, group * cap, d), lambda i, j: (i, j, 0)),
        out_shape=jax.ShapeDtypeStruct((b, e * cap, d), BF16),
        compiler_params=_cparams("arbitrary", "arbitrary"),
        name="moe_gather",
    )(pos, h)


def _gather_win_kernel(win_ref, ok_ref, pos_ref, h_ref, o_ref, *, tiles):
    i, j = pl.program_id(0), pl.program_id(1)
    cap = o_ref.shape[2]

    @pl.when(j == 0)
    def _():
        o_ref[...] = jnp.zeros_like(o_ref)

    kts = [j * tiles + u for u in range(tiles)]
    cols = [slice(u * SCATTER_TILE, (u + 1) * SCATTER_TILE) for u in range(tiles)]

    def windows():
        slot = lax.broadcasted_iota(jnp.int32, (SCATTER_WINDOW, SCATTER_TILE), 0)
        firsts, zs = [], []
        for kt, c in zip(kts, cols):
            pos = pos_ref[0, :, c]
            first = [win_ref[i, kt, e] for e in range(N_EXPERTS)]
            pick = [jnp.where(pos[e:e + 1] - first[e] == slot, 1.0, 0.0).astype(BF16) for e in range(N_EXPERTS)]
            zs.append(_dot(jnp.concatenate(pick, axis=0), h_ref[0, c]))
            firsts.append(first)
        for first, z in zip(firsts, zs):
            for e in range(N_EXPERTS):
                rows = pl.ds(pl.multiple_of(first[e], 16), SCATTER_WINDOW)
                o_ref[0, e, rows, :] += z[e * SCATTER_WINDOW:(e + 1) * SCATTER_WINDOW].astype(BF16)

    def full():
        slot = lax.broadcasted_iota(jnp.int32, (cap, SCATTER_TILE), 0)
        for c in cols:
            for e in range(N_EXPERTS):
                pick = jnp.where(pos_ref[0, e:e + 1, c] == slot, 1.0, 0.0).astype(BF16)
                o_ref[0, e] += _dot(pick, h_ref[0, c]).astype(BF16)

    all_fit = functools.reduce(jnp.minimum, [ok_ref[i, kt] for kt in kts])
    lax.cond(all_fit > 0, windows, full)


def _gather_win(win, ok, pos, h, *, cap, tiles):
    b, e, n = pos.shape
    d = h.shape[2]
    smem = pl.BlockSpec(memory_space=pltpu.SMEM)
    step = tiles * SCATTER_TILE
    out = pl.pallas_call(
        functools.partial(_gather_win_kernel, tiles=tiles),
        grid=(b, n // step),
        in_specs=[smem, smem,
                  pl.BlockSpec((1, e, step), lambda i, j: (i, 0, j)),
                  pl.BlockSpec((1, step, d), lambda i, j: (i, j, 0))],
        out_specs=pl.BlockSpec((1, e, cap, d), lambda i, j: (i, 0, 0, 0)),
        out_shape=jax.ShapeDtypeStruct((b, e, cap, d), BF16),
        compiler_params=_cparams("arbitrary", "arbitrary"),
        name="moe_gather_win",
    )(win, ok, pos, h)
    return out.reshape(b, e * cap, d)


def _expert_kernel(*refs, n_sets):
    x_refs, (wg_ref, wu_ref, wd_ref) = refs[:n_sets], refs[n_sets:n_sets + 3]
    o_refs = refs[n_sets + 3:2 * n_sets + 3]
    d = x_refs[0].shape[2]
    rows = [r.shape[0] * r.shape[1] for r in x_refs]
    xs = [r[...].reshape(n, d) for r, n in zip(x_refs, rows)]
    x = xs[0] if n_sets == 1 else jnp.concatenate(xs, axis=0)
    wg, wu, wd = (w[0, 0].astype(BF16) for w in (wg_ref, wu_ref, wd_ref))
    half = x.shape[0] // 2
    halves = [x[:half], x[half:]]
    up = [(_dot(xh, wg), _dot(xh, wu)) for xh in halves]
    y = jnp.concatenate([_dot((a * jax.nn.sigmoid(a) * u).astype(BF16), wd).astype(BF16) for a, u in up], axis=0)
    start = 0
    for o_ref, n in zip(o_refs, rows):
        o_ref[...] = y[start:start + n].reshape(o_ref.shape)
        start += n


def _experts(xgs, caps, layer, wg, wu, wd, *, nb):
    b, _, d = xgs[0].shape
    _, e, _, f = wg.shape
    wspec = lambda r, c: pl.BlockSpec((1, 1, r, c), lambda i, j: (layer, i, 0, 0))
    xspecs = [pl.BlockSpec((nb, cap, d), lambda i, j: (j, i, 0)) for cap in caps]
    return pl.pallas_call(
        functools.partial(_expert_kernel, n_sets=len(xgs)),
        grid=(e, b // nb),
        in_specs=xspecs + [wspec(d, f), wspec(d, f), wspec(f, d)],
        out_specs=xspecs,
        out_shape=[jax.ShapeDtypeStruct(xg.shape, BF16) for xg in xgs],
        compiler_params=_cparams("arbitrary", "arbitrary"),
        name="moe_experts",
    )(*xgs, wg, wu, wd)


def _combine_matrix(rt, first, width):
    slot = lax.broadcasted_iota(jnp.int32, (rt.shape[0], width), 1).astype(F32)
    cols = [jnp.where(rt[:, e:e + 1] - first[e] == slot, rt[:, N_EXPERTS + e:N_EXPERTS + e + 1], 0.0).astype(BF16)
            for e in range(N_EXPERTS)]
    return jnp.concatenate(cols, axis=1)


def _scatter_kernel(*refs, cap, final, windowed):
    if windowed:
        win_ref, ok_ref, x_ref, rt_ref, y_ref, g2_ref, fg_ref, o_ref = refs
    else:
        x_ref, rt_ref, y_ref, g2_ref, fg_ref, o_ref = refs
    i, j = pl.program_id(0), pl.program_id(1)
    tn = x_ref.shape[1]

    def finish(rows, y):
        x = x_ref[0, rows] + g2_ref[0] * y
        if final:
            ms = jnp.mean(x * x, axis=-1, keepdims=True)
            x = x * lax.rsqrt(ms + EPS) * fg_ref[...]
        o_ref[0, rows] = x

    def full():
        rows = slice(0, tn)
        finish(rows, _dot(_combine_matrix(rt_ref[0], [0.0] * N_EXPERTS, cap), y_ref[0]))

    if not windowed:
        full()
        return
    subs = tn // SCATTER_TILE
    tiles = [j * subs + u for u in range(subs)]

    def windows():
        def contract(u):
            rt = rt_ref[0, u * SCATTER_TILE:(u + 1) * SCATTER_TILE]
            first = [win_ref[i, tiles[u], e] for e in range(N_EXPERTS)]
            picked = [y_ref[0, pl.ds(pl.multiple_of(e * cap + first[e], 16), SCATTER_WINDOW), :]
                      for e in range(N_EXPERTS)]
            a = _combine_matrix(rt, [f.astype(F32) for f in first], SCATTER_WINDOW)
            return _dot(a, jnp.concatenate(picked, axis=0))

        ys = {0: contract(0)}
        for u in range(subs):
            if u + 1 < subs:
                ys[u + 1] = contract(u + 1)
            finish(slice(u * SCATTER_TILE, (u + 1) * SCATTER_TILE), ys.pop(u))

    all_fit = functools.reduce(jnp.minimum, [ok_ref[i, tile] for tile in tiles])
    lax.cond(all_fit > 0, windows, full)


def _scatter(x, rt, y, g2, fg, *, cap, final, tn, windows=None):
    b, t, d = x.shape
    smem = pl.BlockSpec(memory_space=pltpu.SMEM)
    windowed = windows is not None
    return pl.pallas_call(
        functools.partial(_scatter_kernel, cap=cap, final=final, windowed=windowed),
        grid=(b, t // tn),
        in_specs=([smem, smem] if windowed else []) + [
            pl.BlockSpec((1, tn, d), lambda i, j: (i, j, 0)),
            pl.BlockSpec((1, tn, ROUTE_ROWS), lambda i, j: (i, j, 0)),
            pl.BlockSpec((1, N_EXPERTS * cap, d), lambda i, j: (i, 0, 0)),
            pl.BlockSpec((1, 1, d), lambda i, j: (i, 0, 0)),
            pl.BlockSpec((1, d), lambda i, j: (0, 0))],
        out_specs=pl.BlockSpec((1, tn, d), lambda i, j: (i, j, 0)),
        out_shape=jax.ShapeDtypeStruct((b, t, d), F32),
        compiler_params=_cparams("arbitrary", "arbitrary"),
        name="moe_scatter",
    )(*(windows if windowed else ()), x, rt, y, g2, fg)


def _capacity(t):
    return CAPACITY_FACTOR * t // N_EXPERTS


def _dft_tables(n):
    k = np.arange(n, dtype=np.int64)
    ang = 2.0 * np.pi * ((k[:, None] * k[None, :]) % n).astype(np.float64) / n
    return np.cos(ang), np.sin(ang)


def _channel_dft():
    c, s = _dft_tables(FOURIER_GROUP_CH)
    eye = np.eye(FOURIER_GROUPS)
    return np.concatenate([np.kron(eye, c), np.kron(eye, s)], axis=1)


def _rope_tables(t):
    rows = t // GRID_W
    r, col = jnp.meshgrid(jnp.arange(rows), jnp.arange(GRID_W), indexing="ij")
    half = HEAD_DIM // 2
    inv = ROPE_THETA ** (-jnp.arange(0, half, 2, dtype=F32) / half)
    ang = jnp.concatenate([r.reshape(-1, 1).astype(F32) * inv, col.reshape(-1, 1).astype(F32) * inv], axis=-1)
    cos = jnp.repeat(jnp.cos(ang), 2, axis=1)
    sin = jnp.repeat(jnp.sin(ang), 2, axis=1) * jnp.tile(jnp.asarray([-1.0, 1.0], F32), half)
    return jnp.tile(cos, (1, HEAD_PAD)), jnp.tile(sin, (1, HEAD_PAD))


def kernel(x, c, ctx, c_ctx, w_ada, b_ada, norm1_g, w_in, q_norm_g, k_norm_g, sink, w_br_fourier, w_br_global,
           w_br_window, w_out, norm2_g, w_router, w_gate_e, w_up_e, w_down_e, final_g):
    b, t, d = x.shape
    l_ctx = ctx.shape[1]
    depth = w_ada.shape[0]
    cap_t, cap_c = _capacity(t), _capacity(l_ctx)

    cos_t, sin_t = _rope_tables(t)
    cos_c, sin_c = jnp.ones((l_ctx, LANES), F32), jnp.zeros((l_ctx, LANES), F32)
    head_avg = jnp.asarray(np.kron(np.eye(MXU_DIM // HEAD_DIM), np.full((HEAD_DIM, HEAD_DIM), 1.0 / HEAD_DIM)), BF16)
    cdft = jnp.asarray(_channel_dft(), F32)
    cn_t, sn_t = (jnp.asarray(a, F32) for a in _dft_tables(t))
    cn_c, sn_c = (jnp.asarray(a, F32) for a in _dft_tables(l_ctx))
    tri = jnp.asarray(np.triu(np.ones((LANES, LANES))), BF16)

    cvec = jnp.concatenate([c, c_ctx[None], jnp.zeros((MOD_ROWS - b - 1, d), F32)], axis=0)
    mods = _ada(cvec, w_ada, b_ada)

    xc = ctx
    for l in range(depth):
        need_ctx = l < depth - 1
        final = l == depth - 1
        lat = [mods[l, :b, i * d:(i + 1) * d].reshape(b, 1, d) for i in range(6)]
        cmod = [jnp.broadcast_to(mods[l, b, i * d:(i + 1) * d].reshape(1, 1, d), (b, 1, d)) for i in range(6)]
        w_mix = w_in[l, :, :MIX_W].astype(BF16)
        w_gate = w_in[l, :, MIX_W:].astype(BF16)
        n1, n2 = norm1_g[l].reshape(1, d), norm2_g[l].reshape(1, d)
        qn = jnp.tile(q_norm_g[l], G_HEADS).reshape(1, GQ_W)
        kn = jnp.tile(k_norm_g[l], G_KV).reshape(1, GKV_W)
        wbf, wbg, wbw = (w[l].astype(BF16) for w in (w_br_fourier, w_br_global, w_br_window))
        wout = w_out[l].astype(BF16)
        wr = jnp.pad(w_router[l], ((0, 0), (0, LANES - N_EXPERTS))).astype(BF16)
        fg = final_g.reshape(1, d)
        merge_w = (n1, n2, w_gate, wbf, wbg, wbw, wout, wr)

        cfcs, cqg, ckg, cvg, cqw, ckw, cvw = _inproj(xc, cmod[0], cmod[1], n1, w_mix, cos_c, sin_c, head_avg, cdft,
                                                     qn, kn, rope=False, tm=l_ctx, chunks=2)
        xgs, caps = [], []
        if need_ctx:
            cf_mix = _fourier(cfcs, cn_c, sn_c, tr=l_ctx)
            oc_g, oc_w = _ctx_attn(sink[l], cqg, ckg, cvg, cqw, ckw, cvw)
            xc1, hc2, caff = _merge(xc, cf_mix, oc_g, oc_w, *cmod[:5], *merge_w, tm=l_ctx, chunks=1)
            cpos, crt = _route(caff, tri, cap=cap_c, windows=False)
            xgs.append(_gather(cpos, hc2, cap=cap_c, group=4))
            caps.append(cap_c)

        fcs, qg, kg, vg, qw, kw, vw = _inproj(x, lat[0], lat[1], n1, w_mix, cos_t, sin_t, head_avg, cdft, qn, kn,
                                              rope=True, tm=1024, chunks=8)
        f_mix = _fourier(fcs, cn_t, sn_t, tr=1024)
        o_g = _global_attn(qg, kg, vg, ckg, cvg, tq=256)
        o_w = _window_attn(sink[l], qw, kw, vw, ckw, cvw)
        x1, h2, aff = _merge(x, f_mix, o_g, o_w, *lat[:5], *merge_w, tm=1024, chunks=2)
        pos, rt, win, win_ok = _route(aff, tri, cap=cap_t, windows=True)
        xgs.insert(0, _gather_win(win, win_ok, pos, h2, cap=cap_t, tiles=4))
        caps.insert(0, cap_t)

        ys = _experts(xgs, caps, l, w_gate_e, w_up_e, w_down_e, nb=4)
        x = _scatter(x1, rt, ys[0], lat[5], fg, cap=cap_t, final=final, tn=1024, windows=(win, win_ok))
        if need_ctx:
            xc = _scatter(xc1, crt, ys[1], cmod[5], fg, cap=cap_c, final=False, tn=l_ctx)
    return x
```

```python
import functools

import numpy as np
import jax
import jax.numpy as jnp
from jax import lax
from jax.experimental import pallas as pl
from jax.experimental.pallas import tpu as pltpu

F32 = jnp.float32
BF16 = jnp.bfloat16

D_MODEL = 1024
HEAD_DIM = 64
GRID_W = 64
FOURIER_GROUPS = 4
FOURIER_GROUP_CH = 64
FOURIER_W = FOURIER_GROUPS * FOURIER_GROUP_CH
G_HEADS, G_KV = 8, 2
G_GROUP = G_HEADS // G_KV
W_HEADS, W_KV = 4, 2
W_GROUP = W_HEADS // W_KV
WINDOW = 128
Q_BLOCK = 128
N_BRANCH = 3
GQ_W = G_HEADS * HEAD_DIM
GKV_W = G_KV * HEAD_DIM
WQ_W = W_HEADS * HEAD_DIM
WKV_W = W_KV * HEAD_DIM
OFF_F = 0
OFF_GQ = OFF_F + FOURIER_W
OFF_GK = OFF_GQ + GQ_W
OFF_GV = OFF_GK + GKV_W
OFF_WQ = OFF_GV + GKV_W
OFF_WK = OFF_WQ + WQ_W
OFF_WV = OFF_WK + WKV_W
MIX_W = OFF_WV + WKV_W
GATE_W = N_BRANCH * D_MODEL
ROPE_THETA = 10000.0
N_EXPERTS = 16
CAPACITY_FACTOR = 2
EPS = 1e-6
NEG_INF = -1e30
LANES = 128
HEAD_PAD = LANES // HEAD_DIM
MXU_DIM = 256
MOD_ROWS = 16
SCATTER_TILE = 256
SCATTER_WINDOW = 64
ROUTE_ROWS = 128
WINDOW_TILE = 8
WINDOW_LOOKAHEAD = 2
GLOBAL_CHAINS_PER_KV = 1
VMEM_LIMIT = 56 * 1024 * 1024


def _cparams(*sem):
    return pltpu.CompilerParams(dimension_semantics=sem, vmem_limit_bytes=VMEM_LIMIT)


def _norm_mod(x, g, sh, sc):
    ms = jnp.mean(x * x, axis=-1, keepdims=True)
    return (x * lax.rsqrt(ms + EPS) * g) * (1.0 + sc) + sh


def _dot(a, b):
    return jnp.dot(a, b, preferred_element_type=F32)


def _dot_nt(a, b):
    return lax.dot_general(a, b, (((1,), (1,)), ((), ())), preferred_element_type=F32)


def _ada_kernel(c_ref, w_ref, b_ref, o_ref):
    c = c_ref[...]
    s = (c * jax.nn.sigmoid(c)).astype(BF16)
    o_ref[0] = _dot(s, w_ref[0].astype(BF16)) + b_ref[0]


def _ada(cvec, w_ada, b_ada):
    depth, d, n = w_ada.shape
    tn = 1536
    return pl.pallas_call(
        _ada_kernel,
        grid=(depth, n // tn),
        in_specs=[pl.BlockSpec((MOD_ROWS, d), lambda l, j: (0, 0)),
                  pl.BlockSpec((1, d, tn), lambda l, j: (l, 0, j)),
                  pl.BlockSpec((1, 1, tn), lambda l, j: (l, 0, j))],
        out_specs=pl.BlockSpec((1, MOD_ROWS, tn), lambda l, j: (l, 0, j)),
        out_shape=jax.ShapeDtypeStruct((depth, MOD_ROWS, n), F32),
        compiler_params=_cparams("arbitrary", "arbitrary"),
        name="ada",
    )(cvec, w_ada, b_ada.reshape(depth, 1, n))


def _rope(x, cos, sin_signed, even):
    outs = []
    for j in range(x.shape[1] // LANES):
        xb = x[:, j * LANES:(j + 1) * LANES]
        swap = jnp.where(even, pltpu.roll(xb, LANES - 1, 1), pltpu.roll(xb, 1, 1))
        outs.append(xb * cos + swap * sin_signed)
    return outs[0] if len(outs) == 1 else jnp.concatenate(outs, axis=1)


def _head_mean_square(z, bd):
    zz = (z * z).astype(BF16)
    blk = bd.shape[0]
    if z.shape[1] < blk:
        return _dot(zz, bd[:z.shape[1], :z.shape[1]])
    outs = [_dot(zz[:, j * blk:(j + 1) * blk], bd) for j in range(z.shape[1] // blk)]
    return outs[0] if len(outs) == 1 else jnp.concatenate(outs, axis=1)


def _pad_heads_f32(x, fill):
    blk = jnp.full((x.shape[0], LANES - HEAD_DIM), fill, F32)
    parts = []
    for h in range(x.shape[1] // HEAD_DIM):
        parts += [x[:, h * HEAD_DIM:(h + 1) * HEAD_DIM], blk]
    return jnp.concatenate(parts, axis=1)


def _pad_heads(x, fill):
    return _pad_heads_f32(x, fill).astype(BF16)


def _inproj_kernel(x_ref, sh_ref, sc_ref, g_ref, w_ref, cos_ref, sin_ref, bd_ref, cdft_ref, qn_ref, kn_ref,
                   fcs_ref, qg_ref, kg_ref, vg_ref, qw_ref, kw_ref, vw_ref, *, rope, chunks):
    tm = x_ref.shape[1]
    rows = [slice(c * (tm // chunks), (c + 1) * (tm // chunks)) for c in range(chunks)]
    bd = bd_ref[...]
    scale = HEAD_DIM ** -0.5

    def project(r):
        return _dot(_norm_mod(x_ref[0, r], g_ref[...], sh_ref[0], sc_ref[0]).astype(BF16), w_ref[...])

    zs = {0: project(rows[0])}
    for c, r in enumerate(rows):
        if c + 1 < chunks:
            zs[c + 1] = project(rows[c + 1])
        z = zs.pop(c)
        fcs_ref[0, r] = _dot(z[:, OFF_F:OFF_F + FOURIER_W].astype(BF16), cdft_ref[...].astype(BF16)).astype(BF16)
        q = z[:, OFF_GQ:OFF_GQ + GQ_W]
        k = z[:, OFF_GK:OFF_GK + GKV_W]
        q = q * lax.rsqrt(_head_mean_square(q, bd) + EPS) * qn_ref[...]
        k = k * lax.rsqrt(_head_mean_square(k, bd) + EPS) * kn_ref[...]
        qw = z[:, OFF_WQ:OFF_WQ + WQ_W]
        kw = z[:, OFF_WK:OFF_WK + WKV_W]
        if rope:
            cos, sin = cos_ref[r], sin_ref[r]
            even = (lax.broadcasted_iota(jnp.int32, cos.shape, 1) % 2) == 0
            q, k = _rope(q, cos, sin, even), _rope(k, cos, sin, even)
            qw, kw = _rope(qw, cos, sin, even), _rope(kw, cos, sin, even)
        qg_ref[0, r] = _pad_heads(q * scale, 0.0)
        kg_ref[0, r] = _pad_heads(k, 0.0)
        vg_ref[0, r] = _pad_heads(z[:, OFF_GV:OFF_GV + GKV_W], 1.0)
        qw_ref[0, r] = _pad_heads(qw * scale, 0.0)
        kw_ref[0, r] = _pad_heads(kw, 0.0)
        vw_ref[0, r] = _pad_heads(z[:, OFF_WV:OFF_WV + WKV_W], 1.0)


def _inproj(x, sh, sc, g, w_mix, cos_t, sin_t, bd, cdft, qn, kn, *, rope, tm, chunks):
    b, t, d = x.shape
    tok = lambda w: pl.BlockSpec((1, tm, w), lambda i, j: (i, j, 0))
    row = pl.BlockSpec((1, 1, d), lambda i, j: (i, 0, 0))
    const = lambda shape: pl.BlockSpec(shape, lambda i, j: (0,) * len(shape))
    widths = (2 * FOURIER_W,) + tuple(HEAD_PAD * w for w in (GQ_W, GKV_W, GKV_W, WQ_W, WKV_W, WKV_W))
    out_specs = [tok(w) for w in widths]
    out_shape = [jax.ShapeDtypeStruct((b, t, w), BF16) for w in widths]
    return pl.pallas_call(
        functools.partial(_inproj_kernel, rope=rope, chunks=chunks),
        grid=(b, t // tm),
        in_specs=[tok(d), row, row, const((1, d)), const((d, MIX_W)),
                  pl.BlockSpec((tm, LANES), lambda i, j: (j, 0)),
                  pl.BlockSpec((tm, LANES), lambda i, j: (j, 0)),
                  const((MXU_DIM, MXU_DIM)), const((FOURIER_W, 2 * FOURIER_W)),
                  const((1, GQ_W)), const((1, GKV_W))],
        out_specs=out_specs,
        out_shape=out_shape,
        compiler_params=_cparams("arbitrary", "arbitrary"),
        name="inproj_rope" if rope else "inproj_ctx",
    )(x, sh, sc, g, w_mix, cos_t, sin_t, bd, cdft, qn, kn)


def _fourier_kernel(cn_ref, sn_ref, fcs_ref, o_ref, cn_s, sn_s, *, scale):
    @pl.when(pl.program_id(1) == 0)
    def _():
        cn_s[...] = cn_ref[...].astype(BF16)
        sn_s[...] = sn_ref[...].astype(BF16)

    fcs = fcs_ref[0]
    re = _dot(cn_s[...], fcs[:, :FOURIER_W]) - _dot(sn_s[...], fcs[:, FOURIER_W:])
    o_ref[0] = (re * scale).astype(BF16)


def _fourier(fcs, cn, sn, *, tr):
    b, n, _ = fcs.shape
    scale = float((n * FOURIER_GROUP_CH) ** -0.5)
    return pl.pallas_call(
        functools.partial(_fourier_kernel, scale=scale),
        grid=(n // tr, b),
        in_specs=[pl.BlockSpec((tr, n), lambda r, i: (r, 0)),
                  pl.BlockSpec((tr, n), lambda r, i: (r, 0)),
                  pl.BlockSpec((1, n, 2 * FOURIER_W), lambda r, i: (i, 0, 0))],
        out_specs=pl.BlockSpec((1, tr, FOURIER_W), lambda r, i: (i, r, 0)),
        out_shape=jax.ShapeDtypeStruct((b, n, FOURIER_W), BF16),
        scratch_shapes=[pltpu.VMEM((tr, n), BF16), pltpu.VMEM((tr, n), BF16)],
        compiler_params=_cparams("arbitrary", "arbitrary"),
        name="fourier",
    )(cn, sn, fcs)


def _head(x, h):
    return x[:, h * LANES:(h + 1) * LANES]


def _stack_heads(q, first, count):
    return jnp.concatenate([_head(q, first + g) for g in range(count)], axis=0)


def _softmax_pv(parts, extra=None):
    m = functools.reduce(jnp.maximum, [jnp.max(s, axis=-1, keepdims=True) for s, _ in parts])
    if extra is not None:
        m = jnp.maximum(m, extra)
    acc = 0.0
    for s, v in parts:
        acc = acc + _dot(jnp.exp((s - m).astype(BF16)), v)
    den = acc[:, HEAD_DIM:HEAD_DIM + 1]
    if extra is not None:
        den = den + jnp.exp(extra - m)
    return acc[:, :HEAD_DIM] / den


def _unstack_heads(o_list, tq, count):
    cols = []
    for o in o_list:
        cols += [o[g * tq:(g + 1) * tq] for g in range(count)]
    return jnp.concatenate(cols, axis=1)


def _global_attn_kernel(q_ref, kl_ref, vl_ref, kc_ref, vc_ref, o_ref):
    q = q_ref[0]
    tq = q.shape[0]
    per_chain = G_GROUP // GLOBAL_CHAINS_PER_KV
    parts = []
    for kv in range(G_KV):
        for c in range(GLOBAL_CHAINS_PER_KV):
            qs = _stack_heads(q, kv * G_GROUP + c * per_chain, per_chain)
            parts.append([(_dot_nt(qs, _head(kc_ref[0], kv)), _head(vc_ref[0], kv)),
                          (_dot_nt(qs, _head(kl_ref[0], kv)), _head(vl_ref[0], kv))])
    outs = [_softmax_pv(p) for p in parts]
    o_ref[0] = _unstack_heads(outs, tq, per_chain).astype(BF16)


def _global_attn(q, kl, vl, kc, vc, *, tq):
    b, t, _ = q.shape
    l = kc.shape[1]
    full = lambda n: pl.BlockSpec((1, n, HEAD_PAD * GKV_W), lambda i, j: (i, 0, 0))
    return pl.pallas_call(
        _global_attn_kernel,
        grid=(b, t // tq),
        in_specs=[pl.BlockSpec((1, tq, HEAD_PAD * GQ_W), lambda i, j: (i, j, 0)), full(t), full(t), full(l), full(l)],
        out_specs=pl.BlockSpec((1, tq, GQ_W), lambda i, j: (i, j, 0)),
        out_shape=jax.ShapeDtypeStruct((b, t, GQ_W), BF16),
        compiler_params=_cparams("arbitrary", "arbitrary"),
        name="global_attn",
    )(q, kl, vl, kc, vc)


def _sink_column(sink_ref, kv, rows_per_head):
    r = lax.broadcasted_iota(jnp.int32, (W_GROUP * rows_per_head, 1), 0)
    col = jnp.full(r.shape, sink_ref[kv * W_GROUP], F32)
    for g in range(1, W_GROUP):
        col = jnp.where(r >= g * rows_per_head, sink_ref[kv * W_GROUP + g], col)
    return col


def _window_attn_kernel(sink_ref, q_ref, kp_ref, k0_ref, kn_ref, vp_ref, v0_ref, vn_ref, kc_ref, vc_ref, o_ref,
                        *, n_tiles):
    j = pl.program_id(1)
    q = q_ref[0]
    kb = jnp.concatenate([kp_ref[0], k0_ref[0], kn_ref[0]], axis=0)
    vb = jnp.concatenate([vp_ref[0], v0_ref[0], vn_ref[0]], axis=0)
    rows = W_GROUP * Q_BLOCK
    qpos = lax.broadcasted_iota(jnp.int32, (rows, 3 * Q_BLOCK), 0) % Q_BLOCK
    kpos = lax.broadcasted_iota(jnp.int32, (rows, 3 * Q_BLOCK), 1)
    dist = jnp.abs(qpos - kpos + Q_BLOCK)
    in_window = dist <= WINDOW
    outside = jnp.full(dist.shape, WINDOW + 1, jnp.int32)
    first_ok = jnp.where((kpos < Q_BLOCK) & (j == 0), outside, dist) <= WINDOW
    last_ok = jnp.where((kpos >= 2 * Q_BLOCK) & (j == n_tiles - 1), outside, dist) <= WINDOW
    def scores(n):
        i, kv = divmod(n, W_KV)
        valid = first_ok if i == 0 else (last_ok if i == WINDOW_TILE - 1 else in_window)
        band = slice(i * Q_BLOCK, (i + 3) * Q_BLOCK)
        qs = _stack_heads(q[i * Q_BLOCK:(i + 1) * Q_BLOCK], kv * W_GROUP, W_GROUP)
        s_b = jnp.where(valid, _dot_nt(qs, _head(kb, kv)[band]), NEG_INF)
        return [(_dot_nt(qs, _head(kc_ref[0], kv)), _head(vc_ref[0], kv)), (s_b, _head(vb, kv)[band])]

    sinks = [_sink_column(sink_ref, kv, Q_BLOCK) for kv in range(W_KV)]
    n_chains = WINDOW_TILE * W_KV
    pending = {n: scores(n) for n in range(min(WINDOW_LOOKAHEAD, n_chains))}
    outs = []
    for n in range(n_chains):
        if n + WINDOW_LOOKAHEAD < n_chains:
            pending[n + WINDOW_LOOKAHEAD] = scores(n + WINDOW_LOOKAHEAD)
        outs.append(_softmax_pv(pending.pop(n), sinks[n % W_KV]))
    blocks = [_unstack_heads(outs[i * W_KV:(i + 1) * W_KV], Q_BLOCK, W_GROUP) for i in range(WINDOW_TILE)]
    o_ref[0] = jnp.concatenate(blocks, axis=0).astype(BF16)


def _window_attn(sink, q, k, v, kc, vc):
    b, t, _ = q.shape
    l = kc.shape[1]
    nb = t // Q_BLOCK
    tile = WINDOW_TILE * Q_BLOCK
    kvw = HEAD_PAD * WKV_W
    edge = lambda f: pl.BlockSpec((1, Q_BLOCK, kvw), lambda i, j: (i, f(j), 0))
    prev = lambda j: jnp.maximum(j * WINDOW_TILE - 1, 0)
    nxt = lambda j: jnp.minimum((j + 1) * WINDOW_TILE, nb - 1)
    mid = pl.BlockSpec((1, tile, kvw), lambda i, j: (i, j, 0))
    full = pl.BlockSpec((1, l, kvw), lambda i, j: (i, 0, 0))
    return pl.pallas_call(
        functools.partial(_window_attn_kernel, n_tiles=t // tile),
        grid=(b, t // tile),
        in_specs=[pl.BlockSpec(memory_space=pltpu.SMEM),
                  pl.BlockSpec((1, tile, HEAD_PAD * WQ_W), lambda i, j: (i, j, 0)),
                  edge(prev), mid, edge(nxt), edge(prev), mid, edge(nxt), full, full],
        out_specs=pl.BlockSpec((1, tile, WQ_W), lambda i, j: (i, j, 0)),
        out_shape=jax.ShapeDtypeStruct((b, t, WQ_W), BF16),
        compiler_params=_cparams("arbitrary", "arbitrary"),
        name="window_attn",
    )(sink, q, k, k, k, v, v, v, kc, vc)


def _ctx_attn_kernel(sink_ref, qg_ref, kg_ref, vg_ref, qw_ref, kw_ref, vw_ref, og_ref, ow_ref):
    l = qg_ref.shape[1]
    qg, qw = qg_ref[0], qw_ref[0]
    g_parts = [[(_dot_nt(_stack_heads(qg, kv * G_GROUP, G_GROUP), _head(kg_ref[0], kv)), _head(vg_ref[0], kv))]
               for kv in range(G_KV)]
    w_parts = [[(_dot_nt(_stack_heads(qw, kv * W_GROUP, W_GROUP), _head(kw_ref[0], kv)), _head(vw_ref[0], kv))]
               for kv in range(W_KV)]
    og_ref[0] = _unstack_heads([_softmax_pv(p) for p in g_parts], l, G_GROUP).astype(BF16)
    outs = [_softmax_pv(p, _sink_column(sink_ref, kv, l)) for kv, p in enumerate(w_parts)]
    ow_ref[0] = _unstack_heads(outs, l, W_GROUP).astype(BF16)


def _ctx_attn(sink, qg, kg, vg, qw, kw, vw):
    b, l, _ = qg.shape
    spec = lambda w: pl.BlockSpec((1, l, w), lambda i: (i, 0, 0))
    padded = lambda w: spec(HEAD_PAD * w)
    return pl.pallas_call(
        _ctx_attn_kernel,
        grid=(b,),
        in_specs=[pl.BlockSpec(memory_space=pltpu.SMEM),
                  padded(GQ_W), padded(GKV_W), padded(GKV_W), padded(WQ_W), padded(WKV_W), padded(WKV_W)],
        out_specs=[spec(GQ_W), spec(WQ_W)],
        out_shape=[jax.ShapeDtypeStruct((b, l, GQ_W), BF16), jax.ShapeDtypeStruct((b, l, WQ_W), BF16)],
        compiler_params=_cparams("arbitrary"),
        name="ctx_attn",
    )(sink, qg, kg, vg, qw, kw, vw)


def _merge_kernel(x_ref, f_ref, og_ref, ow_ref, sh1_ref, sc1_ref, g1_ref, sh2_ref, sc2_ref,
                  n1_ref, n2_ref, wgate_ref, wbf_ref, wbg_ref, wbw_ref, wout_ref, wr_ref,
                  x1_ref, h2_ref, aff_ref, *, chunks):
    tm = x_ref.shape[1]
    rows = [slice(c * (tm // chunks), (c + 1) * (tm // chunks)) for c in range(chunks)]
    d = D_MODEL

    def project(r):
        h = _norm_mod(x_ref[0, r], n1_ref[...], sh1_ref[0], sc1_ref[0]).astype(BF16)
        branches = (_dot(f_ref[0, r], wbf_ref[...]), _dot(og_ref[0, r], wbg_ref[...]), _dot(ow_ref[0, r], wbw_ref[...]))
        return [(_dot(h, wgate_ref[:, i * d:(i + 1) * d]), br) for i, br in enumerate(branches)]

    heads = {0: project(rows[0])}
    for c, r in enumerate(rows):
        if c + 1 < chunks:
            heads[c + 1] = project(rows[c + 1])
        (g0, b0), (g1, b1), (g2, b2) = heads.pop(c)
        m = jax.nn.sigmoid(g0) * b0 + jax.nn.sigmoid(g1) * b1 + jax.nn.sigmoid(g2) * b2
        x1 = x_ref[0, r] + g1_ref[0] * _dot(m.astype(BF16), wout_ref[...])
        x1_ref[0, r] = x1
        h2 = _norm_mod(x1, n2_ref[...], sh2_ref[0], sc2_ref[0]).astype(BF16)
        h2_ref[0, r] = h2
        logits = _dot(h2, wr_ref[...]).T[:N_EXPERTS]
        e = jnp.exp(logits - jnp.max(logits, axis=0, keepdims=True))
        aff_ref[0, :, r] = e / jnp.sum(e, axis=0, keepdims=True)


def _merge(x, f, og, ow, sh1, sc1, g1, sh2, sc2, n1, n2, wgate, wbf, wbg, wbw, wout, wr, *, tm, chunks):
    b, t, d = x.shape
    tok = lambda w: pl.BlockSpec((1, tm, w), lambda i, j: (i, j, 0))
    row = pl.BlockSpec((1, 1, d), lambda i, j: (i, 0, 0))
    const = lambda shape: pl.BlockSpec(shape, lambda i, j: (0,) * len(shape))
    return pl.pallas_call(
        functools.partial(_merge_kernel, chunks=chunks),
        grid=(b, t // tm),
        in_specs=[tok(d), tok(FOURIER_W), tok(GQ_W), tok(WQ_W), row, row, row, row, row,
                  const((1, d)), const((1, d)), const((d, GATE_W)), const((FOURIER_W, d)),
                  const((GQ_W, d)), const((WQ_W, d)), const((d, d)), const((d, LANES))],
        out_specs=[tok(d), tok(d), pl.BlockSpec((1, N_EXPERTS, tm), lambda i, j: (i, 0, j))],
        out_shape=[jax.ShapeDtypeStruct((b, t, d), F32), jax.ShapeDtypeStruct((b, t, d), BF16),
                   jax.ShapeDtypeStruct((b, N_EXPERTS, t), F32)],
        compiler_params=_cparams("arbitrary", "arbitrary"),
        name="merge",
    )(x, f, og, ow, sh1, sc1, g1, sh2, sc2, n1, n2, wgate, wbf, wbg, wbw, wout, wr)


def _cumsum_lanes(m, tri):
    e, n = m.shape
    nch = n // LANES
    stacked = jnp.concatenate([m[:, j * LANES:(j + 1) * LANES] for j in range(nch)], axis=0).astype(BF16)
    w = _dot(stacked, tri)
    outs, off = [], jnp.zeros((e, 1), F32)
    for j in range(nch):
        wj = w[j * e:(j + 1) * e]
        outs.append(wj + off)
        off = off + wj[:, LANES - 1:LANES]
    return jnp.concatenate(outs, axis=1)


def _route_kernel(aff_ref, tri_ref, pos_ref, rt_ref, *win_refs, cap, n_exp):
    aff = aff_ref[...]
    e, n = aff.shape
    thr_bits = jnp.zeros((e, 1), jnp.int32)
    for bit in range(30, -1, -1):
        cand = thr_bits | (1 << bit)
        cnt = jnp.sum(jnp.where(aff >= pltpu.bitcast(cand, F32), 1.0, 0.0), axis=1, keepdims=True)
        thr_bits = jnp.where(cnt >= cap, cand, thr_bits)
    ge = jnp.where(aff >= pltpu.bitcast(thr_bits, F32), 1.0, 0.0)
    gt = jnp.where(aff >= pltpu.bitcast(thr_bits + 1, F32), 1.0, 0.0)
    eq = ge - gt
    room = cap - jnp.sum(gt, axis=1, keepdims=True)
    tri = tri_ref[...]
    sel = gt + jnp.where(_cumsum_lanes(eq, tri) <= room, eq, 0.0)
    cums = _cumsum_lanes(sel, tri)
    pos = jnp.where(sel > 0.0, cums - 1.0, -1.0)
    pos_ref[...] = pos.astype(jnp.int32)
    weight = sel * aff
    pad = jnp.zeros((ROUTE_ROWS - 2 * n_exp, n), F32)
    for i in range(e // n_exp):
        rows = slice(i * n_exp, (i + 1) * n_exp)
        rt_ref[i] = jnp.concatenate([pos[rows], weight[rows], pad], axis=0).T
    if win_refs:
        win_ref, ok_ref = win_refs
        n_tiles = n // SCATTER_TILE
        ends = jnp.concatenate([cums[:, (k + 1) * SCATTER_TILE - 1:(k + 1) * SCATTER_TILE] for k in range(n_tiles)],
                               axis=1)
        starts = jnp.concatenate([jnp.zeros((e, 1), F32), ends[:, :n_tiles - 1]], axis=1)
        align = 16.0
        first = jnp.minimum(jnp.floor(starts / align) * align, float(cap - SCATTER_WINDOW))
        fits = jnp.where(ends - first <= SCATTER_WINDOW, 1.0, 0.0)
        win_ref[...] = first.astype(jnp.int32)
        ok = [jnp.min(fits[i * n_exp:(i + 1) * n_exp], axis=0, keepdims=True) for i in range(e // n_exp)]
        ok_ref[...] = jnp.concatenate(ok, axis=0).astype(jnp.int32)


def _route(aff, tri, *, cap, windows):
    b, e, n = aff.shape
    whole = lambda shape: pl.BlockSpec(shape, lambda i: (0,) * len(shape))
    out_shape = [jax.ShapeDtypeStruct((b * e, n), jnp.int32), jax.ShapeDtypeStruct((b, n, ROUTE_ROWS), F32)]
    if windows:
        n_tiles = n // SCATTER_TILE
        out_shape += [jax.ShapeDtypeStruct((b * e, n_tiles), jnp.int32), jax.ShapeDtypeStruct((b, n_tiles), jnp.int32)]
    outs = pl.pallas_call(
        functools.partial(_route_kernel, cap=cap, n_exp=e),
        grid=(1,),
        in_specs=[whole((b * e, n)), whole((LANES, LANES))],
        out_specs=[whole(s.shape) for s in out_shape],
        out_shape=out_shape,
        compiler_params=_cparams("arbitrary"),
        name="route",
    )(aff.reshape(b * e, n), tri)
    pos, rt = outs[0].reshape(b, e, n), outs[1]
    if not windows:
        return pos, rt
    win = outs[2].reshape(b, e, n_tiles).transpose(0, 2, 1)
    return pos, rt, win, outs[3]


def _gather_kernel(pos_ref, h_ref, o_ref, *, cap, group):
    j = pl.program_id(1)
    n = h_ref.shape[1]
    slot = lax.broadcasted_iota(jnp.int32, (cap, n), 0)
    sel = [jnp.where(pos_ref[0, pl.ds(j * group + g, 1), :] == slot, 1.0, 0.0).astype(BF16) for g in range(group)]
    o_ref[0] = _dot(jnp.concatenate(sel, axis=0), h_ref[0]).astype(BF16)


def _gather(pos, h, *, cap, group):
    b, e, n = pos.shape
    d = h.shape[2]
    return pl.pallas_call(
        functools.partial(_gather_kernel, cap=cap, group=group),
        grid=(b, e // group),
        in_specs=[pl.BlockSpec((1, e, n), lambda i, j: (i, 0, 0)), pl.BlockSpec((1, n, d), lambda i, j: (i, 0, 0))],
        out_specs=pl.BlockSpec((1, group * cap, d), lambda i, j: (i, j, 0)),
        out_shape=jax.ShapeDtypeStruct((b, e * cap, d), BF16),
        compiler_params=_cparams("arbitrary", "arbitrary"),
        name="moe_gather",
    )(pos, h)


def _gather_win_kernel(win_ref, ok_ref, pos_ref, h_ref, o_ref, *, tiles):
    i, j = pl.program_id(0), pl.program_id(1)
    cap = o_ref.shape[2]

    @pl.when(j == 0)
    def _():
        o_ref[...] = jnp.zeros_like(o_ref)

    kts = [j * tiles + u for u in range(tiles)]
    cols = [slice(u * SCATTER_TILE, (u + 1) * SCATTER_TILE) for u in range(tiles)]

    def windows():
        slot = lax.broadcasted_iota(jnp.int32, (SCATTER_WINDOW, SCATTER_TILE), 0)
        firsts, zs = [], []
        for kt, c in zip(kts, cols):
            pos = pos_ref[0, :, c]
            first = [win_ref[i, kt, e] for e in range(N_EXPERTS)]
            pick = [jnp.where(pos[e:e + 1] - first[e] == slot, 1.0, 0.0).astype(BF16) for e in range(N_EXPERTS)]
            zs.append(_dot(jnp.concatenate(pick, axis=0), h_ref[0, c]))
            firsts.append(first)
        for first, z in zip(firsts, zs):
            for e in range(N_EXPERTS):
                rows = pl.ds(pl.multiple_of(first[e], 16), SCATTER_WINDOW)
                o_ref[0, e, rows, :] += z[e * SCATTER_WINDOW:(e + 1) * SCATTER_WINDOW].astype(BF16)

    def full():
        slot = lax.broadcasted_iota(jnp.int32, (cap, SCATTER_TILE), 0)
        for c in cols:
            for e in range(N_EXPERTS):
                pick = jnp.where(pos_ref[0, e:e + 1, c] == slot, 1.0, 0.0).astype(BF16)
                o_ref[0, e] += _dot(pick, h_ref[0, c]).astype(BF16)

    all_fit = functools.reduce(jnp.minimum, [ok_ref[i, kt] for kt in kts])
    lax.cond(all_fit > 0, windows, full)


def _gather_win(win, ok, pos, h, *, cap, tiles):
    b, e, n = pos.shape
    d = h.shape[2]
    smem = pl.BlockSpec(memory_space=pltpu.SMEM)
    step = tiles * SCATTER_TILE
    out = pl.pallas_call(
        functools.partial(_gather_win_kernel, tiles=tiles),
        grid=(b, n // step),
        in_specs=[smem, smem,
                  pl.BlockSpec((1, e, step), lambda i, j: (i, 0, j)),
                  pl.BlockSpec((1, step, d), lambda i, j: (i, j, 0))],
        out_specs=pl.BlockSpec((1, e, cap, d), lambda i, j: (i, 0, 0, 0)),
        out_shape=jax.ShapeDtypeStruct((b, e, cap, d), BF16),
        compiler_params=_cparams("arbitrary", "arbitrary"),
        name="moe_gather_win",
    )(win, ok, pos, h)
    return out.reshape(b, e * cap, d)


def _expert_kernel(*refs, n_sets):
    x_refs, (wg_ref, wu_ref, wd_ref) = refs[:n_sets], refs[n_sets:n_sets + 3]
    o_refs = refs[n_sets + 3:2 * n_sets + 3]
    d = x_refs[0].shape[2]
    rows = [r.shape[0] * r.shape[1] for r in x_refs]
    xs = [r[...].reshape(n, d) for r, n in zip(x_refs, rows)]
    x = xs[0] if n_sets == 1 else jnp.concatenate(xs, axis=0)
    wg, wu, wd = (w[0, 0].astype(BF16) for w in (wg_ref, wu_ref, wd_ref))
    half = x.shape[0] // 2
    halves = [x[:half], x[half:]]
    up = [(_dot(xh, wg), _dot(xh, wu)) for xh in halves]
    y = jnp.concatenate([_dot((a * jax.nn.sigmoid(a) * u).astype(BF16), wd).astype(BF16) for a, u in up], axis=0)
    start = 0
    for o_ref, n in zip(o_refs, rows):
        o_ref[...] = y[start:start + n].reshape(o_ref.shape)
        start += n


def _experts(xgs, caps, layer, wg, wu, wd, *, nb):
    b, _, d = xgs[0].shape
    _, e, _, f = wg.shape
    wspec = lambda r, c: pl.BlockSpec((1, 1, r, c), lambda i, j: (layer, i, 0, 0))
    xspecs = [pl.BlockSpec((nb, cap, d), lambda i, j: (j, i, 0)) for cap in caps]
    return pl.pallas_call(
        functools.partial(_expert_kernel, n_sets=len(xgs)),
        grid=(e, b // nb),
        in_specs=xspecs + [wspec(d, f), wspec(d, f), wspec(f, d)],
        out_specs=xspecs,
        out_shape=[jax.ShapeDtypeStruct(xg.shape, BF16) for xg in xgs],
        compiler_params=_cparams("arbitrary", "arbitrary"),
        name="moe_experts",
    )(*xgs, wg, wu, wd)


def _combine_matrix(rt, first, width):
    slot = lax.broadcasted_iota(jnp.int32, (rt.shape[0], width), 1).astype(F32)
    cols = [jnp.where(rt[:, e:e + 1] - first[e] == slot, rt[:, N_EXPERTS + e:N_EXPERTS + e + 1], 0.0).astype(BF16)
            for e in range(N_EXPERTS)]
    return jnp.concatenate(cols, axis=1)


def _scatter_kernel(*refs, cap, final, windowed):
    if windowed:
        win_ref, ok_ref, x_ref, rt_ref, y_ref, g2_ref, fg_ref, o_ref = refs
    else:
        x_ref, rt_ref, y_ref, g2_ref, fg_ref, o_ref = refs
    i, j = pl.program_id(0), pl.program_id(1)
    tn = x_ref.shape[1]

    def finish(rows, y):
        x = x_ref[0, rows] + g2_ref[0] * y
        if final:
            ms = jnp.mean(x * x, axis=-1, keepdims=True)
            x = x * lax.rsqrt(ms + EPS) * fg_ref[...]
        o_ref[0, rows] = x

    def full():
        rows = slice(0, tn)
        finish(rows, _dot(_combine_matrix(rt_ref[0], [0.0] * N_EXPERTS, cap), y_ref[0]))

    if not windowed:
        full()
        return
    subs = tn // SCATTER_TILE
    tiles = [j * subs + u for u in range(subs)]

    def windows():
        def contract(u):
            rt = rt_ref[0, u * SCATTER_TILE:(u + 1) * SCATTER_TILE]
            first = [win_ref[i, tiles[u], e] for e in range(N_EXPERTS)]
            picked = [y_ref[0, pl.ds(pl.multiple_of(e * cap + first[e], 16), SCATTER_WINDOW), :]
                      for e in range(N_EXPERTS)]
            a = _combine_matrix(rt, [f.astype(F32) for f in first], SCATTER_WINDOW)
            return _dot(a, jnp.concatenate(picked, axis=0))

        ys = {0: contract(0)}
        for u in range(subs):
            if u + 1 < subs:
                ys[u + 1] = contract(u + 1)
            finish(slice(u * SCATTER_TILE, (u + 1) * SCATTER_TILE), ys.pop(u))

    all_fit = functools.reduce(jnp.minimum, [ok_ref[i, tile] for tile in tiles])
    lax.cond(all_fit > 0, windows, full)


def _scatter(x, rt, y, g2, fg, *, cap, final, tn, windows=None):
    b, t, d = x.shape
    smem = pl.BlockSpec(memory_space=pltpu.SMEM)
    windowed = windows is not None
    return pl.pallas_call(
        functools.partial(_scatter_kernel, cap=cap, final=final, windowed=windowed),
        grid=(b, t // tn),
        in_specs=([smem, smem] if windowed else []) + [
            pl.BlockSpec((1, tn, d), lambda i, j: (i, j, 0)),
            pl.BlockSpec((1, tn, ROUTE_ROWS), lambda i, j: (i, j, 0)),
            pl.BlockSpec((1, N_EXPERTS * cap, d), lambda i, j: (i, 0, 0)),
            pl.BlockSpec((1, 1, d), lambda i, j: (i, 0, 0)),
            pl.BlockSpec((1, d), lambda i, j: (0, 0))],
        out_specs=pl.BlockSpec((1, tn, d), lambda i, j: (i, j, 0)),
        out_shape=jax.ShapeDtypeStruct((b, t, d), F32),
        compiler_params=_cparams("arbitrary", "arbitrary"),
        name="moe_scatter",
    )(*(windows if windowed else ()), x, rt, y, g2, fg)


def _capacity(t):
    return CAPACITY_FACTOR * t // N_EXPERTS


def _dft_tables(n):
    k = np.arange(n, dtype=np.int64)
    ang = 2.0 * np.pi * ((k[:, None] * k[None, :]) % n).astype(np.float64) / n
    return np.cos(ang), np.sin(ang)


def _channel_dft():
    c, s = _dft_tables(FOURIER_GROUP_CH)
    eye = np.eye(FOURIER_GROUPS)
    return np.concatenate([np.kron(eye, c), np.kron(eye, s)], axis=1)


def _rope_tables(t):
    rows = t // GRID_W
    r, col = jnp.meshgrid(jnp.arange(rows), jnp.arange(GRID_W), indexing="ij")
    half = HEAD_DIM // 2
    inv = ROPE_THETA ** (-jnp.arange(0, half, 2, dtype=F32) / half)
    ang = jnp.concatenate([r.reshape(-1, 1).astype(F32) * inv, col.reshape(-1, 1).astype(F32) * inv], axis=-1)
    cos = jnp.repeat(jnp.cos(ang), 2, axis=1)
    sin = jnp.repeat(jnp.sin(ang), 2, axis=1) * jnp.tile(jnp.asarray([-1.0, 1.0], F32), half)
    return jnp.tile(cos, (1, HEAD_PAD)), jnp.tile(sin, (1, HEAD_PAD))


def kernel(x, c, ctx, c_ctx, w_ada, b_ada, norm1_g, w_in, q_norm_g, k_norm_g, sink, w_br_fourier, w_br_global,
           w_br_window, w_out, norm2_g, w_router, w_gate_e, w_up_e, w_down_e, final_g):
    b, t, d = x.shape
    l_ctx = ctx.shape[1]
    depth = w_ada.shape[0]
    cap_t, cap_c = _capacity(t), _capacity(l_ctx)

    cos_t, sin_t = _rope_tables(t)
    cos_c, sin_c = jnp.ones((l_ctx, LANES), F32), jnp.zeros((l_ctx, LANES), F32)
    head_avg = jnp.asarray(np.kron(np.eye(MXU_DIM // HEAD_DIM), np.full((HEAD_DIM, HEAD_DIM), 1.0 / HEAD_DIM)), BF16)
    cdft = jnp.asarray(_channel_dft(), F32)
    cn_t, sn_t = (jnp.asarray(a, F32) for a in _dft_tables(t))
    cn_c, sn_c = (jnp.asarray(a, F32) for a in _dft_tables(l_ctx))
    tri = jnp.asarray(np.triu(np.ones((LANES, LANES))), BF16)

    cvec = jnp.concatenate([c, c_ctx[None], jnp.zeros((MOD_ROWS - b - 1, d), F32)], axis=0)
    mods = _ada(cvec, w_ada, b_ada)

    xc = ctx
    for l in range(depth):
        need_ctx = l < depth - 1
        final = l == depth - 1
        lat = [mods[l, :b, i * d:(i + 1) * d].reshape(b, 1, d) for i in range(6)]
        cmod = [jnp.broadcast_to(mods[l, b, i * d:(i + 1) * d].reshape(1, 1, d), (b, 1, d)) for i in range(6)]
        w_mix = w_in[l, :, :MIX_W].astype(BF16)
        w_gate = w_in[l, :, MIX_W:].astype(BF16)
        n1, n2 = norm1_g[l].reshape(1, d), norm2_g[l].reshape(1, d)
        qn = jnp.tile(q_norm_g[l], G_HEADS).reshape(1, GQ_W)
        kn = jnp.tile(k_norm_g[l], G_KV).reshape(1, GKV_W)
        wbf, wbg, wbw = (w[l].astype(BF16) for w in (w_br_fourier, w_br_global, w_br_window))
        wout = w_out[l].astype(BF16)
        wr = jnp.pad(w_router[l], ((0, 0), (0, LANES - N_EXPERTS))).astype(BF16)
        fg = final_g.reshape(1, d)
        merge_w = (n1, n2, w_gate, wbf, wbg, wbw, wout, wr)

        cfcs, cqg, ckg, cvg, cqw, ckw, cvw = _inproj(xc, cmod[0], cmod[1], n1, w_mix, cos_c, sin_c, head_avg, cdft,
                                                     qn, kn, rope=False, tm=l_ctx, chunks=2)
        xgs, caps = [], []
        if need_ctx:
            cf_mix = _fourier(cfcs, cn_c, sn_c, tr=l_ctx)
            oc_g, oc_w = _ctx_attn(sink[l], cqg, ckg, cvg, cqw, ckw, cvw)
            xc1, hc2, caff = _merge(xc, cf_mix, oc_g, oc_w, *cmod[:5], *merge_w, tm=l_ctx, chunks=1)
            cpos, crt = _route(caff, tri, cap=cap_c, windows=False)
            xgs.append(_gather(cpos, hc2, cap=cap_c, group=N_EXPERTS))
            caps.append(cap_c)

        fcs, qg, kg, vg, qw, kw, vw = _inproj(x, lat[0], lat[1], n1, w_mix, cos_t, sin_t, head_avg, cdft, qn, kn,
                                              rope=True, tm=1024, chunks=8)
        f_mix = _fourier(fcs, cn_t, sn_t, tr=1024)
        o_g = _global_attn(qg, kg, vg, ckg, cvg, tq=256)
        o_w = _window_attn(sink[l], qw, kw, vw, ckw, cvw)
        x1, h2, aff = _merge(x, f_mix, o_g, o_w, *lat[:5], *merge_w, tm=1024, chunks=2)
        pos, rt, win, win_ok = _route(aff, tri, cap=cap_t, windows=True)
        xgs.insert(0, _gather_win(win, win_ok, pos, h2, cap=cap_t, tiles=4))
        caps.insert(0, cap_t)

        ys = _experts(xgs, caps, l, w_gate_e, w_up_e, w_down_e, nb=4)
        x = _scatter(x1, rt, ys[0], lat[5], fg, cap=cap_t, final=final, tn=1024, windows=(win, win_ok))
        if need_ctx:
            xc = _scatter(xc1, crt, ys[1], cmod[5], fg, cap=cap_c, final=False, tn=l_ctx)
    return x
```

```python
import functools

import numpy as np
import jax
import jax.numpy as jnp
from jax import lax
from jax.experimental import pallas as pl
from jax.experimental.pallas import tpu as pltpu

F32 = jnp.float32
BF16 = jnp.bfloat16

D_MODEL = 1024
HEAD_DIM = 64
GRID_W = 64
FOURIER_GROUPS = 4
FOURIER_GROUP_CH = 64
FOURIER_W = FOURIER_GROUPS * FOURIER_GROUP_CH
G_HEADS, G_KV = 8, 2
G_GROUP = G_HEADS // G_KV
W_HEADS, W_KV = 4, 2
W_GROUP = W_HEADS // W_KV
WINDOW = 128
Q_BLOCK = 128
N_BRANCH = 3
GQ_W = G_HEADS * HEAD_DIM
GKV_W = G_KV * HEAD_DIM
WQ_W = W_HEADS * HEAD_DIM
WKV_W = W_KV * HEAD_DIM
OFF_F = 0
OFF_GQ = OFF_F + FOURIER_W
OFF_GK = OFF_GQ + GQ_W
OFF_GV = OFF_GK + GKV_W
OFF_WQ = OFF_GV + GKV_W
OFF_WK = OFF_WQ + WQ_W
OFF_WV = OFF_WK + WKV_W
MIX_W = OFF_WV + WKV_W
GATE_W = N_BRANCH * D_MODEL
ROPE_THETA = 10000.0
N_EXPERTS = 16
CAPACITY_FACTOR = 2
EPS = 1e-6
NEG_INF = -1e30
LANES = 128
HEAD_PAD = LANES // HEAD_DIM
MXU_DIM = 256
MOD_ROWS = 16
SCATTER_TILE = 256
SCATTER_WINDOW = 64
ROUTE_ROWS = 128
WINDOW_TILE = 8
WINDOW_LOOKAHEAD = 2
GLOBAL_CHAINS_PER_KV = 1
VMEM_LIMIT = 56 * 1024 * 1024


def _cparams(*sem):
    return pltpu.CompilerParams(dimension_semantics=sem, vmem_limit_bytes=VMEM_LIMIT)


def _norm_mod(x, g, sh, sc):
    ms = jnp.mean(x * x, axis=-1, keepdims=True)
    return (x * lax.rsqrt(ms + EPS) * g) * (1.0 + sc) + sh


def _dot(a, b):
    return jnp.dot(a, b, preferred_element_type=F32)


def _dot_nt(a, b):
    return lax.dot_general(a, b, (((1,), (1,)), ((), ())), preferred_element_type=F32)


def _ada_kernel(c_ref, w_ref, b_ref, o_ref):
    c = c_ref[...]
    s = (c * jax.nn.sigmoid(c)).astype(BF16)
    o_ref[0] = _dot(s, w_ref[0].astype(BF16)) + b_ref[0]


def _ada(cvec, w_ada, b_ada):
    depth, d, n = w_ada.shape
    tn = 3072
    return pl.pallas_call(
        _ada_kernel,
        grid=(depth, n // tn),
        in_specs=[pl.BlockSpec((MOD_ROWS, d), lambda l, j: (0, 0)),
                  pl.BlockSpec((1, d, tn), lambda l, j: (l, 0, j)),
                  pl.BlockSpec((1, 1, tn), lambda l, j: (l, 0, j))],
        out_specs=pl.BlockSpec((1, MOD_ROWS, tn), lambda l, j: (l, 0, j)),
        out_shape=jax.ShapeDtypeStruct((depth, MOD_ROWS, n), F32),
        compiler_params=_cparams("arbitrary", "arbitrary"),
        name="ada",
    )(cvec, w_ada, b_ada.reshape(depth, 1, n))


def _rope(x, cos, sin_signed, even):
    outs = []
    for j in range(x.shape[1] // LANES):
        xb = x[:, j * LANES:(j + 1) * LANES]
        swap = jnp.where(even, pltpu.roll(xb, LANES - 1, 1), pltpu.roll(xb, 1, 1))
        outs.append(xb * cos + swap * sin_signed)
    return outs[0] if len(outs) == 1 else jnp.concatenate(outs, axis=1)


def _head_mean_square(z, bd):
    zz = (z * z).astype(BF16)
    blk = bd.shape[0]
    if z.shape[1] < blk:
        return _dot(zz, bd[:z.shape[1], :z.shape[1]])
    outs = [_dot(zz[:, j * blk:(j + 1) * blk], bd) for j in range(z.shape[1] // blk)]
    return outs[0] if len(outs) == 1 else jnp.concatenate(outs, axis=1)


def _pad_heads_f32(x, fill):
    blk = jnp.full((x.shape[0], LANES - HEAD_DIM), fill, F32)
    parts = []
    for h in range(x.shape[1] // HEAD_DIM):
        parts += [x[:, h * HEAD_DIM:(h + 1) * HEAD_DIM], blk]
    return jnp.concatenate(parts, axis=1)


def _pad_heads(x, fill):
    return _pad_heads_f32(x, fill).astype(BF16)


def _inproj_kernel(x_ref, sh_ref, sc_ref, g_ref, w_ref, cos_ref, sin_ref, bd_ref, cdft_ref, qn_ref, kn_ref,
                   fcs_ref, qg_ref, kg_ref, vg_ref, qw_ref, kw_ref, vw_ref, *, rope, chunks):
    tm = x_ref.shape[1]
    rows = [slice(c * (tm // chunks), (c + 1) * (tm // chunks)) for c in range(chunks)]
    bd = bd_ref[...]
    scale = HEAD_DIM ** -0.5

    def project(r):
        return _dot(_norm_mod(x_ref[0, r], g_ref[...], sh_ref[0], sc_ref[0]).astype(BF16), w_ref[...])

    zs = {0: project(rows[0])}
    for c, r in enumerate(rows):
        if c + 1 < chunks:
            zs[c + 1] = project(rows[c + 1])
        z = zs.pop(c)
        fcs_ref[0, r] = _dot(z[:, OFF_F:OFF_F + FOURIER_W].astype(BF16), cdft_ref[...].astype(BF16)).astype(BF16)
        q = z[:, OFF_GQ:OFF_GQ + GQ_W]
        k = z[:, OFF_GK:OFF_GK + GKV_W]
        q = q * lax.rsqrt(_head_mean_square(q, bd) + EPS) * qn_ref[...]
        k = k * lax.rsqrt(_head_mean_square(k, bd) + EPS) * kn_ref[...]
        qw = z[:, OFF_WQ:OFF_WQ + WQ_W]
        kw = z[:, OFF_WK:OFF_WK + WKV_W]
        if rope:
            cos, sin = cos_ref[r], sin_ref[r]
            even = (lax.broadcasted_iota(jnp.int32, cos.shape, 1) % 2) == 0
            q, k = _rope(q, cos, sin, even), _rope(k, cos, sin, even)
            qw, kw = _rope(qw, cos, sin, even), _rope(kw, cos, sin, even)
        qg_ref[0, r] = _pad_heads(q * scale, 0.0)
        kg_ref[0, r] = _pad_heads(k, 0.0)
        vg_ref[0, r] = _pad_heads(z[:, OFF_GV:OFF_GV + GKV_W], 1.0)
        qw_ref[0, r] = _pad_heads(qw * scale, 0.0)
        kw_ref[0, r] = _pad_heads(kw, 0.0)
        vw_ref[0, r] = _pad_heads(z[:, OFF_WV:OFF_WV + WKV_W], 1.0)


def _inproj(x, sh, sc, g, w_mix, cos_t, sin_t, bd, cdft, qn, kn, *, rope, tm, chunks):
    b, t, d = x.shape
    tok = lambda w: pl.BlockSpec((1, tm, w), lambda i, j: (i, j, 0))
    row = pl.BlockSpec((1, 1, d), lambda i, j: (i, 0, 0))
    const = lambda shape: pl.BlockSpec(shape, lambda i, j: (0,) * len(shape))
    widths = (2 * FOURIER_W,) + tuple(HEAD_PAD * w for w in (GQ_W, GKV_W, GKV_W, WQ_W, WKV_W, WKV_W))
    out_specs = [tok(w) for w in widths]
    out_shape = [jax.ShapeDtypeStruct((b, t, w), BF16) for w in widths]
    return pl.pallas_call(
        functools.partial(_inproj_kernel, rope=rope, chunks=chunks),
        grid=(b, t // tm),
        in_specs=[tok(d), row, row, const((1, d)), const((d, MIX_W)),
                  pl.BlockSpec((tm, LANES), lambda i, j: (j, 0)),
                  pl.BlockSpec((tm, LANES), lambda i, j: (j, 0)),
                  const((MXU_DIM, MXU_DIM)), const((FOURIER_W, 2 * FOURIER_W)),
                  const((1, GQ_W)), const((1, GKV_W))],
        out_specs=out_specs,
        out_shape=out_shape,
        compiler_params=_cparams("arbitrary", "arbitrary"),
        name="inproj_rope" if rope else "inproj_ctx",
    )(x, sh, sc, g, w_mix, cos_t, sin_t, bd, cdft, qn, kn)


def _fourier_kernel(cn_ref, sn_ref, fcs_ref, o_ref, cn_s, sn_s, *, scale):
    @pl.when(pl.program_id(1) == 0)
    def _():
        cn_s[...] = cn_ref[...].astype(BF16)
        sn_s[...] = sn_ref[...].astype(BF16)

    fcs = fcs_ref[0]
    re = _dot(cn_s[...], fcs[:, :FOURIER_W]) - _dot(sn_s[...], fcs[:, FOURIER_W:])
    o_ref[0] = (re * scale).astype(BF16)


def _fourier(fcs, cn, sn, *, tr):
    b, n, _ = fcs.shape
    scale = float((n * FOURIER_GROUP_CH) ** -0.5)
    return pl.pallas_call(
        functools.partial(_fourier_kernel, scale=scale),
        grid=(n // tr, b),
        in_specs=[pl.BlockSpec((tr, n), lambda r, i: (r, 0)),
                  pl.BlockSpec((tr, n), lambda r, i: (r, 0)),
                  pl.BlockSpec((1, n, 2 * FOURIER_W), lambda r, i: (i, 0, 0))],
        out_specs=pl.BlockSpec((1, tr, FOURIER_W), lambda r, i: (i, r, 0)),
        out_shape=jax.ShapeDtypeStruct((b, n, FOURIER_W), BF16),
        scratch_shapes=[pltpu.VMEM((tr, n), BF16), pltpu.VMEM((tr, n), BF16)],
        compiler_params=_cparams("arbitrary", "arbitrary"),
        name="fourier",
    )(cn, sn, fcs)


def _head(x, h):
    return x[:, h * LANES:(h + 1) * LANES]


def _stack_heads(q, first, count):
    return jnp.concatenate([_head(q, first + g) for g in range(count)], axis=0)


def _softmax_pv(parts, extra=None):
    m = functools.reduce(jnp.maximum, [jnp.max(s, axis=-1, keepdims=True) for s, _ in parts])
    if extra is not None:
        m = jnp.maximum(m, extra)
    acc = 0.0
    for s, v in parts:
        acc = acc + _dot(jnp.exp((s - m).astype(BF16)), v)
    den = acc[:, HEAD_DIM:HEAD_DIM + 1]
    if extra is not None:
        den = den + jnp.exp(extra - m)
    return acc[:, :HEAD_DIM] / den


def _unstack_heads(o_list, tq, count):
    cols = []
    for o in o_list:
        cols += [o[g * tq:(g + 1) * tq] for g in range(count)]
    return jnp.concatenate(cols, axis=1)


def _global_attn_kernel(q_ref, kl_ref, vl_ref, kc_ref, vc_ref, o_ref):
    q = q_ref[0]
    tq = q.shape[0]
    per_chain = G_GROUP // GLOBAL_CHAINS_PER_KV
    parts = []
    for kv in range(G_KV):
        for c in range(GLOBAL_CHAINS_PER_KV):
            qs = _stack_heads(q, kv * G_GROUP + c * per_chain, per_chain)
            parts.append([(_dot_nt(qs, _head(kc_ref[0], kv)), _head(vc_ref[0], kv)),
                          (_dot_nt(qs, _head(kl_ref[0], kv)), _head(vl_ref[0], kv))])
    outs = [_softmax_pv(p) for p in parts]
    o_ref[0] = _unstack_heads(outs, tq, per_chain).astype(BF16)


def _global_attn(q, kl, vl, kc, vc, *, tq):
    b, t, _ = q.shape
    l = kc.shape[1]
    full = lambda n: pl.BlockSpec((1, n, HEAD_PAD * GKV_W), lambda i, j: (i, 0, 0))
    return pl.pallas_call(
        _global_attn_kernel,
        grid=(b, t // tq),
        in_specs=[pl.BlockSpec((1, tq, HEAD_PAD * GQ_W), lambda i, j: (i, j, 0)), full(t), full(t), full(l), full(l)],
        out_specs=pl.BlockSpec((1, tq, GQ_W), lambda i, j: (i, j, 0)),
        out_shape=jax.ShapeDtypeStruct((b, t, GQ_W), BF16),
        compiler_params=_cparams("arbitrary", "arbitrary"),
        name="global_attn",
    )(q, kl, vl, kc, vc)


def _sink_column(sink_ref, kv, rows_per_head):
    r = lax.broadcasted_iota(jnp.int32, (W_GROUP * rows_per_head, 1), 0)
    col = jnp.full(r.shape, sink_ref[kv * W_GROUP], F32)
    for g in range(1, W_GROUP):
        col = jnp.where(r >= g * rows_per_head, sink_ref[kv * W_GROUP + g], col)
    return col


def _window_attn_kernel(sink_ref, q_ref, kp_ref, k0_ref, kn_ref, vp_ref, v0_ref, vn_ref, kc_ref, vc_ref, o_ref,
                        *, n_tiles):
    j = pl.program_id(1)
    q = q_ref[0]
    kb = jnp.concatenate([kp_ref[0], k0_ref[0], kn_ref[0]], axis=0)
    vb = jnp.concatenate([vp_ref[0], v0_ref[0], vn_ref[0]], axis=0)
    rows = W_GROUP * Q_BLOCK
    qpos = lax.broadcasted_iota(jnp.int32, (rows, 3 * Q_BLOCK), 0) % Q_BLOCK
    kpos = lax.broadcasted_iota(jnp.int32, (rows, 3 * Q_BLOCK), 1)
    dist = jnp.abs(qpos - kpos + Q_BLOCK)
    in_window = dist <= WINDOW
    outside = jnp.full(dist.shape, WINDOW + 1, jnp.int32)
    first_ok = jnp.where((kpos < Q_BLOCK) & (j == 0), outside, dist) <= WINDOW
    last_ok = jnp.where((kpos >= 2 * Q_BLOCK) & (j == n_tiles - 1), outside, dist) <= WINDOW
    def scores(n):
        i, kv = divmod(n, W_KV)
        valid = first_ok if i == 0 else (last_ok if i == WINDOW_TILE - 1 else in_window)
        band = slice(i * Q_BLOCK, (i + 3) * Q_BLOCK)
        qs = _stack_heads(q[i * Q_BLOCK:(i + 1) * Q_BLOCK], kv * W_GROUP, W_GROUP)
        s_b = jnp.where(valid, _dot_nt(qs, _head(kb, kv)[band]), NEG_INF)
        return [(_dot_nt(qs, _head(kc_ref[0], kv)), _head(vc_ref[0], kv)), (s_b, _head(vb, kv)[band])]

    sinks = [_sink_column(sink_ref, kv, Q_BLOCK) for kv in range(W_KV)]
    n_chains = WINDOW_TILE * W_KV
    pending = {n: scores(n) for n in range(min(WINDOW_LOOKAHEAD, n_chains))}
    outs = []
    for n in range(n_chains):
        if n + WINDOW_LOOKAHEAD < n_chains:
            pending[n + WINDOW_LOOKAHEAD] = scores(n + WINDOW_LOOKAHEAD)
        outs.append(_softmax_pv(pending.pop(n), sinks[n % W_KV]))
    blocks = [_unstack_heads(outs[i * W_KV:(i + 1) * W_KV], Q_BLOCK, W_GROUP) for i in range(WINDOW_TILE)]
    o_ref[0] = jnp.concatenate(blocks, axis=0).astype(BF16)


def _window_attn(sink, q, k, v, kc, vc):
    b, t, _ = q.shape
    l = kc.shape[1]
    nb = t // Q_BLOCK
    tile = WINDOW_TILE * Q_BLOCK
    kvw = HEAD_PAD * WKV_W
    edge = lambda f: pl.BlockSpec((1, Q_BLOCK, kvw), lambda i, j: (i, f(j), 0))
    prev = lambda j: jnp.maximum(j * WINDOW_TILE - 1, 0)
    nxt = lambda j: jnp.minimum((j + 1) * WINDOW_TILE, nb - 1)
    mid = pl.BlockSpec((1, tile, kvw), lambda i, j: (i, j, 0))
    full = pl.BlockSpec((1, l, kvw), lambda i, j: (i, 0, 0))
    return pl.pallas_call(
        functools.partial(_window_attn_kernel, n_tiles=t // tile),
        grid=(b, t // tile),
        in_specs=[pl.BlockSpec(memory_space=pltpu.SMEM),
                  pl.BlockSpec((1, tile, HEAD_PAD * WQ_W), lambda i, j: (i, j, 0)),
                  edge(prev), mid, edge(nxt), edge(prev), mid, edge(nxt), full, full],
        out_specs=pl.BlockSpec((1, tile, WQ_W), lambda i, j: (i, j, 0)),
        out_shape=jax.ShapeDtypeStruct((b, t, WQ_W), BF16),
        compiler_params=_cparams("arbitrary", "arbitrary"),
        name="window_attn",
    )(sink, q, k, k, k, v, v, v, kc, vc)


def _ctx_attn_kernel(sink_ref, qg_ref, kg_ref, vg_ref, qw_ref, kw_ref, vw_ref, og_ref, ow_ref):
    l = qg_ref.shape[1]
    qg, qw = qg_ref[0], qw_ref[0]
    g_parts = [[(_dot_nt(_stack_heads(qg, kv * G_GROUP, G_GROUP), _head(kg_ref[0], kv)), _head(vg_ref[0], kv))]
               for kv in range(G_KV)]
    w_parts = [[(_dot_nt(_stack_heads(qw, kv * W_GROUP, W_GROUP), _head(kw_ref[0], kv)), _head(vw_ref[0], kv))]
               for kv in range(W_KV)]
    og_ref[0] = _unstack_heads([_softmax_pv(p) for p in g_parts], l, G_GROUP).astype(BF16)
    outs = [_softmax_pv(p, _sink_column(sink_ref, kv, l)) for kv, p in enumerate(w_parts)]
    ow_ref[0] = _unstack_heads(outs, l, W_GROUP).astype(BF16)


def _ctx_attn(sink, qg, kg, vg, qw, kw, vw):
    b, l, _ = qg.shape
    spec = lambda w: pl.BlockSpec((1, l, w), lambda i: (i, 0, 0))
    padded = lambda w: spec(HEAD_PAD * w)
    return pl.pallas_call(
        _ctx_attn_kernel,
        grid=(b,),
        in_specs=[pl.BlockSpec(memory_space=pltpu.SMEM),
                  padded(GQ_W), padded(GKV_W), padded(GKV_W), padded(WQ_W), padded(WKV_W), padded(WKV_W)],
        out_specs=[spec(GQ_W), spec(WQ_W)],
        out_shape=[jax.ShapeDtypeStruct((b, l, GQ_W), BF16), jax.ShapeDtypeStruct((b, l, WQ_W), BF16)],
        compiler_params=_cparams("arbitrary"),
        name="ctx_attn",
    )(sink, qg, kg, vg, qw, kw, vw)


def _merge_kernel(x_ref, f_ref, og_ref, ow_ref, sh1_ref, sc1_ref, g1_ref, sh2_ref, sc2_ref,
                  n1_ref, n2_ref, wgate_ref, wbf_ref, wbg_ref, wbw_ref, wout_ref, wr_ref,
                  x1_ref, h2_ref, aff_ref, *, chunks):
    tm = x_ref.shape[1]
    rows = [slice(c * (tm // chunks), (c + 1) * (tm // chunks)) for c in range(chunks)]
    d = D_MODEL

    def project(r):
        h = _norm_mod(x_ref[0, r], n1_ref[...], sh1_ref[0], sc1_ref[0]).astype(BF16)
        branches = (_dot(f_ref[0, r], wbf_ref[...]), _dot(og_ref[0, r], wbg_ref[...]), _dot(ow_ref[0, r], wbw_ref[...]))
        return [(_dot(h, wgate_ref[:, i * d:(i + 1) * d]), br) for i, br in enumerate(branches)]

    heads = {0: project(rows[0])}
    for c, r in enumerate(rows):
        if c + 1 < chunks:
            heads[c + 1] = project(rows[c + 1])
        (g0, b0), (g1, b1), (g2, b2) = heads.pop(c)
        m = jax.nn.sigmoid(g0) * b0 + jax.nn.sigmoid(g1) * b1 + jax.nn.sigmoid(g2) * b2
        x1 = x_ref[0, r] + g1_ref[0] * _dot(m.astype(BF16), wout_ref[...])
        x1_ref[0, r] = x1
        h2 = _norm_mod(x1, n2_ref[...], sh2_ref[0], sc2_ref[0]).astype(BF16)
        h2_ref[0, r] = h2
        logits = _dot(h2, wr_ref[...]).T[:N_EXPERTS]
        e = jnp.exp(logits - jnp.max(logits, axis=0, keepdims=True))
        aff_ref[0, :, r] = e / jnp.sum(e, axis=0, keepdims=True)


def _merge(x, f, og, ow, sh1, sc1, g1, sh2, sc2, n1, n2, wgate, wbf, wbg, wbw, wout, wr, *, tm, chunks):
    b, t, d = x.shape
    tok = lambda w: pl.BlockSpec((1, tm, w), lambda i, j: (i, j, 0))
    row = pl.BlockSpec((1, 1, d), lambda i, j: (i, 0, 0))
    const = lambda shape: pl.BlockSpec(shape, lambda i, j: (0,) * len(shape))
    return pl.pallas_call(
        functools.partial(_merge_kernel, chunks=chunks),
        grid=(b, t // tm),
        in_specs=[tok(d), tok(FOURIER_W), tok(GQ_W), tok(WQ_W), row, row, row, row, row,
                  const((1, d)), const((1, d)), const((d, GATE_W)), const((FOURIER_W, d)),
                  const((GQ_W, d)), const((WQ_W, d)), const((d, d)), const((d, LANES))],
        out_specs=[tok(d), tok(d), pl.BlockSpec((1, N_EXPERTS, tm), lambda i, j: (i, 0, j))],
        out_shape=[jax.ShapeDtypeStruct((b, t, d), F32), jax.ShapeDtypeStruct((b, t, d), BF16),
                   jax.ShapeDtypeStruct((b, N_EXPERTS, t), F32)],
        compiler_params=_cparams("arbitrary", "arbitrary"),
        name="merge",
    )(x, f, og, ow, sh1, sc1, g1, sh2, sc2, n1, n2, wgate, wbf, wbg, wbw, wout, wr)


def _cumsum_lanes(m, tri):
    e, n = m.shape
    nch = n // LANES
    stacked = jnp.concatenate([m[:, j * LANES:(j + 1) * LANES] for j in range(nch)], axis=0).astype(BF16)
    w = _dot(stacked, tri)
    outs, off = [], jnp.zeros((e, 1), F32)
    for j in range(nch):
        wj = w[j * e:(j + 1) * e]
        outs.append(wj + off)
        off = off + wj[:, LANES - 1:LANES]
    return jnp.concatenate(outs, axis=1)


def _route_kernel(aff_ref, tri_ref, pos_ref, rt_ref, *win_refs, cap, n_exp):
    aff = aff_ref[...]
    e, n = aff.shape
    thr_bits = jnp.zeros((e, 1), jnp.int32)
    for bit in range(30, -1, -1):
        cand = thr_bits | (1 << bit)
        cnt = jnp.sum(jnp.where(aff >= pltpu.bitcast(cand, F32), 1.0, 0.0), axis=1, keepdims=True)
        thr_bits = jnp.where(cnt >= cap, cand, thr_bits)
    ge = jnp.where(aff >= pltpu.bitcast(thr_bits, F32), 1.0, 0.0)
    gt = jnp.where(aff >= pltpu.bitcast(thr_bits + 1, F32), 1.0, 0.0)
    eq = ge - gt
    room = cap - jnp.sum(gt, axis=1, keepdims=True)
    tri = tri_ref[...]
    sel = gt + jnp.where(_cumsum_lanes(eq, tri) <= room, eq, 0.0)
    cums = _cumsum_lanes(sel, tri)
    pos = jnp.where(sel > 0.0, cums - 1.0, -1.0)
    pos_ref[...] = pos.astype(jnp.int32)
    weight = sel * aff
    pad = jnp.zeros((ROUTE_ROWS - 2 * n_exp, n), F32)
    for i in range(e // n_exp):
        rows = slice(i * n_exp, (i + 1) * n_exp)
        rt_ref[i] = jnp.concatenate([pos[rows], weight[rows], pad], axis=0).T
    if win_refs:
        win_ref, ok_ref = win_refs
        n_tiles = n // SCATTER_TILE
        ends = jnp.concatenate([cums[:, (k + 1) * SCATTER_TILE - 1:(k + 1) * SCATTER_TILE] for k in range(n_tiles)],
                               axis=1)
        starts = jnp.concatenate([jnp.zeros((e, 1), F32), ends[:, :n_tiles - 1]], axis=1)
        align = 16.0
        first = jnp.minimum(jnp.floor(starts / align) * align, float(cap - SCATTER_WINDOW))
        fits = jnp.where(ends - first <= SCATTER_WINDOW, 1.0, 0.0)
        win_ref[...] = first.astype(jnp.int32)
        ok = [jnp.min(fits[i * n_exp:(i + 1) * n_exp], axis=0, keepdims=True) for i in range(e // n_exp)]
        ok_ref[...] = jnp.concatenate(ok, axis=0).astype(jnp.int32)


def _route(aff, tri, *, cap, windows):
    b, e, n = aff.shape
    whole = lambda shape: pl.BlockSpec(shape, lambda i: (0,) * len(shape))
    out_shape = [jax.ShapeDtypeStruct((b * e, n), jnp.int32), jax.ShapeDtypeStruct((b, n, ROUTE_ROWS), F32)]
    if windows:
        n_tiles = n // SCATTER_TILE
        out_shape += [jax.ShapeDtypeStruct((b * e, n_tiles), jnp.int32), jax.ShapeDtypeStruct((b, n_tiles), jnp.int32)]
    outs = pl.pallas_call(
        functools.partial(_route_kernel, cap=cap, n_exp=e),
        grid=(1,),
        in_specs=[whole((b * e, n)), whole((LANES, LANES))],
        out_specs=[whole(s.shape) for s in out_shape],
        out_shape=out_shape,
        compiler_params=_cparams("arbitrary"),
        name="route",
    )(aff.reshape(b * e, n), tri)
    pos, rt = outs[0].reshape(b, e, n), outs[1]
    if not windows:
        return pos, rt
    win = outs[2].reshape(b, e, n_tiles).transpose(0, 2, 1)
    return pos, rt, win, outs[3]


def _gather_kernel(pos_ref, h_ref, o_ref, *, cap, group):
    j = pl.program_id(1)
    n = h_ref.shape[1]
    slot = lax.broadcasted_iota(jnp.int32, (cap, n), 0)
    sel = [jnp.where(pos_ref[0, pl.ds(j * group + g, 1), :] == slot, 1.0, 0.0).astype(BF16) for g in range(group)]
    o_ref[0] = _dot(jnp.concatenate(sel, axis=0), h_ref[0]).astype(BF16)


def _gather(pos, h, *, cap, group):
    b, e, n = pos.shape
    d = h.shape[2]
    return pl.pallas_call(
        functools.partial(_gather_kernel, cap=cap, group=group),
        grid=(b, e // group),
        in_specs=[pl.BlockSpec((1, e, n), lambda i, j: (i, 0, 0)), pl.BlockSpec((1, n, d), lambda i, j: (i, 0, 0))],
        out_specs=pl.BlockSpec((1, group * cap, d), lambda i, j: (i, j, 0)),
        out_shape=jax.ShapeDtypeStruct((b, e * cap, d), BF16),
        compiler_params=_cparams("arbitrary", "arbitrary"),
        name="moe_gather",
    )(pos, h)


def _gather_win_kernel(win_ref, ok_ref, pos_ref, h_ref, o_ref, *, tiles):
    i, j = pl.program_id(0), pl.program_id(1)
    cap = o_ref.shape[2]

    @pl.when(j == 0)
    def _():
        o_ref[...] = jnp.zeros_like(o_ref)

    kts = [j * tiles + u for u in range(tiles)]
    cols = [slice(u * SCATTER_TILE, (u + 1) * SCATTER_TILE) for u in range(tiles)]

    def windows():
        slot = lax.broadcasted_iota(jnp.int32, (SCATTER_WINDOW, SCATTER_TILE), 0)
        firsts, zs = [], []
        for kt, c in zip(kts, cols):
            pos = pos_ref[0, :, c]
            first = [win_ref[i, kt, e] for e in range(N_EXPERTS)]
            pick = [jnp.where(pos[e:e + 1] - first[e] == slot, 1.0, 0.0).astype(BF16) for e in range(N_EXPERTS)]
            zs.append(_dot(jnp.concatenate(pick, axis=0), h_ref[0, c]))
            firsts.append(first)
        for first, z in zip(firsts, zs):
            for e in range(N_EXPERTS):
                rows = pl.ds(pl.multiple_of(first[e], 16), SCATTER_WINDOW)
                o_ref[0, e, rows, :] += z[e * SCATTER_WINDOW:(e + 1) * SCATTER_WINDOW].astype(BF16)

    def full():
        slot = lax.broadcasted_iota(jnp.int32, (cap, SCATTER_TILE), 0)
        for c in cols:
            for e in range(N_EXPERTS):
                pick = jnp.where(pos_ref[0, e:e + 1, c] == slot, 1.0, 0.0).astype(BF16)
                o_ref[0, e] += _dot(pick, h_ref[0, c]).astype(BF16)

    all_fit = functools.reduce(jnp.minimum, [ok_ref[i, kt] for kt in kts])
    lax.cond(all_fit > 0, windows, full)


def _gather_win(win, ok, pos, h, *, cap, tiles):
    b, e, n = pos.shape
    d = h.shape[2]
    smem = pl.BlockSpec(memory_space=pltpu.SMEM)
    step = tiles * SCATTER_TILE
    out = pl.pallas_call(
        functools.partial(_gather_win_kernel, tiles=tiles),
        grid=(b, n // step),
        in_specs=[smem, smem,
                  pl.BlockSpec((1, e, step), lambda i, j: (i, 0, j)),
                  pl.BlockSpec((1, step, d), lambda i, j: (i, j, 0))],
        out_specs=pl.BlockSpec((1, e, cap, d), lambda i, j: (i, 0, 0, 0)),
        out_shape=jax.ShapeDtypeStruct((b, e, cap, d), BF16),
        compiler_params=_cparams("arbitrary", "arbitrary"),
        name="moe_gather_win",
    )(win, ok, pos, h)
    return out.reshape(b, e * cap, d)


def _expert_kernel(*refs, n_sets):
    x_refs, (wg_ref, wu_ref, wd_ref) = refs[:n_sets], refs[n_sets:n_sets + 3]
    o_refs = refs[n_sets + 3:2 * n_sets + 3]
    d = x_refs[0].shape[2]
    rows = [r.shape[0] * r.shape[1] for r in x_refs]
    xs = [r[...].reshape(n, d) for r, n in zip(x_refs, rows)]
    x = xs[0] if n_sets == 1 else jnp.concatenate(xs, axis=0)
    wg, wu, wd = (w[0, 0].astype(BF16) for w in (wg_ref, wu_ref, wd_ref))
    half = x.shape[0] // 2
    halves = [x[:half], x[half:]]
    up = [(_dot(xh, wg), _dot(xh, wu)) for xh in halves]
    y = jnp.concatenate([_dot((a * jax.nn.sigmoid(a) * u).astype(BF16), wd).astype(BF16) for a, u in up], axis=0)
    start = 0
    for o_ref, n in zip(o_refs, rows):
        o_ref[...] = y[start:start + n].reshape(o_ref.shape)
        start += n


def _experts(xgs, caps, layer, wg, wu, wd, *, nb):
    b, _, d = xgs[0].shape
    _, e, _, f = wg.shape
    wspec = lambda r, c: pl.BlockSpec((1, 1, r, c), lambda i, j: (layer, i, 0, 0))
    xspecs = [pl.BlockSpec((nb, cap, d), lambda i, j: (j, i, 0)) for cap in caps]
    return pl.pallas_call(
        functools.partial(_expert_kernel, n_sets=len(xgs)),
        grid=(e, b // nb),
        in_specs=xspecs + [wspec(d, f), wspec(d, f), wspec(f, d)],
        out_specs=xspecs,
        out_shape=[jax.ShapeDtypeStruct(xg.shape, BF16) for xg in xgs],
        compiler_params=_cparams("arbitrary", "arbitrary"),
        name="moe_experts",
    )(*xgs, wg, wu, wd)


def _combine_matrix(rt, first, width):
    slot = lax.broadcasted_iota(jnp.int32, (rt.shape[0], width), 1).astype(F32)
    cols = [jnp.where(rt[:, e:e + 1] - first[e] == slot, rt[:, N_EXPERTS + e:N_EXPERTS + e + 1], 0.0).astype(BF16)
            for e in range(N_EXPERTS)]
    return jnp.concatenate(cols, axis=1)


def _scatter_kernel(*refs, cap, final, windowed):
    if windowed:
        win_ref, ok_ref, x_ref, rt_ref, y_ref, g2_ref, fg_ref, o_ref = refs
    else:
        x_ref, rt_ref, y_ref, g2_ref, fg_ref, o_ref = refs
    i, j = pl.program_id(0), pl.program_id(1)
    tn = x_ref.shape[1]

    def finish(rows, y):
        x = x_ref[0, rows] + g2_ref[0] * y
        if final:
            ms = jnp.mean(x * x, axis=-1, keepdims=True)
            x = x * lax.rsqrt(ms + EPS) * fg_ref[...]
        o_ref[0, rows] = x

    def full():
        rows = slice(0, tn)
        finish(rows, _dot(_combine_matrix(rt_ref[0], [0.0] * N_EXPERTS, cap), y_ref[0]))

    if not windowed:
        full()
        return
    subs = tn // SCATTER_TILE
    tiles = [j * subs + u for u in range(subs)]

    def windows():
        def contract(u):
            rt = rt_ref[0, u * SCATTER_TILE:(u + 1) * SCATTER_TILE]
            first = [win_ref[i, tiles[u], e] for e in range(N_EXPERTS)]
            picked = [y_ref[0, pl.ds(pl.multiple_of(e * cap + first[e], 16), SCATTER_WINDOW), :]
                      for e in range(N_EXPERTS)]
            a = _combine_matrix(rt, [f.astype(F32) for f in first], SCATTER_WINDOW)
            return _dot(a, jnp.concatenate(picked, axis=0))

        ys = {0: contract(0)}
        for u in range(subs):
            if u + 1 < subs:
                ys[u + 1] = contract(u + 1)
            finish(slice(u * SCATTER_TILE, (u + 1) * SCATTER_TILE), ys.pop(u))

    all_fit = functools.reduce(jnp.minimum, [ok_ref[i, tile] for tile in tiles])
    lax.cond(all_fit > 0, windows, full)


def _scatter(x, rt, y, g2, fg, *, cap, final, tn, windows=None):
    b, t, d = x.shape
    smem = pl.BlockSpec(memory_space=pltpu.SMEM)
    windowed = windows is not None
    return pl.pallas_call(
        functools.partial(_scatter_kernel, cap=cap, final=final, windowed=windowed),
        grid=(b, t // tn),
        in_specs=([smem, smem] if windowed else []) + [
            pl.BlockSpec((1, tn, d), lambda i, j: (i, j, 0)),
            pl.BlockSpec((1, tn, ROUTE_ROWS), lambda i, j: (i, j, 0)),
            pl.BlockSpec((1, N_EXPERTS * cap, d), lambda i, j: (i, 0, 0)),
            pl.BlockSpec((1, 1, d), lambda i, j: (i, 0, 0)),
            pl.BlockSpec((1, d), lambda i, j: (0, 0))],
        out_specs=pl.BlockSpec((1, tn, d), lambda i, j: (i, j, 0)),
        out_shape=jax.ShapeDtypeStruct((b, t, d), F32),
        compiler_params=_cparams("arbitrary", "arbitrary"),
        name="moe_scatter",
    )(*(windows if windowed else ()), x, rt, y, g2, fg)


def _capacity(t):
    return CAPACITY_FACTOR * t // N_EXPERTS


def _dft_tables(n):
    k = np.arange(n, dtype=np.int64)
    ang = 2.0 * np.pi * ((k[:, None] * k[None, :]) % n).astype(np.float64) / n
    return np.cos(ang), np.sin(ang)


def _channel_dft():
    c, s = _dft_tables(FOURIER_GROUP_CH)
    eye = np.eye(FOURIER_GROUPS)
    return np.concatenate([np.kron(eye, c), np.kron(eye, s)], axis=1)


def _rope_tables(t):
    rows = t // GRID_W
    r, col = jnp.meshgrid(jnp.arange(rows), jnp.arange(GRID_W), indexing="ij")
    half = HEAD_DIM // 2
    inv = ROPE_THETA ** (-jnp.arange(0, half, 2, dtype=F32) / half)
    ang = jnp.concatenate([r.reshape(-1, 1).astype(F32) * inv, col.reshape(-1, 1).astype(F32) * inv], axis=-1)
    cos = jnp.repeat(jnp.cos(ang), 2, axis=1)
    sin = jnp.repeat(jnp.sin(ang), 2, axis=1) * jnp.tile(jnp.asarray([-1.0, 1.0], F32), half)
    return jnp.tile(cos, (1, HEAD_PAD)), jnp.tile(sin, (1, HEAD_PAD))


def kernel(x, c, ctx, c_ctx, w_ada, b_ada, norm1_g, w_in, q_norm_g, k_norm_g, sink, w_br_fourier, w_br_global,
           w_br_window, w_out, norm2_g, w_router, w_gate_e, w_up_e, w_down_e, final_g):
    b, t, d = x.shape
    l_ctx = ctx.shape[1]
    depth = w_ada.shape[0]
    cap_t, cap_c = _capacity(t), _capacity(l_ctx)

    cos_t, sin_t = _rope_tables(t)
    cos_c, sin_c = jnp.ones((b * l_ctx, LANES), F32), jnp.zeros((b * l_ctx, LANES), F32)
    head_avg = jnp.asarray(np.kron(np.eye(MXU_DIM // HEAD_DIM), np.full((HEAD_DIM, HEAD_DIM), 1.0 / HEAD_DIM)), BF16)
    cdft = jnp.asarray(_channel_dft(), F32)
    cn_t, sn_t = (jnp.asarray(a, F32) for a in _dft_tables(t))
    cn_c, sn_c = (jnp.asarray(a, F32) for a in _dft_tables(l_ctx))
    tri = jnp.asarray(np.triu(np.ones((LANES, LANES))), BF16)

    cvec = jnp.concatenate([c, c_ctx[None], jnp.zeros((MOD_ROWS - b - 1, d), F32)], axis=0)
    mods = _ada(cvec, w_ada, b_ada)

    xc = ctx
    for l in range(depth):
        need_ctx = l < depth - 1
        final = l == depth - 1
        lat = [mods[l, :b, i * d:(i + 1) * d].reshape(b, 1, d) for i in range(6)]
        cmod = [jnp.broadcast_to(mods[l, b, i * d:(i + 1) * d].reshape(1, 1, d), (b, 1, d)) for i in range(6)]
        w_mix = w_in[l, :, :MIX_W].astype(BF16)
        w_gate = w_in[l, :, MIX_W:].astype(BF16)
        n1, n2 = norm1_g[l].reshape(1, d), norm2_g[l].reshape(1, d)
        qn = jnp.tile(q_norm_g[l], G_HEADS).reshape(1, GQ_W)
        kn = jnp.tile(k_norm_g[l], G_KV).reshape(1, GKV_W)
        wbf, wbg, wbw = (w[l].astype(BF16) for w in (w_br_fourier, w_br_global, w_br_window))
        wout = w_out[l].astype(BF16)
        wr = jnp.pad(w_router[l], ((0, 0), (0, LANES - N_EXPERTS))).astype(BF16)
        fg = final_g.reshape(1, d)
        merge_w = (n1, n2, w_gate, wbf, wbg, wbw, wout, wr)

        flat = lambda a: a.reshape(1, b * l_ctx, a.shape[-1])
        unflat = lambda a: a.reshape(b, l_ctx, a.shape[-1])
        cmod1 = [m[:1] for m in cmod]
        ctx_tm = min(b * l_ctx, 1024)
        cfcs, cqg, ckg, cvg, cqw, ckw, cvw = map(unflat, _inproj(
            flat(xc), cmod1[0], cmod1[1], n1, w_mix, cos_c, sin_c, head_avg, cdft, qn, kn,
            rope=False, tm=ctx_tm, chunks=8))
        xgs, caps = [], []
        if need_ctx:
            cf_mix = _fourier(cfcs, cn_c, sn_c, tr=l_ctx)
            oc_g, oc_w = _ctx_attn(sink[l], cqg, ckg, cvg, cqw, ckw, cvw)
            xc1, hc2, caff = _merge(flat(xc), flat(cf_mix), flat(oc_g), flat(oc_w), *cmod1[:5], *merge_w,
                                    tm=ctx_tm, chunks=2)
            xc1, hc2 = unflat(xc1), unflat(hc2)
            caff = caff.reshape(N_EXPERTS, b, l_ctx).transpose(1, 0, 2)
            cpos, crt = _route(caff, tri, cap=cap_c, windows=False)
            xgs.append(_gather(cpos, hc2, cap=cap_c, group=N_EXPERTS))
            caps.append(cap_c)

        fcs, qg, kg, vg, qw, kw, vw = _inproj(x, lat[0], lat[1], n1, w_mix, cos_t, sin_t, head_avg, cdft, qn, kn,
                                              rope=True, tm=1024, chunks=8)
        f_mix = _fourier(fcs, cn_t, sn_t, tr=1024)
        o_g = _global_attn(qg, kg, vg, ckg, cvg, tq=256)
        o_w = _window_attn(sink[l], qw, kw, vw, ckw, cvw)
        x1, h2, aff = _merge(x, f_mix, o_g, o_w, *lat[:5], *merge_w, tm=1024, chunks=2)
        pos, rt, win, win_ok = _route(aff, tri, cap=cap_t, windows=True)
        xgs.insert(0, _gather_win(win, win_ok, pos, h2, cap=cap_t, tiles=4))
        caps.insert(0, cap_t)

        ys = _experts(xgs, caps, l, w_gate_e, w_up_e, w_down_e, nb=4)
        x = _scatter(x1, rt, ys[0], lat[5], fg, cap=cap_t, final=final, tn=1024, windows=(win, win_ok))
        if need_ctx:
            xc = _scatter(xc1, crt, ys[1], cmod[5], fg, cap=cap_c, final=False, tn=l_ctx)
    return x
```

```python
import functools

import numpy as np
import jax
import jax.numpy as jnp
from jax import lax
from jax.experimental import pallas as pl
from jax.experimental.pallas import tpu as pltpu

F32 = jnp.float32
BF16 = jnp.bfloat16

D_MODEL = 1024
HEAD_DIM = 64
GRID_W = 64
FOURIER_GROUPS = 4
FOURIER_GROUP_CH = 64
FOURIER_W = FOURIER_GROUPS * FOURIER_GROUP_CH
G_HEADS, G_KV = 8, 2
G_GROUP = G_HEADS // G_KV
W_HEADS, W_KV = 4, 2
W_GROUP = W_HEADS // W_KV
WINDOW = 128
Q_BLOCK = 128
N_BRANCH = 3
GQ_W = G_HEADS * HEAD_DIM
GKV_W = G_KV * HEAD_DIM
WQ_W = W_HEADS * HEAD_DIM
WKV_W = W_KV * HEAD_DIM
OFF_F = 0
OFF_GQ = OFF_F + FOURIER_W
OFF_GK = OFF_GQ + GQ_W
OFF_GV = OFF_GK + GKV_W
OFF_WQ = OFF_GV + GKV_W
OFF_WK = OFF_WQ + WQ_W
OFF_WV = OFF_WK + WKV_W
MIX_W = OFF_WV + WKV_W
GATE_W = N_BRANCH * D_MODEL
ROPE_THETA = 10000.0
N_EXPERTS = 16
CAPACITY_FACTOR = 2
EPS = 1e-6
NEG_INF = -1e30
LANES = 128
HEAD_PAD = LANES // HEAD_DIM
MXU_DIM = 256
MOD_ROWS = 16
SCATTER_TILE = 256
SCATTER_WINDOW = 64
ROUTE_ROWS = 128
WINDOW_TILE = 8
WINDOW_LOOKAHEAD = 2
GLOBAL_CHAINS_PER_KV = 1
VMEM_LIMIT = 56 * 1024 * 1024


def _cparams(*sem):
    return pltpu.CompilerParams(dimension_semantics=sem, vmem_limit_bytes=VMEM_LIMIT)


def _norm_mod(x, g, sh, sc):
    ms = jnp.mean(x * x, axis=-1, keepdims=True)
    return (x * lax.rsqrt(ms + EPS) * g) * (1.0 + sc) + sh


def _dot(a, b):
    return jnp.dot(a, b, preferred_element_type=F32)


def _dot_nt(a, b):
    return lax.dot_general(a, b, (((1,), (1,)), ((), ())), preferred_element_type=F32)


def _ada_kernel(c_ref, w_ref, b_ref, o_ref):
    c = c_ref[...]
    s = (c * jax.nn.sigmoid(c)).astype(BF16)
    o_ref[0] = _dot(s, w_ref[0].astype(BF16)) + b_ref[0]


def _ada(cvec, w_ada, b_ada):
    depth, d, n = w_ada.shape
    tn = 1536
    return pl.pallas_call(
        _ada_kernel,
        grid=(depth, n // tn),
        in_specs=[pl.BlockSpec((MOD_ROWS, d), lambda l, j: (0, 0)),
                  pl.BlockSpec((1, d, tn), lambda l, j: (l, 0, j)),
                  pl.BlockSpec((1, 1, tn), lambda l, j: (l, 0, j))],
        out_specs=pl.BlockSpec((1, MOD_ROWS, tn), lambda l, j: (l, 0, j)),
        out_shape=jax.ShapeDtypeStruct((depth, MOD_ROWS, n), F32),
        compiler_params=_cparams("arbitrary", "arbitrary"),
        name="ada",
    )(cvec, w_ada, b_ada.reshape(depth, 1, n))


def _rope(x, cos, sin_signed, even):
    outs = []
    for j in range(x.shape[1] // LANES):
        xb = x[:, j * LANES:(j + 1) * LANES]
        swap = jnp.where(even, pltpu.roll(xb, LANES - 1, 1), pltpu.roll(xb, 1, 1))
        outs.append(xb * cos + swap * sin_signed)
    return outs[0] if len(outs) == 1 else jnp.concatenate(outs, axis=1)


def _head_mean_square(z, bd):
    zz = (z * z).astype(BF16)
    blk = bd.shape[0]
    if z.shape[1] < blk:
        return _dot(zz, bd[:z.shape[1], :z.shape[1]])
    outs = [_dot(zz[:, j * blk:(j + 1) * blk], bd) for j in range(z.shape[1] // blk)]
    return outs[0] if len(outs) == 1 else jnp.concatenate(outs, axis=1)


def _pad_heads_f32(x, fill):
    blk = jnp.full((x.shape[0], LANES - HEAD_DIM), fill, F32)
    parts = []
    for h in range(x.shape[1] // HEAD_DIM):
        parts += [x[:, h * HEAD_DIM:(h + 1) * HEAD_DIM], blk]
    return jnp.concatenate(parts, axis=1)


def _pad_heads(x, fill):
    return _pad_heads_f32(x, fill).astype(BF16)


def _inproj_kernel(x_ref, sh_ref, sc_ref, g_ref, w_ref, cos_ref, sin_ref, bd_ref, cdft_ref, qn_ref, kn_ref,
                   fcs_ref, qg_ref, kg_ref, vg_ref, qw_ref, kw_ref, vw_ref, *, rope, chunks):
    tm = x_ref.shape[1]
    rows = [slice(c * (tm // chunks), (c + 1) * (tm // chunks)) for c in range(chunks)]
    bd = bd_ref[...]
    scale = HEAD_DIM ** -0.5

    def project(r):
        return _dot(_norm_mod(x_ref[0, r], g_ref[...], sh_ref[0], sc_ref[0]).astype(BF16), w_ref[...])

    zs = {0: project(rows[0])}
    for c, r in enumerate(rows):
        if c + 1 < chunks:
            zs[c + 1] = project(rows[c + 1])
        z = zs.pop(c)
        fcs_ref[0, r] = _dot(z[:, OFF_F:OFF_F + FOURIER_W].astype(BF16), cdft_ref[...].astype(BF16)).astype(BF16)
        q = z[:, OFF_GQ:OFF_GQ + GQ_W]
        k = z[:, OFF_GK:OFF_GK + GKV_W]
        q = q * lax.rsqrt(_head_mean_square(q, bd) + EPS) * qn_ref[...]
        k = k * lax.rsqrt(_head_mean_square(k, bd) + EPS) * kn_ref[...]
        qw = z[:, OFF_WQ:OFF_WQ + WQ_W]
        kw = z[:, OFF_WK:OFF_WK + WKV_W]
        if rope:
            cos, sin = cos_ref[r], sin_ref[r]
            even = (lax.broadcasted_iota(jnp.int32, cos.shape, 1) % 2) == 0
            q, k = _rope(q, cos, sin, even), _rope(k, cos, sin, even)
            qw, kw = _rope(qw, cos, sin, even), _rope(kw, cos, sin, even)
        qg_ref[0, r] = _pad_heads(q * scale, 0.0)
        kg_ref[0, r] = _pad_heads(k, 0.0)
        vg_ref[0, r] = _pad_heads(z[:, OFF_GV:OFF_GV + GKV_W], 1.0)
        qw_ref[0, r] = _pad_heads(qw * scale, 0.0)
        kw_ref[0, r] = _pad_heads(kw, 0.0)
        vw_ref[0, r] = _pad_heads(z[:, OFF_WV:OFF_WV + WKV_W], 1.0)


def _inproj(x, sh, sc, g, w_mix, cos_t, sin_t, bd, cdft, qn, kn, *, rope, tm, chunks):
    b, t, d = x.shape
    tok = lambda w: pl.BlockSpec((1, tm, w), lambda i, j: (i, j, 0))
    row = pl.BlockSpec((1, 1, d), lambda i, j: (i, 0, 0))
    const = lambda shape: pl.BlockSpec(shape, lambda i, j: (0,) * len(shape))
    widths = (2 * FOURIER_W,) + tuple(HEAD_PAD * w for w in (GQ_W, GKV_W, GKV_W, WQ_W, WKV_W, WKV_W))
    out_specs = [tok(w) for w in widths]
    out_shape = [jax.ShapeDtypeStruct((b, t, w), BF16) for w in widths]
    return pl.pallas_call(
        functools.partial(_inproj_kernel, rope=rope, chunks=chunks),
        grid=(b, t // tm),
        in_specs=[tok(d), row, row, const((1, d)), const((d, MIX_W)),
                  pl.BlockSpec((tm, LANES), lambda i, j: (j, 0)),
                  pl.BlockSpec((tm, LANES), lambda i, j: (j, 0)),
                  const((MXU_DIM, MXU_DIM)), const((FOURIER_W, 2 * FOURIER_W)),
                  const((1, GQ_W)), const((1, GKV_W))],
        out_specs=out_specs,
        out_shape=out_shape,
        compiler_params=_cparams("arbitrary", "arbitrary"),
        name="inproj_rope" if rope else "inproj_ctx",
    )(x, sh, sc, g, w_mix, cos_t, sin_t, bd, cdft, qn, kn)


def _fourier_kernel(cn_ref, sn_ref, fcs_ref, o_ref, cn_s, sn_s, *, scale):
    @pl.when(pl.program_id(1) == 0)
    def _():
        cn_s[...] = cn_ref[...].astype(BF16)
        sn_s[...] = sn_ref[...].astype(BF16)

    fcs = fcs_ref[0]
    re = _dot(cn_s[...], fcs[:, :FOURIER_W]) - _dot(sn_s[...], fcs[:, FOURIER_W:])
    o_ref[0] = (re * scale).astype(BF16)


def _fourier(fcs, cn, sn, *, tr):
    b, n, _ = fcs.shape
    scale = float((n * FOURIER_GROUP_CH) ** -0.5)
    return pl.pallas_call(
        functools.partial(_fourier_kernel, scale=scale),
        grid=(n // tr, b),
        in_specs=[pl.BlockSpec((tr, n), lambda r, i: (r, 0)),
                  pl.BlockSpec((tr, n), lambda r, i: (r, 0)),
                  pl.BlockSpec((1, n, 2 * FOURIER_W), lambda r, i: (i, 0, 0))],
        out_specs=pl.BlockSpec((1, tr, FOURIER_W), lambda r, i: (i, r, 0)),
        out_shape=jax.ShapeDtypeStruct((b, n, FOURIER_W), BF16),
        scratch_shapes=[pltpu.VMEM((tr, n), BF16), pltpu.VMEM((tr, n), BF16)],
        compiler_params=_cparams("arbitrary", "arbitrary"),
        name="fourier",
    )(cn, sn, fcs)


def _head(x, h):
    return x[:, h * LANES:(h + 1) * LANES]


def _stack_heads(q, first, count):
    return jnp.concatenate([_head(q, first + g) for g in range(count)], axis=0)


def _softmax_pv(parts, extra=None):
    m = functools.reduce(jnp.maximum, [jnp.max(s, axis=-1, keepdims=True) for s, _ in parts])
    if extra is not None:
        m = jnp.maximum(m, extra)
    acc = 0.0
    for s, v in parts:
        acc = acc + _dot(jnp.exp((s - m).astype(BF16)), v)
    den = acc[:, HEAD_DIM:HEAD_DIM + 1]
    if extra is not None:
        den = den + jnp.exp(extra - m)
    return acc[:, :HEAD_DIM] / den


def _unstack_heads(o_list, tq, count):
    cols = []
    for o in o_list:
        cols += [o[g * tq:(g + 1) * tq] for g in range(count)]
    return jnp.concatenate(cols, axis=1)


def _global_attn_kernel(q_ref, kl_ref, vl_ref, kc_ref, vc_ref, o_ref):
    q = q_ref[0]
    tq = q.shape[0]
    per_chain = G_GROUP // GLOBAL_CHAINS_PER_KV
    parts = []
    for kv in range(G_KV):
        for c in range(GLOBAL_CHAINS_PER_KV):
            qs = _stack_heads(q, kv * G_GROUP + c * per_chain, per_chain)
            parts.append([(_dot_nt(qs, _head(kc_ref[0], kv)), _head(vc_ref[0], kv)),
                          (_dot_nt(qs, _head(kl_ref[0], kv)), _head(vl_ref[0], kv))])
    outs = [_softmax_pv(p) for p in parts]
    o_ref[0] = _unstack_heads(outs, tq, per_chain).astype(BF16)


def _global_attn(q, kl, vl, kc, vc, *, tq):
    b, t, _ = q.shape
    l = kc.shape[1]
    full = lambda n: pl.BlockSpec((1, n, HEAD_PAD * GKV_W), lambda i, j: (i, 0, 0))
    return pl.pallas_call(
        _global_attn_kernel,
        grid=(b, t // tq),
        in_specs=[pl.BlockSpec((1, tq, HEAD_PAD * GQ_W), lambda i, j: (i, j, 0)), full(t), full(t), full(l), full(l)],
        out_specs=pl.BlockSpec((1, tq, GQ_W), lambda i, j: (i, j, 0)),
        out_shape=jax.ShapeDtypeStruct((b, t, GQ_W), BF16),
        compiler_params=_cparams("arbitrary", "arbitrary"),
        name="global_attn",
    )(q, kl, vl, kc, vc)


def _sink_column(sink_ref, kv, rows_per_head):
    r = lax.broadcasted_iota(jnp.int32, (W_GROUP * rows_per_head, 1), 0)
    col = jnp.full(r.shape, sink_ref[kv * W_GROUP], F32)
    for g in range(1, W_GROUP):
        col = jnp.where(r >= g * rows_per_head, sink_ref[kv * W_GROUP + g], col)
    return col


def _window_attn_kernel(sink_ref, q_ref, kp_ref, k0_ref, kn_ref, vp_ref, v0_ref, vn_ref, kc_ref, vc_ref, o_ref,
                        *, n_tiles):
    j = pl.program_id(1)
    q = q_ref[0]
    kb = jnp.concatenate([kp_ref[0], k0_ref[0], kn_ref[0]], axis=0)
    vb = jnp.concatenate([vp_ref[0], v0_ref[0], vn_ref[0]], axis=0)
    rows = W_GROUP * Q_BLOCK
    qpos = lax.broadcasted_iota(jnp.int32, (rows, 3 * Q_BLOCK), 0) % Q_BLOCK
    kpos = lax.broadcasted_iota(jnp.int32, (rows, 3 * Q_BLOCK), 1)
    dist = jnp.abs(qpos - kpos + Q_BLOCK)
    in_window = dist <= WINDOW
    outside = jnp.full(dist.shape, WINDOW + 1, jnp.int32)
    first_ok = jnp.where((kpos < Q_BLOCK) & (j == 0), outside, dist) <= WINDOW
    last_ok = jnp.where((kpos >= 2 * Q_BLOCK) & (j == n_tiles - 1), outside, dist) <= WINDOW
    def scores(n):
        i, kv = divmod(n, W_KV)
        valid = first_ok if i == 0 else (last_ok if i == WINDOW_TILE - 1 else in_window)
        band = slice(i * Q_BLOCK, (i + 3) * Q_BLOCK)
        qs = _stack_heads(q[i * Q_BLOCK:(i + 1) * Q_BLOCK], kv * W_GROUP, W_GROUP)
        s_b = jnp.where(valid, _dot_nt(qs, _head(kb, kv)[band]), NEG_INF)
        return [(_dot_nt(qs, _head(kc_ref[0], kv)), _head(vc_ref[0], kv)), (s_b, _head(vb, kv)[band])]

    sinks = [_sink_column(sink_ref, kv, Q_BLOCK) for kv in range(W_KV)]
    n_chains = WINDOW_TILE * W_KV
    pending = {n: scores(n) for n in range(min(WINDOW_LOOKAHEAD, n_chains))}
    outs = []
    for n in range(n_chains):
        if n + WINDOW_LOOKAHEAD < n_chains:
            pending[n + WINDOW_LOOKAHEAD] = scores(n + WINDOW_LOOKAHEAD)
        outs.append(_softmax_pv(pending.pop(n), sinks[n % W_KV]))
    blocks = [_unstack_heads(outs[i * W_KV:(i + 1) * W_KV], Q_BLOCK, W_GROUP) for i in range(WINDOW_TILE)]
    o_ref[0] = jnp.concatenate(blocks, axis=0).astype(BF16)


def _window_attn(sink, q, k, v, kc, vc):
    b, t, _ = q.shape
    l = kc.shape[1]
    nb = t // Q_BLOCK
    tile = WINDOW_TILE * Q_BLOCK
    kvw = HEAD_PAD * WKV_W
    edge = lambda f: pl.BlockSpec((1, Q_BLOCK, kvw), lambda i, j: (i, f(j), 0))
    prev = lambda j: jnp.maximum(j * WINDOW_TILE - 1, 0)
    nxt = lambda j: jnp.minimum((j + 1) * WINDOW_TILE, nb - 1)
    mid = pl.BlockSpec((1, tile, kvw), lambda i, j: (i, j, 0))
    full = pl.BlockSpec((1, l, kvw), lambda i, j: (i, 0, 0))
    return pl.pallas_call(
        functools.partial(_window_attn_kernel, n_tiles=t // tile),
        grid=(b, t // tile),
        in_specs=[pl.BlockSpec(memory_space=pltpu.SMEM),
                  pl.BlockSpec((1, tile, HEAD_PAD * WQ_W), lambda i, j: (i, j, 0)),
                  edge(prev), mid, edge(nxt), edge(prev), mid, edge(nxt), full, full],
        out_specs=pl.BlockSpec((1, tile, WQ_W), lambda i, j: (i, j, 0)),
        out_shape=jax.ShapeDtypeStruct((b, t, WQ_W), BF16),
        compiler_params=_cparams("arbitrary", "arbitrary"),
        name="window_attn",
    )(sink, q, k, k, k, v, v, v, kc, vc)


def _ctx_attn_kernel(sink_ref, qg_ref, kg_ref, vg_ref, qw_ref, kw_ref, vw_ref, og_ref, ow_ref):
    l = qg_ref.shape[1]
    qg, qw = qg_ref[0], qw_ref[0]
    g_parts = [[(_dot_nt(_stack_heads(qg, kv * G_GROUP, G_GROUP), _head(kg_ref[0], kv)), _head(vg_ref[0], kv))]
               for kv in range(G_KV)]
    w_parts = [[(_dot_nt(_stack_heads(qw, kv * W_GROUP, W_GROUP), _head(kw_ref[0], kv)), _head(vw_ref[0], kv))]
               for kv in range(W_KV)]
    og_ref[0] = _unstack_heads([_softmax_pv(p) for p in g_parts], l, G_GROUP).astype(BF16)
    outs = [_softmax_pv(p, _sink_column(sink_ref, kv, l)) for kv, p in enumerate(w_parts)]
    ow_ref[0] = _unstack_heads(outs, l, W_GROUP).astype(BF16)


def _ctx_attn(sink, qg, kg, vg, qw, kw, vw):
    b, l, _ = qg.shape
    spec = lambda w: pl.BlockSpec((1, l, w), lambda i: (i, 0, 0))
    padded = lambda w: spec(HEAD_PAD * w)
    return pl.pallas_call(
        _ctx_attn_kernel,
        grid=(b,),
        in_specs=[pl.BlockSpec(memory_space=pltpu.SMEM),
                  padded(GQ_W), padded(GKV_W), padded(GKV_W), padded(WQ_W), padded(WKV_W), padded(WKV_W)],
        out_specs=[spec(GQ_W), spec(WQ_W)],
        out_shape=[jax.ShapeDtypeStruct((b, l, GQ_W), BF16), jax.ShapeDtypeStruct((b, l, WQ_W), BF16)],
        compiler_params=_cparams("arbitrary"),
        name="ctx_attn",
    )(sink, qg, kg, vg, qw, kw, vw)


def _merge_kernel(x_ref, f_ref, og_ref, ow_ref, sh1_ref, sc1_ref, g1_ref, sh2_ref, sc2_ref,
                  n1_ref, n2_ref, wgate_ref, wbf_ref, wbg_ref, wbw_ref, wout_ref, wr_ref,
                  x1_ref, h2_ref, aff_ref, *, chunks):
    tm = x_ref.shape[1]
    rows = [slice(c * (tm // chunks), (c + 1) * (tm // chunks)) for c in range(chunks)]
    d = D_MODEL

    def project(r):
        h = _norm_mod(x_ref[0, r], n1_ref[...], sh1_ref[0], sc1_ref[0]).astype(BF16)
        branches = (_dot(f_ref[0, r], wbf_ref[...]), _dot(og_ref[0, r], wbg_ref[...]), _dot(ow_ref[0, r], wbw_ref[...]))
        return [(_dot(h, wgate_ref[:, i * d:(i + 1) * d]), br) for i, br in enumerate(branches)]

    heads = {0: project(rows[0])}
    for c, r in enumerate(rows):
        if c + 1 < chunks:
            heads[c + 1] = project(rows[c + 1])
        (g0, b0), (g1, b1), (g2, b2) = heads.pop(c)
        m = jax.nn.sigmoid(g0) * b0 + jax.nn.sigmoid(g1) * b1 + jax.nn.sigmoid(g2) * b2
        x1 = x_ref[0, r] + g1_ref[0] * _dot(m.astype(BF16), wout_ref[...])
        x1_ref[0, r] = x1
        h2 = _norm_mod(x1, n2_ref[...], sh2_ref[0], sc2_ref[0]).astype(BF16)
        h2_ref[0, r] = h2
        logits = _dot(h2, wr_ref[...]).T[:N_EXPERTS]
        e = jnp.exp(logits - jnp.max(logits, axis=0, keepdims=True))
        aff_ref[0, :, r] = e / jnp.sum(e, axis=0, keepdims=True)


def _merge(x, f, og, ow, sh1, sc1, g1, sh2, sc2, n1, n2, wgate, wbf, wbg, wbw, wout, wr, *, tm, chunks):
    b, t, d = x.shape
    tok = lambda w: pl.BlockSpec((1, tm, w), lambda i, j: (i, j, 0))
    row = pl.BlockSpec((1, 1, d), lambda i, j: (i, 0, 0))
    const = lambda shape: pl.BlockSpec(shape, lambda i, j: (0,) * len(shape))
    return pl.pallas_call(
        functools.partial(_merge_kernel, chunks=chunks),
        grid=(b, t // tm),
        in_specs=[tok(d), tok(FOURIER_W), tok(GQ_W), tok(WQ_W), row, row, row, row, row,
                  const((1, d)), const((1, d)), const((d, GATE_W)), const((FOURIER_W, d)),
                  const((GQ_W, d)), const((WQ_W, d)), const((d, d)), const((d, LANES))],
        out_specs=[tok(d), tok(d), pl.BlockSpec((1, N_EXPERTS, tm), lambda i, j: (i, 0, j))],
        out_shape=[jax.ShapeDtypeStruct((b, t, d), F32), jax.ShapeDtypeStruct((b, t, d), BF16),
                   jax.ShapeDtypeStruct((b, N_EXPERTS, t), F32)],
        compiler_params=_cparams("arbitrary", "arbitrary"),
        name="merge",
    )(x, f, og, ow, sh1, sc1, g1, sh2, sc2, n1, n2, wgate, wbf, wbg, wbw, wout, wr)


def _cumsum_lanes(m, tri):
    e, n = m.shape
    nch = n // LANES
    stacked = jnp.concatenate([m[:, j * LANES:(j + 1) * LANES] for j in range(nch)], axis=0).astype(BF16)
    w = _dot(stacked, tri)
    outs, off = [], jnp.zeros((e, 1), F32)
    for j in range(nch):
        wj = w[j * e:(j + 1) * e]
        outs.append(wj + off)
        off = off + wj[:, LANES - 1:LANES]
    return jnp.concatenate(outs, axis=1)


def _route_kernel(aff_ref, tri_ref, pos_ref, rt_ref, *win_refs, cap, n_exp):
    aff = aff_ref[...]
    e, n = aff.shape
    thr_bits = jnp.zeros((e, 1), jnp.int32)
    for bit in range(30, -1, -1):
        cand = thr_bits | (1 << bit)
        cnt = jnp.sum(jnp.where(aff >= pltpu.bitcast(cand, F32), 1.0, 0.0), axis=1, keepdims=True)
        thr_bits = jnp.where(cnt >= cap, cand, thr_bits)
    ge = jnp.where(aff >= pltpu.bitcast(thr_bits, F32), 1.0, 0.0)
    gt = jnp.where(aff >= pltpu.bitcast(thr_bits + 1, F32), 1.0, 0.0)
    eq = ge - gt
    room = cap - jnp.sum(gt, axis=1, keepdims=True)
    tri = tri_ref[...]
    sel = gt + jnp.where(_cumsum_lanes(eq, tri) <= room, eq, 0.0)
    cums = _cumsum_lanes(sel, tri)
    pos = jnp.where(sel > 0.0, cums - 1.0, -1.0)
    pos_ref[...] = pos.astype(jnp.int32)
    weight = sel * aff
    pad = jnp.zeros((ROUTE_ROWS - 2 * n_exp, n), F32)
    for i in range(e // n_exp):
        rows = slice(i * n_exp, (i + 1) * n_exp)
        rt_ref[i] = jnp.concatenate([pos[rows], weight[rows], pad], axis=0).T
    if win_refs:
        win_ref, ok_ref = win_refs
        n_tiles = n // SCATTER_TILE
        ends = jnp.concatenate([cums[:, (k + 1) * SCATTER_TILE - 1:(k + 1) * SCATTER_TILE] for k in range(n_tiles)],
                               axis=1)
        starts = jnp.concatenate([jnp.zeros((e, 1), F32), ends[:, :n_tiles - 1]], axis=1)
        align = 16.0
        first = jnp.minimum(jnp.floor(starts / align) * align, float(cap - SCATTER_WINDOW))
        fits = jnp.where(ends - first <= SCATTER_WINDOW, 1.0, 0.0)
        win_ref[...] = first.astype(jnp.int32)
        ok = [jnp.min(fits[i * n_exp:(i + 1) * n_exp], axis=0, keepdims=True) for i in range(e // n_exp)]
        ok_ref[...] = jnp.concatenate(ok, axis=0).astype(jnp.int32)


def _route(aff, tri, *, cap, windows):
    b, e, n = aff.shape
    whole = lambda shape: pl.BlockSpec(shape, lambda i: (0,) * len(shape))
    out_shape = [jax.ShapeDtypeStruct((b * e, n), jnp.int32), jax.ShapeDtypeStruct((b, n, ROUTE_ROWS), F32)]
    if windows:
        n_tiles = n // SCATTER_TILE
        out_shape += [jax.ShapeDtypeStruct((b * e, n_tiles), jnp.int32), jax.ShapeDtypeStruct((b, n_tiles), jnp.int32)]
    outs = pl.pallas_call(
        functools.partial(_route_kernel, cap=cap, n_exp=e),
        grid=(1,),
        in_specs=[whole((b * e, n)), whole((LANES, LANES))],
        out_specs=[whole(s.shape) for s in out_shape],
        out_shape=out_shape,
        compiler_params=_cparams("arbitrary"),
        name="route",
    )(aff.reshape(b * e, n), tri)
    pos, rt = outs[0].reshape(b, e, n), outs[1]
    if not windows:
        return pos, rt
    win = outs[2].reshape(b, e, n_tiles).transpose(0, 2, 1)
    return pos, rt, win, outs[3]


def _gather_kernel(pos_ref, h_ref, o_ref, *, cap, group):
    j = pl.program_id(1)
    n = h_ref.shape[1]
    slot = lax.broadcasted_iota(jnp.int32, (cap, n), 0)
    sel = [jnp.where(pos_ref[0, pl.ds(j * group + g, 1), :] == slot, 1.0, 0.0).astype(BF16) for g in range(group)]
    o_ref[0] = _dot(jnp.concatenate(sel, axis=0), h_ref[0]).astype(BF16)


def _gather(pos, h, *, cap, group):
    b, e, n = pos.shape
    d = h.shape[2]
    return pl.pallas_call(
        functools.partial(_gather_kernel, cap=cap, group=group),
        grid=(b, e // group),
        in_specs=[pl.BlockSpec((1, e, n), lambda i, j: (i, 0, 0)), pl.BlockSpec((1, n, d), lambda i, j: (i, 0, 0))],
        out_specs=pl.BlockSpec((1, group * cap, d), lambda i, j: (i, j, 0)),
        out_shape=jax.ShapeDtypeStruct((b, e * cap, d), BF16),
        compiler_params=_cparams("arbitrary", "arbitrary"),
        name="moe_gather",
    )(pos, h)


def _gather_win_kernel(win_ref, ok_ref, pos_ref, h_ref, o_ref, *, tiles):
    i, j = pl.program_id(0), pl.program_id(1)
    cap = o_ref.shape[2]

    @pl.when(j == 0)
    def _():
        o_ref[...] = jnp.zeros_like(o_ref)

    kts = [j * tiles + u for u in range(tiles)]
    cols = [slice(u * SCATTER_TILE, (u + 1) * SCATTER_TILE) for u in range(tiles)]

    def windows():
        slot = lax.broadcasted_iota(jnp.int32, (SCATTER_WINDOW, SCATTER_TILE), 0)
        firsts, zs = [], []
        for kt, c in zip(kts, cols):
            pos = pos_ref[0, :, c]
            first = [win_ref[i, kt, e] for e in range(N_EXPERTS)]
            pick = [jnp.where(pos[e:e + 1] - first[e] == slot, 1.0, 0.0).astype(BF16) for e in range(N_EXPERTS)]
            zs.append(_dot(jnp.concatenate(pick, axis=0), h_ref[0, c]))
            firsts.append(first)
        for first, z in zip(firsts, zs):
            for e in range(N_EXPERTS):
                rows = pl.ds(pl.multiple_of(first[e], 16), SCATTER_WINDOW)
                o_ref[0, e, rows, :] += z[e * SCATTER_WINDOW:(e + 1) * SCATTER_WINDOW].astype(BF16)

    def full():
        slot = lax.broadcasted_iota(jnp.int32, (cap, SCATTER_TILE), 0)
        for c in cols:
            for e in range(N_EXPERTS):
                pick = jnp.where(pos_ref[0, e:e + 1, c] == slot, 1.0, 0.0).astype(BF16)
                o_ref[0, e] += _dot(pick, h_ref[0, c]).astype(BF16)

    all_fit = functools.reduce(jnp.minimum, [ok_ref[i, kt] for kt in kts])
    lax.cond(all_fit > 0, windows, full)


def _gather_win(win, ok, pos, h, *, cap, tiles):
    b, e, n = pos.shape
    d = h.shape[2]
    smem = pl.BlockSpec(memory_space=pltpu.SMEM)
    step = tiles * SCATTER_TILE
    out = pl.pallas_call(
        functools.partial(_gather_win_kernel, tiles=tiles),
        grid=(b, n // step),
        in_specs=[smem, smem,
                  pl.BlockSpec((1, e, step), lambda i, j: (i, 0, j)),
                  pl.BlockSpec((1, step, d), lambda i, j: (i, j, 0))],
        out_specs=pl.BlockSpec((1, e, cap, d), lambda i, j: (i, 0, 0, 0)),
        out_shape=jax.ShapeDtypeStruct((b, e, cap, d), BF16),
        compiler_params=_cparams("arbitrary", "arbitrary"),
        name="moe_gather_win",
    )(win, ok, pos, h)
    return out.reshape(b, e * cap, d)


def _expert_kernel(*refs, n_sets):
    x_refs, (wg_ref, wu_ref, wd_ref) = refs[:n_sets], refs[n_sets:n_sets + 3]
    o_refs = refs[n_sets + 3:2 * n_sets + 3]
    d = x_refs[0].shape[2]
    rows = [r.shape[0] * r.shape[1] for r in x_refs]
    xs = [r[...].reshape(n, d) for r, n in zip(x_refs, rows)]
    x = xs[0] if n_sets == 1 else jnp.concatenate(xs, axis=0)
    wg, wu, wd = (w[0, 0].astype(BF16) for w in (wg_ref, wu_ref, wd_ref))
    half = x.shape[0] // 2
    halves = [x[:half], x[half:]]
    up = [(_dot(xh, wg), _dot(xh, wu)) for xh in halves]
    y = jnp.concatenate([_dot((a * jax.nn.sigmoid(a) * u).astype(BF16), wd).astype(BF16) for a, u in up], axis=0)
    start = 0
    for o_ref, n in zip(o_refs, rows):
        o_ref[...] = y[start:start + n].reshape(o_ref.shape)
        start += n


def _experts(xgs, caps, layer, wg, wu, wd, *, nb):
    b, _, d = xgs[0].shape
    _, e, _, f = wg.shape
    wspec = lambda r, c: pl.BlockSpec((1, 1, r, c), lambda i, j: (layer, i, 0, 0))
    xspecs = [pl.BlockSpec((nb, cap, d), lambda i, j: (j, i, 0)) for cap in caps]
    return pl.pallas_call(
        functools.partial(_expert_kernel, n_sets=len(xgs)),
        grid=(e, b // nb),
        in_specs=xspecs + [wspec(d, f), wspec(d, f), wspec(f, d)],
        out_specs=xspecs,
        out_shape=[jax.ShapeDtypeStruct(xg.shape, BF16) for xg in xgs],
        compiler_params=_cparams("arbitrary", "arbitrary"),
        name="moe_experts",
    )(*xgs, wg, wu, wd)


def _combine_matrix(rt, first, width):
    slot = lax.broadcasted_iota(jnp.int32, (rt.shape[0], width), 1).astype(F32)
    cols = [jnp.where(rt[:, e:e + 1] - first[e] == slot, rt[:, N_EXPERTS + e:N_EXPERTS + e + 1], 0.0).astype(BF16)
            for e in range(N_EXPERTS)]
    return jnp.concatenate(cols, axis=1)


def _scatter_kernel(*refs, cap, final, windowed):
    if windowed:
        win_ref, ok_ref, x_ref, rt_ref, y_ref, g2_ref, fg_ref, o_ref = refs
    else:
        x_ref, rt_ref, y_ref, g2_ref, fg_ref, o_ref = refs
    i, j = pl.program_id(0), pl.program_id(1)
    tn = x_ref.shape[1]

    def finish(rows, y):
        x = x_ref[0, rows] + g2_ref[0] * y
        if final:
            ms = jnp.mean(x * x, axis=-1, keepdims=True)
            x = x * lax.rsqrt(ms + EPS) * fg_ref[...]
        o_ref[0, rows] = x

    def full():
        rows = slice(0, tn)
        finish(rows, _dot(_combine_matrix(rt_ref[0], [0.0] * N_EXPERTS, cap), y_ref[0]))

    if not windowed:
        full()
        return
    subs = tn // SCATTER_TILE
    tiles = [j * subs + u for u in range(subs)]

    def windows():
        def contract(u):
            rt = rt_ref[0, u * SCATTER_TILE:(u + 1) * SCATTER_TILE]
            first = [win_ref[i, tiles[u], e] for e in range(N_EXPERTS)]
            picked = [y_ref[0, pl.ds(pl.multiple_of(e * cap + first[e], 16), SCATTER_WINDOW), :]
                      for e in range(N_EXPERTS)]
            a = _combine_matrix(rt, [f.astype(F32) for f in first], SCATTER_WINDOW)
            return _dot(a, jnp.concatenate(picked, axis=0))

        ys = {0: contract(0)}
        for u in range(subs):
            if u + 1 < subs:
                ys[u + 1] = contract(u + 1)
            finish(slice(u * SCATTER_TILE, (u + 1) * SCATTER_TILE), ys.pop(u))

    all_fit = functools.reduce(jnp.minimum, [ok_ref[i, tile] for tile in tiles])
    lax.cond(all_fit > 0, windows, full)


def _scatter(x, rt, y, g2, fg, *, cap, final, tn, windows=None):
    b, t, d = x.shape
    smem = pl.BlockSpec(memory_space=pltpu.SMEM)
    windowed = windows is not None
    return pl.pallas_call(
        functools.partial(_scatter_kernel, cap=cap, final=final, windowed=windowed),
        grid=(b, t // tn),
        in_specs=([smem, smem] if windowed else []) + [
            pl.BlockSpec((1, tn, d), lambda i, j: (i, j, 0)),
            pl.BlockSpec((1, tn, ROUTE_ROWS), lambda i, j: (i, j, 0)),
            pl.BlockSpec((1, N_EXPERTS * cap, d), lambda i, j: (i, 0, 0)),
            pl.BlockSpec((1, 1, d), lambda i, j: (i, 0, 0)),
            pl.BlockSpec((1, d), lambda i, j: (0, 0))],
        out_specs=pl.BlockSpec((1, tn, d), lambda i, j: (i, j, 0)),
        out_shape=jax.ShapeDtypeStruct((b, t, d), F32),
        compiler_params=_cparams("arbitrary", "arbitrary"),
        name="moe_scatter",
    )(*(windows if windowed else ()), x, rt, y, g2, fg)


def _capacity(t):
    return CAPACITY_FACTOR * t // N_EXPERTS


def _dft_tables(n):
    k = np.arange(n, dtype=np.int64)
    ang = 2.0 * np.pi * ((k[:, None] * k[None, :]) % n).astype(np.float64) / n
    return np.cos(ang), np.sin(ang)


def _channel_dft():
    c, s = _dft_tables(FOURIER_GROUP_CH)
    eye = np.eye(FOURIER_GROUPS)
    return np.concatenate([np.kron(eye, c), np.kron(eye, s)], axis=1)


def _rope_tables(t):
    rows = t // GRID_W
    r, col = jnp.meshgrid(jnp.arange(rows), jnp.arange(GRID_W), indexing="ij")
    half = HEAD_DIM // 2
    inv = ROPE_THETA ** (-jnp.arange(0, half, 2, dtype=F32) / half)
    ang = jnp.concatenate([r.reshape(-1, 1).astype(F32) * inv, col.reshape(-1, 1).astype(F32) * inv], axis=-1)
    cos = jnp.repeat(jnp.cos(ang), 2, axis=1)
    sin = jnp.repeat(jnp.sin(ang), 2, axis=1) * jnp.tile(jnp.asarray([-1.0, 1.0], F32), half)
    return jnp.tile(cos, (1, HEAD_PAD)), jnp.tile(sin, (1, HEAD_PAD))


def kernel(x, c, ctx, c_ctx, w_ada, b_ada, norm1_g, w_in, q_norm_g, k_norm_g, sink, w_br_fourier, w_br_global,
           w_br_window, w_out, norm2_g, w_router, w_gate_e, w_up_e, w_down_e, final_g):
    b, t, d = x.shape
    l_ctx = ctx.shape[1]
    depth = w_ada.shape[0]
    cap_t, cap_c = _capacity(t), _capacity(l_ctx)

    cos_t, sin_t = _rope_tables(t)
    cos_c, sin_c = jnp.ones((l_ctx, LANES), F32), jnp.zeros((l_ctx, LANES), F32)
    head_avg = jnp.asarray(np.kron(np.eye(MXU_DIM // HEAD_DIM), np.full((HEAD_DIM, HEAD_DIM), 1.0 / HEAD_DIM)), BF16)
    cdft = jnp.asarray(_channel_dft(), F32)
    cn_t, sn_t = (jnp.asarray(a, F32) for a in _dft_tables(t))
    cn_c, sn_c = (jnp.asarray(a, F32) for a in _dft_tables(l_ctx))
    tri = jnp.asarray(np.triu(np.ones((LANES, LANES))), BF16)

    cvec = jnp.concatenate([c, c_ctx[None], jnp.zeros((MOD_ROWS - b - 1, d), F32)], axis=0)
    mods = _ada(cvec, w_ada, b_ada)

    xc = ctx
    for l in range(depth):
        need_ctx = l < depth - 1
        final = l == depth - 1
        lat = [mods[l, :b, i * d:(i + 1) * d].reshape(b, 1, d) for i in range(6)]
        cmod = [jnp.broadcast_to(mods[l, b, i * d:(i + 1) * d].reshape(1, 1, d), (b, 1, d)) for i in range(6)]
        w_mix = w_in[l, :, :MIX_W].astype(BF16)
        w_gate = w_in[l, :, MIX_W:].astype(BF16)
        n1, n2 = norm1_g[l].reshape(1, d), norm2_g[l].reshape(1, d)
        qn = jnp.tile(q_norm_g[l], G_HEADS).reshape(1, GQ_W)
        kn = jnp.tile(k_norm_g[l], G_KV).reshape(1, GKV_W)
        wbf, wbg, wbw = (w[l].astype(BF16) for w in (w_br_fourier, w_br_global, w_br_window))
        wout = w_out[l].astype(BF16)
        wr = jnp.pad(w_router[l], ((0, 0), (0, LANES - N_EXPERTS))).astype(BF16)
        fg = final_g.reshape(1, d)
        merge_w = (n1, n2, w_gate, wbf, wbg, wbw, wout, wr)

        cfcs, cqg, ckg, cvg, cqw, ckw, cvw = _inproj(xc, cmod[0], cmod[1], n1, w_mix, cos_c, sin_c, head_avg, cdft,
                                                     qn, kn, rope=False, tm=l_ctx, chunks=2)
        xgs, caps = [], []
        if need_ctx:
            cf_mix = _fourier(cfcs, cn_c, sn_c, tr=l_ctx)
            oc_g, oc_w = _ctx_attn(sink[l], cqg, ckg, cvg, cqw, ckw, cvw)
            xc1, hc2, caff = _merge(xc, cf_mix, oc_g, oc_w, *cmod[:5], *merge_w, tm=l_ctx, chunks=1)
            cpos, crt = _route(caff, tri, cap=cap_c, windows=False)
            xgs.append(_gather(cpos, hc2, cap=cap_c, group=N_EXPERTS))
            caps.append(cap_c)

        fcs, qg, kg, vg, qw, kw, vw = _inproj(x, lat[0], lat[1], n1, w_mix, cos_t, sin_t, head_avg, cdft, qn, kn,
                                              rope=True, tm=1024, chunks=8)
        f_mix = _fourier(fcs, cn_t, sn_t, tr=1024)
        o_g = _global_attn(qg, kg, vg, ckg, cvg, tq=256)
        o_w = _window_attn(sink[l], qw, kw, vw, ckw, cvw)
        x1, h2, aff = _merge(x, f_mix, o_g, o_w, *lat[:5], *merge_w, tm=1024, chunks=2)
        pos, rt, win, win_ok = _route(aff, tri, cap=cap_t, windows=True)
        xgs.insert(0, _gather_win(win, win_ok, pos, h2, cap=cap_t, tiles=8))
        caps.insert(0, cap_t)

        ys = _experts(xgs, caps, l, w_gate_e, w_up_e, w_down_e, nb=4)
        x = _scatter(x1, rt, ys[0], lat[5], fg, cap=cap_t, final=final, tn=1024, windows=(win, win_ok))
        if need_ctx:
            xc = _scatter(xc1, crt, ys[1], cmod[5], fg, cap=cap_c, final=False, tn=l_ctx)
    return x
```
